```python
import math
import jax
import jax.numpy as jnp
from jax import lax
import numpy as np

D_MODEL = 1024
BATCH = 1
SEQ = 16384
DEPTH = 2
DEC_BATCH = 32
DEC_SEQ = 4
PAST_LEN = 16384
PAGE_SIZE = 128

HEAD_DIM = 64
NSA_HEADS = 8
NSA_KV_HEADS = 2
NSA_GROUP = NSA_HEADS // NSA_KV_HEADS
NSA_CMP_LEN = 32
NSA_CMP_STRIDE = 16
NSA_PHI_HIDDEN = 128
NSA_SEL_BLOCK = 64
NSA_SEL_TOPK = 16
NSA_N_LOCAL = 2
NSA_WINDOW = 512
MOBA_HEADS = 8
MOBA_KV_HEADS = 4
MOBA_GROUP = MOBA_HEADS // MOBA_KV_HEADS
MOBA_BLOCK = 256
MOBA_TOPK = 3
S5_GROUPS = 32
S5_GROUP_CH = 16
S5_WIDTH = S5_GROUPS * S5_GROUP_CH
S5_STATE = 64
S5_DT_MIN = 0.001
S5_DT_MAX = 0.1
N_BRANCH = 3
D_FF = 2816
N_EXPERTS = 8
MOE_TOP_K = 2
D_FF_EXPERT = 1408
N_DENSE = (DEPTH + 1) // 2
N_MOE = DEPTH // 2
Q_BLOCK = 128
LN_EPS = 1e-5
DN_ALPHA = (2.0 * DEPTH) ** 0.25
DN_BETA = (8.0 * DEPTH) ** -0.25
NSA_Q_W = NSA_HEADS * HEAD_DIM
NSA_KV_W = NSA_KV_HEADS * HEAD_DIM
MOBA_Q_W = MOBA_HEADS * HEAD_DIM
MOBA_KV_W = MOBA_KV_HEADS * HEAD_DIM
IN_SPLITS = (NSA_Q_W, 6 * NSA_KV_W, 3 * NSA_HEADS, MOBA_Q_W, 2 * MOBA_KV_W, S5_WIDTH, N_BRANCH * D_MODEL)
D_IN = sum(IN_SPLITS)

kernel_name = 'nsa_moba_s5_gated_hybrid_step'


def layer_norm(x, g, b):
    xf = x.astype(jnp.float32)
    mu = jnp.mean(xf, axis=-1, keepdims=True)
    xc = xf - mu
    var = jnp.mean(xc * xc, axis=-1, keepdims=True)
    return (xc * lax.rsqrt(var + LN_EPS) * g + b).astype(x.dtype)


def alibi_slopes(n):
    return jnp.exp2(-8.0 * jnp.arange(1, n + 1, dtype=jnp.float32) / n)


def masked_softmax(s, mask, axis):
    s = jnp.where(mask, s, -jnp.inf)
    m = jnp.max(s, axis=axis, keepdims=True)
    m = jnp.where(jnp.isfinite(m), m, 0.0)
    e = jnp.where(mask, jnp.exp(s - m), 0.0)
    return e / jnp.maximum(jnp.sum(e, axis=axis, keepdims=True), 1e-30)


def swiglu(x, wg, wu, wd):
    return (jax.nn.silu(x @ wg) * (x @ wu)) @ wd


def nsa_compress(k, pe, w1, w2):
    B, L, G, hd = k.shape
    r = NSA_CMP_LEN // NSA_CMP_STRIDE
    n_chunk = L // NSA_CMP_STRIDE
    n_cmp = n_chunk - r + 1
    ch = k[:, :n_chunk * NSA_CMP_STRIDE].reshape(B, n_chunk, NSA_CMP_STRIDE, G, hd)
    blk = jnp.concatenate([ch[:, j:j + n_cmp] for j in range(r)], axis=2)
    blk = blk + pe[:, None, :]
    flat = jnp.moveaxis(blk, 3, 2).reshape(B, n_cmp, G, NSA_CMP_LEN * hd)
    return jax.nn.gelu(flat @ w1) @ w2


def _cplx_combine(left, right):
    ar1, ai1, br1, bi1 = left
    ar2, ai2, br2, bi2 = right
    return (ar2 * ar1 - ai2 * ai1, ar2 * ai1 + ai2 * ar1,
            ar2 * br1 - ai2 * bi1 + br2, ar2 * bi1 + ai2 * br1 + bi2)


def s5_mixer(u, h0_re, h0_im, lp):
    uf = u.astype(jnp.float32)
    lr = lp['s5_lambda_re'].astype(jnp.float32)
    li = lp['s5_lambda_im'].astype(jnp.float32)
    dt = jnp.exp(lp['s5_log_dt'].astype(jnp.float32))[:, None]
    mag = jnp.exp(lr * dt)
    a_re = mag * jnp.cos(li * dt)
    a_im = mag * jnp.sin(li * dt)
    den = lr * lr + li * li
    f_re = ((a_re - 1.0) * lr + a_im * li) / den
    f_im = (a_im * lr - (a_re - 1.0) * li) / den
    b_re = lp['s5_b_re'].astype(jnp.float32)
    b_im = lp['s5_b_im'].astype(jnp.float32)
    bb_re = f_re[..., None] * b_re - f_im[..., None] * b_im
    bb_im = f_re[..., None] * b_im + f_im[..., None] * b_re
    bu_re = jnp.einsum('btgh,gph->btgp', uf, bb_re)
    bu_im = jnp.einsum('btgh,gph->btgp', uf, bb_im)
    h0r = h0_re.astype(jnp.float32)
    h0i = h0_im.astype(jnp.float32)
    bu_re = bu_re.at[:, 0].add(a_re * h0r - a_im * h0i)
    bu_im = bu_im.at[:, 0].add(a_re * h0i + a_im * h0r)
    ar = jnp.broadcast_to(a_re, bu_re.shape)
    ai = jnp.broadcast_to(a_im, bu_re.shape)
    _, _, hr, hi = lax.associative_scan(_cplx_combine, (ar, ai, bu_re, bu_im), axis=1)
    y = (jnp.einsum('btgp,ghp->btgh', hr, lp['s5_c_re'].astype(jnp.float32))
         - jnp.einsum('btgp,ghp->btgh', hi, lp['s5_c_im'].astype(jnp.float32))
         + lp['s5_d'].astype(jnp.float32) * uf)
    return y, hr[:, -1], hi[:, -1]


def token_mixer(x, past_cmp, past_sel, win_buf, past_moba, h0_re, h0_im, lp):
    B, T, _ = x.shape
    L0 = past_cmp.shape[1]
    L = L0 + T
    hd = HEAD_DIM
    G, R = NSA_KV_HEADS, NSA_GROUP
    GB, RB = MOBA_KV_HEADS, MOBA_GROUP
    scale = hd ** -0.5
    f32 = jnp.float32

    proj = x @ lp['w_in']
    q_a, kv_a, g_a, q_b, kv_b, u, g_m = jnp.split(proj, np.cumsum(IN_SPLITS)[:-1].tolist(), axis=-1)
    q_a = q_a.reshape(B, T, G, R, hd)
    kv_a = kv_a.reshape(B, T, 3, 2, G, hd)
    g_a = jax.nn.sigmoid(g_a.astype(f32)).reshape(B, T, G, R, 3)
    q_b = q_b.reshape(B, T, GB, RB, hd)
    new_moba = kv_b.reshape(B, T, 2, GB, hd)
    new_cmp, new_sel, new_win = kv_a[:, :, 0], kv_a[:, :, 1], kv_a[:, :, 2]

    cmp_full = jnp.concatenate([past_cmp, new_cmp], axis=1)
    k_cmp = nsa_compress(cmp_full[:, :, 0], lp['pe_k'], lp['phi1_k'], lp['phi2_k'])
    v_cmp = nsa_compress(cmp_full[:, :, 1], lp['pe_v'], lp['phi1_v'], lp['phi2_v'])
    n_cmp = k_cmp.shape[1]
    cmp_start = jnp.arange(n_cmp) * NSA_CMP_STRIDE
    cmp_end = cmp_start + NSA_CMP_LEN - 1
    n_sel_blk = -(-L // NSA_SEL_BLOCK)
    sel_start = jnp.arange(n_sel_blk) * NSA_SEL_BLOCK
    overlap = ((cmp_start[:, None] + NSA_CMP_LEN > sel_start[None, :])
               & (cmp_start[:, None] < sel_start[None, :] + NSA_SEL_BLOCK)).astype(f32)
    n_top = min(NSA_SEL_TOPK, n_sel_blk)
    sel_full = jnp.concatenate([past_sel, new_sel], axis=1)
    sel_full = jnp.pad(sel_full, ((0, 0), (0, n_sel_blk * NSA_SEL_BLOCK - L), (0, 0), (0, 0), (0, 0)))
    sel_blk = sel_full.reshape(B, n_sel_blk, NSA_SEL_BLOCK, 2, G, hd).transpose(3, 0, 4, 1, 2, 5)
    wb = win_buf.shape[1]
    win_full = jnp.concatenate(
        [jnp.pad(win_buf, ((0, 0), (NSA_WINDOW - wb, 0), (0, 0), (0, 0), (0, 0))), new_win], axis=1)
    moba_full = jnp.concatenate([past_moba, new_moba], axis=1)
    n_mb = -(-L // MOBA_BLOCK)
    moba_full = jnp.pad(moba_full, ((0, 0), (0, n_mb * MOBA_BLOCK + MOBA_BLOCK - L), (0, 0), (0, 0), (0, 0)))
    mb = moba_full[:, :n_mb * MOBA_BLOCK].reshape(B, n_mb, MOBA_BLOCK, 2, GB, hd)
    k_mean = jnp.mean(mb[:, :, :, 0].astype(f32), axis=2)
    mb_t = mb.transpose(3, 0, 4, 1, 2, 5)
    k_top = min(MOBA_TOPK, n_mb)
    slopes_a = alibi_slopes(NSA_HEADS).reshape(G, R)
    slopes_b = alibi_slopes(MOBA_HEADS).reshape(GB, RB)
    qb_len = min(Q_BLOCK, T)
    n_qb = T // qb_len
    bi = jnp.arange(B)

    def attend_block(i):
        qs = i * qb_len
        qpos = L0 + qs + jnp.arange(qb_len)
        qa = lax.dynamic_slice_in_dim(q_a, qs, qb_len, axis=1)
        ga = lax.dynamic_slice_in_dim(g_a, qs, qb_len, axis=1)
        qm = lax.dynamic_slice_in_dim(q_b, qs, qb_len, axis=1)

        d_c = (qpos[:, None] - cmp_end[None, :]).astype(f32)
        s = (jnp.einsum('bqgrd,bcgd->bqgrc', qa, k_cmp).astype(f32) * scale
             - slopes_a[None, None, :, :, None] * d_c[None, :, None, None, :])
        p_c = masked_softmax(s, (d_c >= 0)[None, :, None, None, :], (-1,))
        o_c = jnp.einsum('bqgrc,bcgd->bqgrd', p_c, v_cmp)

        imp = jnp.einsum('bqgc,cj->bqgj', p_c.sum(axis=3), overlap)
        blk = jnp.arange(n_sel_blk)
        cur = qpos // NSA_SEL_BLOCK
        valid = sel_start[None, :] <= qpos[:, None]
        forced = (blk[None, :] == 0) | ((blk[None, :] <= cur[:, None]) & (blk[None, :] > cur[:, None] - NSA_N_LOCAL))
        score = jnp.where(forced[None, :, None, :], jnp.inf, jnp.where(valid[None, :, None, :], imp, -jnp.inf))
        top_s, top_i = lax.top_k(score, n_top)
        ok = top_s > -jnp.inf
        gidx = (bi[:, None, None, None], jnp.arange(G)[None, None, :, None], top_i)
        ks = sel_blk[0][gidx]
        vs = sel_blk[1][gidx]
        kpos = top_i[..., None] * NSA_SEL_BLOCK + jnp.arange(NSA_SEL_BLOCK)
        d_s = (qpos[None, :, None, None, None] - kpos).astype(f32)
        s = (jnp.einsum('bqgrd,bqgnsd->bqgrns', qa, ks).astype(f32) * scale
             - slopes_a[None, None, :, :, None, None] * d_s[:, :, :, None])
        m_s = ((d_s >= 0) & ok[..., None])[:, :, :, None]
        p_s = masked_softmax(s, m_s, (-2, -1))
        o_s = jnp.einsum('bqgrns,bqgnsd->bqgrd', p_s, vs)

        kw = lax.dynamic_slice_in_dim(win_full, qs, NSA_WINDOW + qb_len, axis=1)
        kwpos = L0 - NSA_WINDOW + qs + jnp.arange(NSA_WINDOW + qb_len)
        d_w = qpos[:, None] - kwpos[None, :]
        m_w = (d_w >= 0) & (d_w < NSA_WINDOW) & (kwpos[None, :] >= 0)
        s = (jnp.einsum('bqgrd,bkgd->bqgrk', qa, kw[:, :, 0]).astype(f32) * scale
             - slopes_a[None, None, :, :, None] * d_w.astype(f32)[None, :, None, None, :])
        p_w = masked_softmax(s, m_w[None, :, None, None, :], (-1,))
        o_w = jnp.einsum('bqgrk,bkgd->bqgrd', p_w, kw[:, :, 1])
        o_nsa = (ga[..., 0:1] * o_c + ga[..., 1:2] * o_s + ga[..., 2:3] * o_w).reshape(B, qb_len, NSA_Q_W)

        gs = jnp.einsum('bqgrd,bngd->bqgrn', qm.astype(f32), k_mean)
        own = qpos // MOBA_BLOCK
        past_ok = jnp.arange(n_mb)[None, :] < own[:, None]
        gs = jnp.where(past_ok[None, :, None, None, :], gs, -jnp.inf)
        top_s, top_i = lax.top_k(gs, k_top)
        ok = top_s > -jnp.inf
        midx = (bi[:, None, None, None, None], jnp.arange(GB)[None, None, :, None, None], top_i)
        km = mb_t[0][midx]
        vm = mb_t[1][midx]
        kpos = top_i[..., None] * MOBA_BLOCK + jnp.arange(MOBA_BLOCK)
        d_m = (qpos[None, :, None, None, None, None] - kpos).astype(f32)
        s_sel = (jnp.einsum('bqgrd,bqgrnsd->bqgrns', qm, km).astype(f32) * scale
                 - slopes_b[None, None, :, :, None, None] * d_m)
        m_sel = jnp.broadcast_to(ok[..., None], s_sel.shape)
        o_start = (qpos[0] // MOBA_BLOCK) * MOBA_BLOCK
        kvo = lax.dynamic_slice_in_dim(moba_full, o_start, MOBA_BLOCK + qb_len, axis=1)
        opos = o_start + jnp.arange(MOBA_BLOCK + qb_len)
        d_o = qpos[:, None] - opos[None, :]
        m_own = (d_o >= 0) & ((opos[None, :] // MOBA_BLOCK) == own[:, None])
        s_own = (jnp.einsum('bqgrd,bkgd->bqgrk', qm, kvo[:, :, 0]).astype(f32) * scale
                 - slopes_b[None, None, :, :, None] * d_o.astype(f32)[None, :, None, None, :])
        n_sk = k_top * MOBA_BLOCK
        s_all = jnp.concatenate([s_sel.reshape(B, qb_len, GB, RB, n_sk), s_own], axis=-1)
        m_all = jnp.concatenate([m_sel.reshape(B, qb_len, GB, RB, n_sk),
                                 jnp.broadcast_to(m_own[None, :, None, None, :], s_own.shape)], axis=-1)
        p = masked_softmax(s_all, m_all, (-1,))
        p_sel = p[..., :n_sk].reshape(B, qb_len, GB, RB, k_top, MOBA_BLOCK)
        o_m = (jnp.einsum('bqgrns,bqgrnsd->bqgrd', p_sel, vm)
               + jnp.einsum('bqgrk,bkgd->bqgrd', p[..., n_sk:], kvo[:, :, 1]))
        o_moba = o_m.reshape(B, qb_len, MOBA_Q_W)
        return o_nsa, o_moba

    o_nsa, o_moba = lax.map(attend_block, jnp.arange(n_qb))
    o_nsa = jnp.moveaxis(o_nsa, 0, 1).reshape(B, T, NSA_Q_W).astype(x.dtype)
    o_moba = jnp.moveaxis(o_moba, 0, 1).reshape(B, T, MOBA_Q_W).astype(x.dtype)

    y_s5, h_re, h_im = s5_mixer(u.reshape(B, T, S5_GROUPS, S5_GROUP_CH), h0_re, h0_im, lp)
    z = jax.nn.gelu(y_s5.reshape(B, T, S5_WIDTH)).astype(x.dtype)
    o_s5 = z * jax.nn.sigmoid(z @ lp['s5_w_glu'] + lp['s5_b_glu'])

    gates = jax.nn.sigmoid(g_m.astype(f32)).reshape(B, T, N_BRANCH, D_MODEL)
    merged = (gates[:, :, 0] * (o_nsa @ lp['w_br_nsa'])
              + gates[:, :, 1] * (o_moba @ lp['w_br_moba'])
              + gates[:, :, 2] * (o_s5 @ lp['w_br_s5']))
    out = merged.astype(x.dtype) @ lp['w_out']

    keep = wb if L0 > 0 else min(NSA_WINDOW, T)
    new_win_state = jnp.concatenate([win_buf, new_win], axis=1)[:, -keep:]
    return out, new_cmp, new_sel, new_win_state, new_moba, h_re, h_im


def moe_ffn(x, w_router, b_router, wg, wu, wd):
    logits = (x @ w_router).astype(jnp.float32) + b_router
    top_v, top_i = lax.top_k(logits, MOE_TOP_K)
    w = jax.nn.softmax(top_v, axis=-1)
    comb = jnp.sum(jax.nn.one_hot(top_i, N_EXPERTS, dtype=jnp.float32) * w[..., None], axis=-2)
    out = comb[..., 0:1] * swiglu(x, wg[0], wu[0], wd[0])
    for e in range(1, N_EXPERTS):
        out = out + comb[..., e:e + 1] * swiglu(x, wg[e], wu[e], wd[e])
    return out.astype(x.dtype)


def setup_inputs(seed: int = 0) -> dict:
    key = jax.random.key(seed)
    ks = iter(jax.random.split(key, 64))
    f32 = jnp.float32

    def nrm(shape, scale=1.0):
        return jax.random.normal(next(ks), shape, f32) * scale

    n_pages = PAST_LEN // PAGE_SIZE
    n_used = DEC_BATCH * n_pages
    n_pool = n_used + n_used // 4
    win_len = min(NSA_WINDOW, PAST_LEN)
    row_a = (2, NSA_KV_HEADS, HEAD_DIM)
    row_b = (2, MOBA_KV_HEADS, HEAD_DIM)
    gsp = (DEPTH, S5_GROUPS, S5_STATE)
    inp = {}
    inp['x_prompt'] = nrm((BATCH, SEQ, D_MODEL))
    inp['x_sample'] = nrm((DEC_BATCH, DEC_SEQ, D_MODEL))
    inp['cache_nsa_cmp'] = nrm((DEPTH, n_pool, PAGE_SIZE) + row_a)
    inp['cache_nsa_sel'] = nrm((DEPTH, n_pool, PAGE_SIZE) + row_a)
    inp['cache_nsa_win'] = nrm((DEPTH, DEC_BATCH, win_len) + row_a)
    inp['cache_moba'] = nrm((DEPTH, n_pool, PAGE_SIZE) + row_b)
    inp['state_s5_re'] = nrm((DEPTH, DEC_BATCH, S5_GROUPS, S5_STATE), 0.3)
    inp['state_s5_im'] = nrm((DEPTH, DEC_BATCH, S5_GROUPS, S5_STATE), 0.3)
    inp['page_table'] = jax.random.permutation(next(ks), n_pool)[:n_used].reshape(DEC_BATCH, n_pages).astype(jnp.int32)
    inp['w_in'] = nrm((DEPTH, D_MODEL, D_IN), D_MODEL ** -0.5)
    flat = NSA_CMP_LEN * HEAD_DIM
    inp['pe_k'] = nrm((DEPTH, NSA_CMP_LEN, HEAD_DIM), 0.02)
    inp['phi1_k'] = nrm((DEPTH, flat, NSA_PHI_HIDDEN), flat ** -0.5)
    inp['phi2_k'] = nrm((DEPTH, NSA_PHI_HIDDEN, HEAD_DIM), NSA_PHI_HIDDEN ** -0.5)
    inp['pe_v'] = nrm((DEPTH, NSA_CMP_LEN, HEAD_DIM), 0.02)
    inp['phi1_v'] = nrm((DEPTH, flat, NSA_PHI_HIDDEN), flat ** -0.5)
    inp['phi2_v'] = nrm((DEPTH, NSA_PHI_HIDDEN, HEAD_DIM), NSA_PHI_HIDDEN ** -0.5)
    inp['s5_lambda_re'] = -0.5 + nrm(gsp, 0.01)
    inp['s5_lambda_im'] = jnp.broadcast_to(math.pi * jnp.arange(S5_STATE, dtype=f32), gsp) + nrm(gsp, 0.01)
    inp['s5_log_dt'] = (math.log(S5_DT_MIN) + jax.random.uniform(next(ks), (DEPTH, S5_GROUPS), f32)
                        * (math.log(S5_DT_MAX) - math.log(S5_DT_MIN)))
    inp['s5_b_re'] = nrm((DEPTH, S5_GROUPS, S5_STATE, S5_GROUP_CH), S5_GROUP_CH ** -0.5)
    inp['s5_b_im'] = nrm((DEPTH, S5_GROUPS, S5_STATE, S5_GROUP_CH), S5_GROUP_CH ** -0.5)
    inp['s5_c_re'] = nrm((DEPTH, S5_GROUPS, S5_GROUP_CH, S5_STATE), S5_STATE ** -0.5)
    inp['s5_c_im'] = nrm((DEPTH, S5_GROUPS, S5_GROUP_CH, S5_STATE), S5_STATE ** -0.5)
    inp['s5_d'] = nrm((DEPTH, S5_GROUPS, S5_GROUP_CH))
    inp['s5_w_glu'] = nrm((DEPTH, S5_WIDTH, S5_WIDTH), S5_WIDTH ** -0.5)
    inp['s5_b_glu'] = nrm((DEPTH, S5_WIDTH), 0.01)
    inp['w_br_nsa'] = nrm((DEPTH, NSA_Q_W, D_MODEL), NSA_Q_W ** -0.5)
    inp['w_br_moba'] = nrm((DEPTH, MOBA_Q_W, D_MODEL), MOBA_Q_W ** -0.5)
    inp['w_br_s5'] = nrm((DEPTH, S5_WIDTH, D_MODEL), S5_WIDTH ** -0.5)
    inp['w_out'] = nrm((DEPTH, D_MODEL, D_MODEL), D_MODEL ** -0.5 * DN_BETA)
    inp['ln1_g'] = 1.0 + nrm((DEPTH, D_MODEL), 0.02)
    inp['ln1_b'] = nrm((DEPTH, D_MODEL), 0.02)
    inp['ln2_g'] = 1.0 + nrm((DEPTH, D_MODEL), 0.02)
    inp['ln2_b'] = nrm((DEPTH, D_MODEL), 0.02)
    inp['ffn_w_gate'] = nrm((N_DENSE, D_MODEL, D_FF), D_MODEL ** -0.5)
    inp['ffn_w_up'] = nrm((N_DENSE, D_MODEL, D_FF), D_MODEL ** -0.5)
    inp['ffn_w_down'] = nrm((N_DENSE, D_FF, D_MODEL), D_FF ** -0.5 * DN_BETA)
    inp['moe_w_router'] = nrm((N_MOE, D_MODEL, N_EXPERTS), D_MODEL ** -0.5)
    inp['moe_b_router'] = nrm((N_MOE, N_EXPERTS), 0.01)
    inp['moe_w_gate'] = nrm((N_MOE, N_EXPERTS, D_MODEL, D_FF_EXPERT), D_MODEL ** -0.5)
    inp['moe_w_up'] = nrm((N_MOE, N_EXPERTS, D_MODEL, D_FF_EXPERT), D_MODEL ** -0.5)
    inp['moe_w_down'] = nrm((N_MOE, N_EXPERTS, D_FF_EXPERT, D_MODEL), D_FF_EXPERT ** -0.5 * DN_BETA)
    return inp


def reference(x_prompt, x_sample, cache_nsa_cmp, cache_nsa_sel, cache_nsa_win, cache_moba, state_s5_re, state_s5_im,
              page_table, w_in, pe_k, phi1_k, phi2_k, pe_v, phi1_v, phi2_v, s5_lambda_re, s5_lambda_im, s5_log_dt,
              s5_b_re, s5_b_im, s5_c_re, s5_c_im, s5_d, s5_w_glu, s5_b_glu, w_br_nsa, w_br_moba, w_br_s5, w_out,
              ln1_g, ln1_b, ln2_g, ln2_b, ffn_w_gate, ffn_w_up, ffn_w_down, moe_w_router, moe_b_router,
              moe_w_gate, moe_w_up, moe_w_down):
    dec_b = x_sample.shape[0]
    past_len = page_table.shape[1] * PAGE_SIZE

    def paged(pool):
        return pool[page_table].reshape((dec_b, past_len) + pool.shape[2:])

    def layer(l, x, past_cmp, past_sel, win_buf, past_moba, h_re, h_im):
        lp = dict(w_in=w_in[l], pe_k=pe_k[l], phi1_k=phi1_k[l], phi2_k=phi2_k[l], pe_v=pe_v[l], phi1_v=phi1_v[l],
                  phi2_v=phi2_v[l], s5_lambda_re=s5_lambda_re[l], s5_lambda_im=s5_lambda_im[l],
                  s5_log_dt=s5_log_dt[l], s5_b_re=s5_b_re[l], s5_b_im=s5_b_im[l], s5_c_re=s5_c_re[l],
                  s5_c_im=s5_c_im[l], s5_d=s5_d[l], s5_w_glu=s5_w_glu[l], s5_b_glu=s5_b_glu[l],
                  w_br_nsa=w_br_nsa[l], w_br_moba=w_br_moba[l], w_br_s5=w_br_s5[l], w_out=w_out[l])
        mix, n_cmp, n_sel, n_win, n_moba, hr, hi = token_mixer(x, past_cmp, past_sel, win_buf, past_moba, h_re, h_im, lp)
        x = layer_norm(DN_ALPHA * x + mix, ln1_g[l], ln1_b[l])
        if l % 2 == 0:
            f = swiglu(x, ffn_w_gate[l // 2], ffn_w_up[l // 2], ffn_w_down[l // 2])
        else:
            f = moe_ffn(x, moe_w_router[l // 2], moe_b_router[l // 2], moe_w_gate[l // 2], moe_w_up[l // 2],
                        moe_w_down[l // 2])
        x = layer_norm(DN_ALPHA * x + f, ln2_g[l], ln2_b[l])
        return x, (n_cmp, n_sel, n_win, n_moba, hr, hi)

    bp = x_prompt.shape[0]
    empty_a = jnp.zeros((bp, 0, 2, NSA_KV_HEADS, HEAD_DIM), x_prompt.dtype)
    empty_b = jnp.zeros((bp, 0, 2, MOBA_KV_HEADS, HEAD_DIM), x_prompt.dtype)
    zero_h = jnp.zeros((bp, S5_GROUPS, S5_STATE), jnp.float32)
    y_prompt, y_sample = x_prompt, x_sample
    p_st, s_st = [], []
    for l in range(DEPTH):
        y_prompt, st = layer(l, y_prompt, empty_a, empty_a, empty_a, empty_b, zero_h, zero_h)
        p_st.append(st)
        y_sample, st = layer(l, y_sample, paged(cache_nsa_cmp[l]), paged(cache_nsa_sel[l]), cache_nsa_win[l],
                             paged(cache_moba[l]), state_s5_re[l], state_s5_im[l])
        s_st.append(st)

    def stk(states, j):
        return jnp.stack([st[j] for st in states], axis=0)

    return (y_prompt, y_sample,
            stk(p_st, 0), stk(p_st, 1), stk(p_st, 2), stk(p_st, 3), stk(p_st, 4), stk(p_st, 5),
            stk(s_st, 0), stk(s_st, 1), stk(s_st, 2), stk(s_st, 3), stk(s_st, 4), stk(s_st, 5))
```

```python
import functools
import math

import numpy as np
import jax
import jax.numpy as jnp
from jax import lax
from jax.experimental import pallas as pl
from jax.experimental.pallas import tpu as pltpu

F32 = jnp.float32
BF16 = jnp.bfloat16

D_MODEL = 1024
HEAD_DIM = 64
PAGE_SIZE = 128
NSA_HEADS = 8
NSA_KV_HEADS = 2
NSA_CMP_LEN = 32
NSA_CMP_STRIDE = 16
NSA_PHI_HIDDEN = 128
NSA_SEL_BLOCK = 64
NSA_SEL_TOPK = 16
NSA_N_LOCAL = 2
NSA_WINDOW = 512
MOBA_HEADS = 8
MOBA_KV_HEADS = 4
MOBA_BLOCK = 256
MOBA_TOPK = 3
S5_GROUPS = 32
S5_GROUP_CH = 16
S5_WIDTH = S5_GROUPS * S5_GROUP_CH
S5_STATE = 64
S5_LANES = S5_GROUPS * S5_STATE
N_BRANCH = 3
N_EXPERTS = 8
MOE_TOP_K = 2
LN_EPS = 1e-5
DEPTH = 2
DN_ALPHA = (2.0 * DEPTH) ** 0.25

NSA_Q_W = NSA_HEADS * HEAD_DIM
NSA_KV_W = NSA_KV_HEADS * HEAD_DIM
MOBA_Q_W = MOBA_HEADS * HEAD_DIM
MOBA_KV_W = MOBA_KV_HEADS * HEAD_DIM
IN_SPLITS = (NSA_Q_W, 6 * NSA_KV_W, 3 * NSA_HEADS, MOBA_Q_W, 2 * MOBA_KV_W, S5_WIDTH, N_BRANCH * D_MODEL)
IN_OFFS = tuple(int(v) for v in np.cumsum((0,) + IN_SPLITS))

LANES = 128
SUBLANES = 8
VMEM_LIMIT = 56 * 1024 * 1024

PC_QB = 0
PC_QA = PC_QB + MOBA_HEADS * MOBA_KV_W
PC_GM = PC_QA + NSA_HEADS * NSA_KV_W
PC_GA = PC_GM + N_BRANCH * D_MODEL
PC_U = PC_GA + 3 * NSA_Q_W
PC_KVB = PC_U + S5_WIDTH
PC_KVA = PC_KVB + 2 * MOBA_KV_W
PC_END = PC_KVA + 6 * NSA_KV_W

NEG = -1e30
SLOPES_A = tuple(float(2.0 ** (-8.0 * (i + 1) / NSA_HEADS)) for i in range(NSA_HEADS))
SLOPES_B = tuple(float(2.0 ** (-8.0 * (i + 1) / MOBA_HEADS)) for i in range(MOBA_HEADS))


def _cparams(*sem):
    return pltpu.CompilerParams(dimension_semantics=sem, vmem_limit_bytes=VMEM_LIMIT)


def _round_up(n, m):
    return -(-n // m) * m


def _bf(x):
    return x.astype(BF16)


def _dot(a, b):
    return jnp.dot(_bf(a), _bf(b), preferred_element_type=F32)


def _dot_nt(a, b):
    return lax.dot_general(_bf(a), _bf(b), (((1,), (1,)), ((), ())), preferred_element_type=F32)


def _split(a):
    hi = a.astype(BF16)
    lo = (a - hi.astype(F32)).astype(BF16)
    return hi, lo


def _dot3(a, b):
    ah, al = _split(a)
    bh, bl = _split(b)
    d = functools.partial(jnp.dot, preferred_element_type=F32)
    return d(ah, bh) + (d(ah, bl) + d(al, bh))


def _dot3_nt(a, b):
    ah, al = _split(a)
    bh, bl = _split(b)
    d = functools.partial(lax.dot_general, dimension_numbers=(((1,), (1,)), ((), ())),
                          preferred_element_type=F32)
    return d(ah, bh) + (d(ah, bl) + d(al, bh))


def _dot2_exact_rhs(a, b_bf16):
    ah, al = _split(a)
    d = functools.partial(jnp.dot, preferred_element_type=F32)
    return d(ah, b_bf16) + d(al, b_bf16)


def _sigmoid(x):
    return 1.0 / (1.0 + jnp.exp(-x))


def _gelu(x):
    c = math.sqrt(2.0 / math.pi)
    return x * (0.5 * (1.0 + jnp.tanh(c * (x + 0.044715 * (x * x * x)))))


def _layer_norm(v, g, b):
    mu = jnp.mean(v, axis=-1, keepdims=True)
    vc = v - mu
    var = jnp.mean(vc * vc, axis=-1, keepdims=True)
    return vc * lax.rsqrt(var + LN_EPS) * g + b


def _mm_kernel(x_ref, w_ref, o_ref):
    o_ref[...] = jnp.dot(_bf(x_ref[...]), w_ref[...], preferred_element_type=F32)


def _matmul(x, w, tm, tn):
    m, k = x.shape
    n = w.shape[1]
    return pl.pallas_call(
        _mm_kernel,
        grid=(m // tm, n // tn),
        in_specs=[pl.BlockSpec((tm, k), lambda i, j: (i, 0)),
                  pl.BlockSpec((k, tn), lambda i, j: (0, j))],
        out_specs=pl.BlockSpec((tm, tn), lambda i, j: (i, j)),
        out_shape=jax.ShapeDtypeStruct((m, n), F32),
        compiler_params=_cparams("parallel", "arbitrary"),
        name="proj",
    )(x, w)


def _proj_weight(w_in):
    scale = HEAD_DIM ** -0.5
    qa = w_in[:, IN_OFFS[0]:IN_OFFS[1]].reshape(D_MODEL, NSA_HEADS, 1, HEAD_DIM) * scale
    oh_a = np.zeros((NSA_HEADS, NSA_KV_HEADS, 1), np.float32)
    for h in range(NSA_HEADS):
        oh_a[h, h // (NSA_HEADS // NSA_KV_HEADS)] = 1.0
    qa = (qa * oh_a[None]).reshape(D_MODEL, NSA_HEADS * NSA_KV_W)
    qb = w_in[:, IN_OFFS[3]:IN_OFFS[4]].reshape(D_MODEL, MOBA_HEADS, 1, HEAD_DIM) * scale
    oh_b = np.zeros((MOBA_HEADS, MOBA_KV_HEADS, 1), np.float32)
    for h in range(MOBA_HEADS):
        oh_b[h, h // (MOBA_HEADS // MOBA_KV_HEADS)] = 1.0
    qb = (qb * oh_b[None]).reshape(D_MODEL, MOBA_HEADS * MOBA_KV_W)
    ga = w_in[:, IN_OFFS[2]:IN_OFFS[3]].reshape(D_MODEL, NSA_HEADS, 3).transpose(0, 2, 1)
    ga = jnp.broadcast_to(ga[..., None], (D_MODEL, 3, NSA_HEADS, HEAD_DIM)).reshape(D_MODEL, 3 * NSA_Q_W)
    gm = w_in[:, IN_OFFS[6]:IN_OFFS[7]]
    u = w_in[:, IN_OFFS[5]:IN_OFFS[6]]
    kvb = w_in[:, IN_OFFS[4]:IN_OFFS[5]]
    kva = w_in[:, IN_OFFS[1]:IN_OFFS[2]]
    return jnp.concatenate([qb, qa, gm, ga, u, kvb, kva], axis=1).astype(BF16)


def _cmpab_kernel(*refs, n_x, n_prefetch=0):
    refs = refs[n_prefetch:]
    x_refs, w_ref, o_ref = refs[:n_x], refs[n_x], refs[n_x + 1]
    row_w = 2 * NSA_KV_W
    xs = [r[...].reshape(-1, NSA_CMP_STRIDE * row_w) for r in x_refs]
    x = xs[0] if n_x == 1 else jnp.concatenate(xs, axis=0)
    for kv in range(2):
        xk = jnp.concatenate(
            [x[:, r * row_w + kv * NSA_KV_W: r * row_w + (kv + 1) * NSA_KV_W] for r in range(NSA_CMP_STRIDE)],
            axis=1)
        o_ref[:, kv * 4 * NSA_PHI_HIDDEN:(kv + 1) * 4 * NSA_PHI_HIDDEN] = _dot(xk, w_ref[kv])


def _cmp_weights(phi1_k, phi1_v):
    eye = np.eye(NSA_KV_HEADS, dtype=np.float32)
    out = []
    for phi in (phi1_k, phi1_v):
        halves = []
        for half in range(NSA_CMP_LEN // NSA_CMP_STRIDE):
            w = phi[half * NSA_CMP_STRIDE * HEAD_DIM:(half + 1) * NSA_CMP_STRIDE * HEAD_DIM]
            w = w.reshape(NSA_CMP_STRIDE, 1, HEAD_DIM, 1, NSA_PHI_HIDDEN)
            w = w * eye[None, :, None, :, None]
            halves.append(w.reshape(NSA_CMP_STRIDE * NSA_KV_W, NSA_KV_HEADS * NSA_PHI_HIDDEN))
        out.append(jnp.concatenate(halves, axis=1))
    return jnp.stack(out).astype(BF16)


def _cmp_ab_linear(x, w):
    mc = x.shape[0]
    tmc = min(mc, 256)
    n_out = 8 * NSA_PHI_HIDDEN
    return pl.pallas_call(
        functools.partial(_cmpab_kernel, n_x=1),
        grid=(mc // tmc,),
        in_specs=[pl.BlockSpec((tmc, x.shape[1]), lambda i: (i, 0)),
                  pl.BlockSpec(w.shape, lambda i: (0, 0, 0))],
        out_specs=pl.BlockSpec((tmc, n_out), lambda i: (i, 0)),
        out_shape=jax.ShapeDtypeStruct((mc, n_out), F32),
        compiler_params=_cparams("parallel"),
        name="cmp_ab",
    )(x, w)


CMP_PAGES_PER_STEP = 16


def _cmp_ab_paged(pool, page_table, w):
    n_pool = pool.shape[0]
    b, n_pages = page_table.shape
    cpp = PAGE_SIZE // NSA_CMP_STRIDE
    pool_c = pool.reshape(n_pool, cpp, NSA_CMP_STRIDE * pool.shape[2])
    npp = math.gcd(CMP_PAGES_PER_STEP, n_pages)
    steps = n_pages // npp
    n_out = 8 * NSA_PHI_HIDDEN

    def x_map(k):
        return lambda bi, s, pt: (pt[bi * n_pages + s * npp + k], 0, 0)

    grid_spec = pltpu.PrefetchScalarGridSpec(
        num_scalar_prefetch=1,
        grid=(b, steps),
        in_specs=[pl.BlockSpec((1,) + pool_c.shape[1:], x_map(k)) for k in range(npp)]
        + [pl.BlockSpec(w.shape, lambda bi, s, pt: (0, 0, 0))],
        out_specs=pl.BlockSpec((npp * cpp, n_out), lambda bi, s, pt: (bi * steps + s, 0)),
    )
    return pl.pallas_call(
        functools.partial(_cmpab_kernel, n_x=npp, n_prefetch=1),
        grid_spec=grid_spec,
        out_shape=jax.ShapeDtypeStruct((b * n_pages * cpp, n_out), F32),
        compiler_params=_cparams("parallel", "arbitrary"),
        name="cmp_ab_paged",
    )(page_table.reshape(-1), *([pool_c] * npp), w)


def _cmpfin_kernel(ab_ref, pe_ref, w1_ref, w2_ref, o_ref):
    mc = ab_ref.shape[0]
    hid2 = NSA_KV_HEADS * NSA_PHI_HIDDEN
    for kv in range(2):
        c = _dot(pe_ref[kv], w1_ref[kv])[0:1]
        c = jnp.concatenate([c] * NSA_KV_HEADS, axis=1)
        a = ab_ref[:, kv * 2 * hid2: kv * 2 * hid2 + hid2]
        bnext = pltpu.roll(ab_ref[:, kv * 2 * hid2 + hid2:(kv + 1) * 2 * hid2], mc - 1, 0)
        hpre = a + bnext + c
        o_ref[:, kv * NSA_KV_W:(kv + 1) * NSA_KV_W] = _dot(_gelu(hpre), w2_ref[kv])


def _cmp_finish(ab, nb, pe_k, pe_v, phi1_k, phi1_v, phi2_k, phi2_v):
    mc = ab.shape[0] // nb
    flat = NSA_CMP_LEN * HEAD_DIM
    pe = jnp.stack([jnp.broadcast_to(p.reshape(1, flat), (SUBLANES, flat)) for p in (pe_k, pe_v)])
    w1 = jnp.stack([phi1_k, phi1_v]).astype(BF16)
    eye = np.eye(NSA_KV_HEADS, dtype=np.float32)
    w2 = jnp.stack([(p[None, :, None, :] * eye[:, None, :, None]).reshape(NSA_KV_HEADS * NSA_PHI_HIDDEN, NSA_KV_W)
                    for p in (phi2_k, phi2_v)]).astype(BF16)
    out = pl.pallas_call(
        _cmpfin_kernel,
        grid=(nb,),
        in_specs=[pl.BlockSpec((mc, ab.shape[1]), lambda i: (i, 0)),
                  pl.BlockSpec(pe.shape, lambda i: (0, 0, 0)),
                  pl.BlockSpec(w1.shape, lambda i: (0, 0, 0)),
                  pl.BlockSpec(w2.shape, lambda i: (0, 0, 0))],
        out_specs=pl.BlockSpec((mc, 2 * NSA_KV_W), lambda i: (i, 0)),
        out_shape=jax.ShapeDtypeStruct((nb * mc, 2 * NSA_KV_W), F32),
        compiler_params=_cparams("parallel"),
        name="cmp_finish",
    )(ab, pe, w1, w2)
    return out.reshape(nb, mc, 2 * NSA_KV_W)


def _stack_heads(q, n_heads, w):
    return jnp.concatenate([_bf(q[:, h * w:(h + 1) * w]) for h in range(n_heads)], axis=0)


def _place_heads(o_heads, n_heads, n_groups, o_ref):
    hpg = n_heads // n_groups
    tq = o_heads[0].shape[0]
    lane = lax.broadcasted_iota(jnp.int32, (tq, LANES), 1)
    for pair in range(n_heads // 2):
        pieces = []
        for h in (2 * pair, 2 * pair + 1):
            src = (h // hpg) * HEAD_DIM
            piece = o_heads[h][:, (src // LANES) * LANES:(src // LANES + 1) * LANES]
            if src % LANES != (h % 2) * HEAD_DIM:
                piece = pltpu.roll(piece, HEAD_DIM, 1)
            pieces.append(piece)
        o_ref[0, :, pair * LANES:(pair + 1) * LANES] = jnp.where(lane < HEAD_DIM, pieces[0], pieces[1])


def _top_select(score, lanef, n_pick):
    sel = jnp.zeros(score.shape, F32)
    for _ in range(n_pick):
        m = jnp.max(score, axis=-1, keepdims=True)
        idx = jnp.min(jnp.where(score == m, lanef, 1e9), axis=-1, keepdims=True)
        pick = lanef == idx
        sel = jnp.where(pick & (m > -jnp.inf), 1.0, sel)
        score = jnp.where(pick, -jnp.inf, score)
    return sel


def _cmpattn_kernel(q_ref, kc_ref, ov_ref, o_ref, sel_ref, *, tq, qpos0, n_cmp, n_sel_blk):
    w = NSA_KV_W
    hpg = NSA_HEADS // NSA_KV_HEADS
    ncp = kc_ref.shape[1]
    nbp = ov_ref.shape[1]
    qbase = qpos0 + pl.program_id(1) * tq
    qs = _stack_heads(q_ref[0], NSA_HEADS, w)
    kc = kc_ref[0]
    s_all = _dot_nt(qs, kc[:, :w])
    c_idx = lax.broadcasted_iota(jnp.int32, (1, ncp), 1)
    row = lax.broadcasted_iota(jnp.int32, (tq, 1), 0)
    cend_rel = c_idx * NSA_CMP_STRIDE + (NSA_CMP_LEN - 1) - qbase
    valid = (row - cend_rel >= 0) & (c_idx < n_cmp)
    colf = cend_rel.astype(F32)
    ps, psums = [], []
    for h in range(NSA_HEADS):
        s = jnp.where(valid, s_all[h * tq:(h + 1) * tq] + SLOPES_A[h] * colf, NEG)
        m = jnp.max(s, axis=-1, keepdims=True)
        e = jnp.where(valid, jnp.exp(s - m), 0.0)
        p = e / jnp.maximum(jnp.sum(e, axis=-1, keepdims=True), 1e-30)
        ps.append(_bf(p))
        if h % hpg == 0:
            psums.append(p)
        else:
            psums[-1] = psums[-1] + p
    o_all = jnp.dot(jnp.concatenate(ps, axis=0), _bf(kc[:, w:]), preferred_element_type=F32)
    _place_heads([o_all[h * tq:(h + 1) * tq] for h in range(NSA_HEADS)], NSA_HEADS, NSA_KV_HEADS, o_ref)

    lane = lax.broadcasted_iota(jnp.int32, (tq, nbp), 1)
    lanef = lane.astype(F32)
    cur = (qbase + row) // NSA_SEL_BLOCK
    forced = (lane == 0) | ((lane <= cur) & (lane > cur - NSA_N_LOCAL))
    validb = (lane <= cur) & (lane < n_sel_blk)
    n_top = min(NSA_SEL_TOPK, n_sel_blk)
    for g in range(NSA_KV_HEADS):
        imp = _dot2_exact_rhs(psums[g], ov_ref[...])
        score = jnp.where(forced, jnp.inf, jnp.where(validb, imp, -jnp.inf))
        sel_ref[0, g] = _top_select(score, lanef, n_top)


def _cmp_attention(proj3, kc, qpos0, t_real, tq):
    b, tp, _ = proj3.shape
    ncp = kc.shape[1]
    seq = qpos0 + t_real
    n_cmp = seq // NSA_CMP_STRIDE - NSA_CMP_LEN // NSA_CMP_STRIDE + 1
    n_sel_blk = -(-seq // NSA_SEL_BLOCK)
    nbp = _round_up(n_sel_blk, LANES)
    cs = np.arange(ncp)[:, None] * NSA_CMP_STRIDE
    ss = np.arange(nbp)[None, :] * NSA_SEL_BLOCK
    ov = ((cs + NSA_CMP_LEN > ss) & (cs < ss + NSA_SEL_BLOCK)
          & (np.arange(ncp)[:, None] < n_cmp) & (np.arange(nbp)[None, :] < n_sel_blk))
    ov = jnp.asarray(ov.astype(np.float32), BF16)
    qw = NSA_HEADS * NSA_KV_W
    return pl.pallas_call(
        functools.partial(_cmpattn_kernel, tq=tq, qpos0=qpos0, n_cmp=n_cmp, n_sel_blk=n_sel_blk),
        grid=(b, tp // tq),
        in_specs=[pl.BlockSpec((1, tq, qw), lambda bi, i: (bi, i, PC_QA // qw)),
                  pl.BlockSpec((1, ncp, 2 * NSA_KV_W), lambda bi, i: (bi, 0, 0)),
                  pl.BlockSpec(ov.shape, lambda bi, i: (0, 0))],
        out_specs=[pl.BlockSpec((1, tq, NSA_Q_W), lambda bi, i: (bi, i, 0)),
                   pl.BlockSpec((1, NSA_KV_HEADS, tq, nbp), lambda bi, i: (bi, 0, i, 0))],
        out_shape=[jax.ShapeDtypeStruct((b, tp, NSA_Q_W), F32),
                   jax.ShapeDtypeStruct((b, NSA_KV_HEADS, tp, nbp), F32)],
        compiler_params=_cparams("parallel", "arbitrary"),
        name="cmp_attn",
    )(proj3, kc, ov)


def _kmean_kernel(*refs, n_x, n_prefetch=0):
    refs = refs[n_prefetch:]
    x_refs, o_ref = refs[:n_x], refs[n_x]
    rows_per = x_refs[0].shape[1]
    per_blk = MOBA_BLOCK // rows_per
    outs = []
    for i in range(n_x // per_blk):
        acc = None
        for k in range(per_blk):
            part = jnp.sum(x_refs[i * per_blk + k][0], axis=0, keepdims=True)
            acc = part if acc is None else acc + part
        outs.append(acc * (1.0 / MOBA_BLOCK))
    o_ref[0, 0] = outs[0] if len(outs) == 1 else jnp.concatenate(outs, axis=0)


KMEAN_BLOCKS_PER_STEP = 4


def _moba_kmean(rows3, page_table):
    w = MOBA_KV_W
    if page_table is None:
        b, t, _ = rows3.shape
        n_blk = t // MOBA_BLOCK
        bps = math.gcd(KMEAN_BLOCKS_PER_STEP, n_blk)
        steps = n_blk // bps
        out = pl.pallas_call(
            functools.partial(_kmean_kernel, n_x=bps),
            grid=(b, steps),
            in_specs=[pl.BlockSpec((1, MOBA_BLOCK, w), (lambda bi, s, k=k: (bi, s * bps + k, PC_KVB // w)))
                      for k in range(bps)],
            out_specs=pl.BlockSpec((1, 1, bps, w), lambda bi, s: (bi, s, 0, 0)),
            out_shape=jax.ShapeDtypeStruct((b, steps, bps, w), F32),
            compiler_params=_cparams("parallel", "arbitrary"),
            name="kmean",
        )(*([rows3] * bps))
        return out.reshape(b, n_blk, w)
    b, n_pages = page_table.shape
    ppb = MOBA_BLOCK // PAGE_SIZE
    n_blk = n_pages // ppb
    bps = math.gcd(KMEAN_BLOCKS_PER_STEP, n_blk)
    steps = n_blk // bps
    npp = bps * ppb

    def x_map(k):
        return lambda bi, s, pt: (pt[bi * n_pages + s * npp + k], 0, 0)

    grid_spec = pltpu.PrefetchScalarGridSpec(
        num_scalar_prefetch=1,
        grid=(b, steps),
        in_specs=[pl.BlockSpec((1, PAGE_SIZE, w), x_map(k)) for k in range(npp)],
        out_specs=pl.BlockSpec((1, 1, bps, w), lambda bi, s, pt: (bi, s, 0, 0)),
    )
    out = pl.pallas_call(
        functools.partial(_kmean_kernel, n_x=npp, n_prefetch=1),
        grid_spec=grid_spec,
        out_shape=jax.ShapeDtypeStruct((b, steps, bps, w), F32),
        compiler_params=_cparams("parallel", "arbitrary"),
        name="kmean_paged",
    )(page_table.reshape(-1), *([rows3] * npp))
    return out.reshape(b, n_blk, w)


def _mobagate_kernel(q_ref, km_ref, m_ref, *, tq, qpos0, k_top):
    w = MOBA_KV_W
    nbp = km_ref.shape[1]
    qbase = qpos0 + pl.program_id(1) * tq
    q = q_ref[0]
    qs = jnp.concatenate([q[:, h * w:(h + 1) * w] for h in range(MOBA_HEADS)], axis=0)
    gs = _dot3_nt(qs, km_ref[0])
    lane = lax.broadcasted_iota(jnp.int32, (tq, nbp), 1)
    lanef = lane.astype(F32)
    own = (qbase + lax.broadcasted_iota(jnp.int32, (tq, 1), 0)) // MOBA_BLOCK
    for h in range(MOBA_HEADS):
        score = jnp.where(lane < own, gs[h * tq:(h + 1) * tq], -jnp.inf)
        sel = _top_select(score, lanef, k_top)
        m_ref[0, h] = jnp.where(lane == own, 1.0, sel)


def _moba_gate(proj3, kmean, qpos0, t_real, tq):
    b, tp, _ = proj3.shape
    n_mb = -(-(qpos0 + t_real) // MOBA_BLOCK)
    nbp = _round_up(n_mb, LANES)
    km = jnp.pad(kmean, ((0, 0), (0, nbp - kmean.shape[1]), (0, 0)))
    qw = MOBA_HEADS * MOBA_KV_W
    return pl.pallas_call(
        functools.partial(_mobagate_kernel, tq=tq, qpos0=qpos0, k_top=min(MOBA_TOPK, n_mb)),
        grid=(b, tp // tq),
        in_specs=[pl.BlockSpec((1, tq, qw), lambda bi, i: (bi, i, PC_QB // qw)),
                  pl.BlockSpec((1, nbp, MOBA_KV_W), lambda bi, i: (bi, 0, 0))],
        out_specs=pl.BlockSpec((1, MOBA_HEADS, tq, nbp), lambda bi, i: (bi, 0, i, 0)),
        out_shape=jax.ShapeDtypeStruct((b, MOBA_HEADS, tp, nbp), F32),
        compiler_params=_cparams("parallel", "arbitrary"),
        name="moba_gate",
    )(proj3, km)


def _attend_tile(qs, k, v, mask, kbase, qbase, m_sc, l_sc, acc_sc, *, n_heads, n_groups, n_mask, tq,
                 blk_shift, window, slopes):
    tk = k.shape[0]
    s_all = _dot_nt(qs, k)
    delta = qbase - kbase
    col1 = lax.broadcasted_iota(jnp.int32, (1, tk), 1)
    dd = lax.broadcasted_iota(jnp.int32, (tq, tk), 0) - lax.broadcasted_iota(jnp.int32, (tq, tk), 1)
    ok = dd >= -delta
    if window is not None:
        ok = ok & (dd < window - delta)
    if n_mask:
        nbp = mask.shape[-1]
        blk_col = (kbase + col1) >> blk_shift
        expand = jnp.where(lax.broadcasted_iota(jnp.int32, (nbp, tk), 0) == blk_col, 1.0, 0.0).astype(BF16)
        mexp = jnp.dot(_bf(mask.reshape(n_mask * tq, nbp)), expand, preferred_element_type=F32)
    colf = (col1 - delta).astype(F32)
    hpm = n_heads // n_mask if n_mask else n_heads
    ps = []
    valid = ok
    for h in range(n_heads):
        rows = slice(h * tq, (h + 1) * tq)
        if n_mask and h % hpm == 0:
            g = h // hpm
            valid = ok & (mexp[g * tq:(g + 1) * tq] > 0.5)
        s = jnp.where(valid, s_all[rows] + slopes[h] * colf, NEG)
        m_old = m_sc[rows]
        m_new = jnp.maximum(m_old, jnp.max(s, axis=-1, keepdims=True))
        alpha = jnp.exp(m_old - m_new)
        p = jnp.exp(s - m_new)
        l_sc[rows] = alpha * l_sc[rows] + jnp.sum(p, axis=-1, keepdims=True)
        m_sc[rows] = m_new
        acc_sc[rows] = alpha * acc_sc[rows]
        ps.append(_bf(p))
    acc_sc[...] += jnp.dot(jnp.concatenate(ps, axis=0), _bf(v), preferred_element_type=F32)


def _attn_finish(m_sc, l_sc, acc_sc, o_ref, n_heads, n_groups, tq):
    outs = []
    for h in range(n_heads):
        rows = slice(h * tq, (h + 1) * tq)
        outs.append(jnp.where(m_sc[rows] > 0.5 * NEG, acc_sc[rows] / l_sc[rows], 0.0))
    _place_heads(outs, n_heads, n_groups, o_ref)


def _attn_init(q_ref, q_sc, m_sc, l_sc, acc_sc, n_heads, w):
    q_sc[...] = _stack_heads(q_ref[0], n_heads, w)
    m_sc[...] = jnp.full(m_sc.shape, NEG, F32)
    l_sc[...] = jnp.zeros(l_sc.shape, F32)
    acc_sc[...] = jnp.zeros(acc_sc.shape, F32)


def _kv_tile_range(qi, *, tq, tk, qpos0, kpos0, window, n_kt):
    qlo = qpos0 + qi * tq
    last = jnp.minimum((qlo + tq - 1 - kpos0) // tk, n_kt - 1)
    if window is None:
        first = 0
    else:
        first = jnp.maximum(qlo - window + 1 - kpos0, 0) // tk
    return first, last


def _flash_kernel(*refs, cfg, rng):
    if cfg["n_mask"]:
        q_ref, kv_ref, mask_ref, o_ref, q_sc, m_sc, l_sc, acc_sc = refs
    else:
        q_ref, kv_ref, o_ref, q_sc, m_sc, l_sc, acc_sc = refs
        mask_ref = None
    w, tq, tk = cfg["n_groups"] * HEAD_DIM, rng["tq"], rng["tk"]
    qi, j = pl.program_id(1), pl.program_id(2)
    first, last = _kv_tile_range(qi, **rng)

    @pl.when(j == 0)
    def _():
        _attn_init(q_ref, q_sc, m_sc, l_sc, acc_sc, cfg["n_heads"], w)

    @pl.when(first + j <= last)
    def _():
        kv = kv_ref[0]
        _attend_tile(q_sc[...], kv[:, :w], kv[:, w:], None if mask_ref is None else mask_ref[0],
                     rng["kpos0"] + (first + j) * tk, rng["qpos0"] + qi * tq, m_sc, l_sc, acc_sc, tq=tq, **cfg)

    @pl.when(j == pl.num_programs(2) - 1)
    def _():
        _attn_finish(m_sc, l_sc, acc_sc, o_ref, cfg["n_heads"], cfg["n_groups"], tq)


def _flash_linear(q_arr, q_col, kv_arr, kv_col, mask, cfg, *, tq, tk, qpos0, kpos0, n_steps):
    b, tp, _ = q_arr.shape
    w = cfg["n_groups"] * HEAD_DIM
    n_heads = cfg["n_heads"]
    n_kt = kv_arr.shape[1] // tk
    rng = dict(tq=tq, tk=tk, qpos0=qpos0, kpos0=kpos0, window=cfg["window"], n_kt=n_kt)

    def kv_map(bi, i, j):
        first, last = _kv_tile_range(i, **rng)
        return (bi, jnp.minimum(first + j, last), kv_col)

    in_specs = [pl.BlockSpec((1, tq, n_heads * w), lambda bi, i, j: (bi, i, q_col)),
                pl.BlockSpec((1, tk, 2 * w), kv_map)]
    args = [q_arr, kv_arr]
    if cfg["n_mask"]:
        in_specs.append(pl.BlockSpec((1, cfg["n_mask"], tq, mask.shape[-1]), lambda bi, i, j: (bi, 0, i, 0)))
        args.append(mask)
    return pl.pallas_call(
        functools.partial(_flash_kernel, cfg=cfg, rng=rng),
        grid=(b, tp // tq, n_steps),
        in_specs=in_specs,
        out_specs=pl.BlockSpec((1, tq, n_heads * HEAD_DIM), lambda bi, i, j: (bi, i, 0)),
        out_shape=jax.ShapeDtypeStruct((b, tp, n_heads * HEAD_DIM), F32),
        scratch_shapes=[pltpu.VMEM((n_heads * tq, w), BF16), pltpu.VMEM((n_heads * tq, 1), F32),
                        pltpu.VMEM((n_heads * tq, 1), F32), pltpu.VMEM((n_heads * tq, w), F32)],
        compiler_params=_cparams("parallel", "parallel", "arbitrary"),
        name="flash",
    )(*args)


PAGES_PER_STEP = 8


def _paged_kernel(*refs, cfg, npp, tq, qpos0):
    pt_ref = refs[0]
    page_refs = refs[2:2 + npp]
    q_ref, tail_ref, mask_ref = refs[1], refs[2 + npp], refs[3 + npp]
    o_ref, q_sc, m_sc, l_sc, acc_sc = refs[4 + npp:]
    del pt_ref
    w = cfg["n_groups"] * HEAD_DIM
    j = pl.program_id(1)
    n_steps = pl.num_programs(1)

    @pl.when(j == 0)
    def _():
        _attn_init(q_ref, q_sc, m_sc, l_sc, acc_sc, cfg["n_heads"], w)

    @pl.when(j < n_steps - 1)
    def _():
        for k in range(npp):
            page = page_refs[k][0]
            _attend_tile(q_sc[...], page[:, :w], page[:, w:], mask_ref[0], (j * npp + k) * PAGE_SIZE, qpos0,
                         m_sc, l_sc, acc_sc, tq=tq, **cfg)

    @pl.when(j == n_steps - 1)
    def _():
        tail = tail_ref[0]
        _attend_tile(q_sc[...], tail[:, :w], tail[:, w:], mask_ref[0], qpos0, qpos0, m_sc, l_sc, acc_sc, tq=tq, **cfg)
        _attn_finish(m_sc, l_sc, acc_sc, o_ref, cfg["n_heads"], cfg["n_groups"], tq)


def _flash_paged(proj3, q_col, tail_col, pool, page_table, mask, cfg, *, qpos0):
    b, tq, _ = proj3.shape
    w = cfg["n_groups"] * HEAD_DIM
    n_heads = cfg["n_heads"]
    n_pages = page_table.shape[1]
    npp = math.gcd(PAGES_PER_STEP, n_pages)
    steps = n_pages // npp

    def page_map(k):
        return lambda bi, j, pt: (pt[bi * n_pages + jnp.minimum(j, steps - 1) * npp + k], 0, 0)

    grid_spec = pltpu.PrefetchScalarGridSpec(
        num_scalar_prefetch=1,
        grid=(b, steps + 1),
        in_specs=[pl.BlockSpec((1, tq, n_heads * w), lambda bi, j, pt: (bi, 0, q_col))]
        + [pl.BlockSpec((1, PAGE_SIZE, 2 * w), page_map(k)) for k in range(npp)]
        + [pl.BlockSpec((1, tq, 2 * w), lambda bi, j, pt: (bi, 0, tail_col)),
           pl.BlockSpec((1, cfg["n_mask"], tq, mask.shape[-1]), lambda bi, j, pt: (bi, 0, 0, 0))],
        out_specs=pl.BlockSpec((1, tq, n_heads * HEAD_DIM), lambda bi, j, pt: (bi, 0, 0)),
        scratch_shapes=[pltpu.VMEM((n_heads * tq, w), BF16), pltpu.VMEM((n_heads * tq, 1), F32),
                        pltpu.VMEM((n_heads * tq, 1), F32), pltpu.VMEM((n_heads * tq, w), F32)],
    )
    return pl.pallas_call(
        functools.partial(_paged_kernel, cfg=cfg, npp=npp, tq=tq, qpos0=qpos0),
        grid_spec=grid_spec,
        out_shape=jax.ShapeDtypeStruct((b, tq, n_heads * HEAD_DIM), F32),
        compiler_params=_cparams("parallel", "arbitrary"),
        name="flash_paged",
    )(page_table.reshape(-1), proj3, *([pool] * npp), proj3, mask)


def _attn_cfg(kind):
    if kind == "moba":
        return dict(n_heads=MOBA_HEADS, n_groups=MOBA_KV_HEADS, n_mask=MOBA_HEADS,
                    blk_shift=int(math.log2(MOBA_BLOCK)), window=None, slopes=SLOPES_B)
    if kind == "sel":
        return dict(n_heads=NSA_HEADS, n_groups=NSA_KV_HEADS, n_mask=NSA_KV_HEADS,
                    blk_shift=int(math.log2(NSA_SEL_BLOCK)), window=None, slopes=SLOPES_A)
    return dict(n_heads=NSA_HEADS, n_groups=NSA_KV_HEADS, n_mask=0, blk_shift=0, window=NSA_WINDOW, slopes=SLOPES_A)


S5_CHUNK = 4 * LANES
S5_GCHUNK = S5_CHUNK // S5_STATE


def _s5_params(lp):
    lr, li = lp["s5_lambda_re"], lp["s5_lambda_im"]
    dt = jnp.exp(lp["s5_log_dt"])[:, None]
    mag = jnp.exp(lr * dt)
    a_re = mag * jnp.cos(li * dt)
    a_im = mag * jnp.sin(li * dt)
    den = lr * lr + li * li
    f_re = ((a_re - 1.0) * lr + a_im * li) / den
    f_im = (a_im * lr - (a_re - 1.0) * li) / den
    b_re, b_im = lp["s5_b_re"], lp["s5_b_im"]
    bb_re = f_re[..., None] * b_re - f_im[..., None] * b_im
    bb_im = f_re[..., None] * b_im + f_im[..., None] * b_re
    n_ch = S5_GROUPS // S5_GCHUNK
    eye = np.eye(S5_GCHUNK, dtype=np.float32)

    def in_w(bb):
        x = bb.reshape(n_ch, S5_GCHUNK, S5_STATE, S5_GROUP_CH).transpose(0, 1, 3, 2)
        x = x[:, :, :, None, :] * eye[None, :, None, :, None]
        return x.reshape(n_ch, S5_GCHUNK * S5_GROUP_CH, S5_CHUNK)

    def out_w(c):
        x = c.reshape(n_ch, S5_GCHUNK, S5_GROUP_CH, S5_STATE).transpose(0, 1, 3, 2)
        x = x[:, :, :, None, :] * eye[None, :, None, :, None]
        return x.reshape(n_ch, S5_CHUNK, S5_GCHUNK * S5_GROUP_CH).astype(BF16)

    return dict(a_re=a_re.reshape(1, S5_LANES), a_im=a_im.reshape(1, S5_LANES),
                bw_re=in_w(bb_re), bw_im=in_w(bb_im),
                cw_re=out_w(lp["s5_c_re"]), cw_im=out_w(lp["s5_c_im"]),
                d=lp["s5_d"].reshape(1, S5_WIDTH))


def _s5_out_proj(h_re, h_im, cwr_ref, cwi_ref):
    cols = []
    for c in range(S5_LANES // S5_CHUNK):
        sl = slice(c * S5_CHUNK, (c + 1) * S5_CHUNK)
        cols.append(jnp.dot(_bf(h_re[:, sl]), cwr_ref[c], preferred_element_type=F32)
                    - jnp.dot(_bf(h_im[:, sl]), cwi_ref[c], preferred_element_type=F32))
    return jnp.concatenate(cols, axis=1)


def _s5scan_kernel(u_ref, h0r_ref, h0i_ref, ar_ref, ai_ref, bwr_ref, bwi_ref, cwr_ref, cwi_ref, d_ref,
                   y_ref, hr_ref, hi_ref, bur_sc, bui_sc, hr_sc, hi_sc, *, n_par, n_j):
    @pl.when(pl.program_id(0) == 0)
    def _():
        hr_sc[...] = h0r_ref[...]
        hi_sc[...] = h0i_ref[...]

    u = u_ref[...]
    gw = S5_GCHUNK * S5_GROUP_CH
    for c in range(S5_LANES // S5_CHUNK):
        uc = u[:, c * gw:(c + 1) * gw]
        bur_sc[:, c * S5_CHUNK:(c + 1) * S5_CHUNK] = _dot3(uc, bwr_ref[c])
        bui_sc[:, c * S5_CHUNK:(c + 1) * S5_CHUNK] = _dot3(uc, bwi_ref[c])

    for c in range(S5_LANES // S5_CHUNK):
        sl = slice(c * S5_CHUNK, (c + 1) * S5_CHUNK)
        a_re = jnp.broadcast_to(ar_ref[:, sl], (n_par, S5_CHUNK))
        a_im = jnp.broadcast_to(ai_ref[:, sl], (n_par, S5_CHUNK))

        def step(jj, carry):
            h_re, h_im = carry
            rows = pl.ds(pl.multiple_of(jj * n_par, n_par), n_par)
            n_re = a_re * h_re - a_im * h_im + bur_sc[rows, sl]
            n_im = a_re * h_im + a_im * h_re + bui_sc[rows, sl]
            bur_sc[rows, sl] = n_re
            bui_sc[rows, sl] = n_im
            return n_re, n_im

        h_re, h_im = lax.fori_loop(0, n_j, step, (hr_sc[:, sl], hi_sc[:, sl]))
        hr_sc[:, sl] = h_re
        hi_sc[:, sl] = h_im

    y_ref[...] = _s5_out_proj(bur_sc[...], bui_sc[...], cwr_ref, cwi_ref) + d_ref[...] * u
    hr_ref[...] = hr_sc[...]
    hi_ref[...] = hi_sc[...]


def _s5_scan(u_rows, h0_re, h0_im, sp, n_par, jc):
    rows = u_rows.shape[0]
    n_j_total = rows // n_par
    steps = n_j_total // jc
    full = lambda a: pl.BlockSpec(a.shape, lambda i: (0,) * a.ndim)
    ins = [u_rows, h0_re, h0_im, sp["a_re"], sp["a_im"], sp["bw_re"], sp["bw_im"], sp["cw_re"], sp["cw_im"], sp["d"]]
    return pl.pallas_call(
        functools.partial(_s5scan_kernel, n_par=n_par, n_j=jc),
        grid=(steps,),
        in_specs=[pl.BlockSpec((jc * n_par, S5_WIDTH), lambda i: (i, 0))] + [full(a) for a in ins[1:]],
        out_specs=[pl.BlockSpec((jc * n_par, S5_WIDTH), lambda i: (i, 0)),
                   pl.BlockSpec((n_par, S5_LANES), lambda i: (0, 0)),
                   pl.BlockSpec((n_par, S5_LANES), lambda i: (0, 0))],
        out_shape=[jax.ShapeDtypeStruct((rows, S5_WIDTH), F32),
                   jax.ShapeDtypeStruct((n_par, S5_LANES), F32),
                   jax.ShapeDtypeStruct((n_par, S5_LANES), F32)],
        scratch_shapes=[pltpu.VMEM((jc * n_par, S5_LANES), F32), pltpu.VMEM((jc * n_par, S5_LANES), F32),
                        pltpu.VMEM((n_par, S5_LANES), F32), pltpu.VMEM((n_par, S5_LANES), F32)],
        compiler_params=_cparams("arbitrary"),
        name="s5_scan",
    )(*ins)


def _s5fix_kernel(y_ref, er_ref, ei_ref, ar_ref, ai_ref, cwr_ref, cwi_ref, o_ref, fr_ref, fi_ref,
                  cr_sc, ci_sc, pr_sc, pi_sc, xr_sc, xi_sc, *, n_seg, n_j, n_j_total):
    a_re = ar_ref[...]
    a_im = ai_ref[...]

    @pl.when(pl.program_id(0) == 0)
    def _():
        def pw_step(_, carry):
            p_re, p_im = carry
            return a_re * p_re - a_im * p_im, a_re * p_im + a_im * p_re

        al_re, al_im = lax.fori_loop(0, n_j_total - 1, pw_step, (a_re, a_im))
        c_re = jnp.zeros((1, S5_LANES), F32)
        c_im = jnp.zeros((1, S5_LANES), F32)
        for s in range(n_seg):
            cr_sc[s:s + 1, :] = c_re
            ci_sc[s:s + 1, :] = c_im
            e_re, e_im = er_ref[s:s + 1, :], ei_ref[s:s + 1, :]
            c_re, c_im = (e_re + al_re * c_re - al_im * c_im, e_im + al_re * c_im + al_im * c_re)
        fr_ref[...] = c_re
        fi_ref[...] = c_im
        pr_sc[...] = jnp.broadcast_to(a_re, pr_sc.shape)
        pi_sc[...] = jnp.broadcast_to(a_im, pi_sc.shape)

    for c in range(S5_LANES // S5_CHUNK):
        sl = slice(c * S5_CHUNK, (c + 1) * S5_CHUNK)
        ab_re = jnp.broadcast_to(a_re[:, sl], (n_seg, S5_CHUNK))
        ab_im = jnp.broadcast_to(a_im[:, sl], (n_seg, S5_CHUNK))
        c_re, c_im = cr_sc[:, sl], ci_sc[:, sl]

        def step(jj, carry):
            p_re, p_im = carry
            rows = pl.ds(pl.multiple_of(jj * n_seg, n_seg), n_seg)
            xr_sc[rows, sl] = p_re * c_re - p_im * c_im
            xi_sc[rows, sl] = p_re * c_im + p_im * c_re
            return ab_re * p_re - ab_im * p_im, ab_re * p_im + ab_im * p_re

        p_re, p_im = lax.fori_loop(0, n_j, step, (pr_sc[:, sl], pi_sc[:, sl]))
        pr_sc[:, sl] = p_re
        pi_sc[:, sl] = p_im

    o_ref[...] = y_ref[...] + _s5_out_proj(xr_sc[...], xi_sc[...], cwr_ref, cwi_ref)


def _s5_fix(y_rows, end_re, end_im, sp, n_seg, jc):
    rows = y_rows.shape[0]
    n_j_total = rows // n_seg
    steps = n_j_total // jc
    full = lambda a: pl.BlockSpec(a.shape, lambda i: (0,) * a.ndim)
    ins = [y_rows, end_re, end_im, sp["a_re"], sp["a_im"], sp["cw_re"], sp["cw_im"]]
    seg = lambda: pltpu.VMEM((n_seg, S5_LANES), F32)
    return pl.pallas_call(
        functools.partial(_s5fix_kernel, n_seg=n_seg, n_j=jc, n_j_total=n_j_total),
        grid=(steps,),
        in_specs=[pl.BlockSpec((jc * n_seg, S5_WIDTH), lambda i: (i, 0))] + [full(a) for a in ins[1:]],
        out_specs=[pl.BlockSpec((jc * n_seg, S5_WIDTH), lambda i: (i, 0)),
                   pl.BlockSpec((1, S5_LANES), lambda i: (0, 0)),
                   pl.BlockSpec((1, S5_LANES), lambda i: (0, 0))],
        out_shape=[jax.ShapeDtypeStruct((rows, S5_WIDTH), F32),
                   jax.ShapeDtypeStruct((1, S5_LANES), F32),
                   jax.ShapeDtypeStruct((1, S5_LANES), F32)],
        scratch_shapes=[seg(), seg(), seg(), seg(),
                        pltpu.VMEM((jc * n_seg, S5_LANES), F32), pltpu.VMEM((jc * n_seg, S5_LANES), F32)],
        compiler_params=_cparams("arbitrary"),
        name="s5_fix",
    )(*ins)


S5_SEGMENTS = SUBLANES


def _s5_prompt(u, sp):
    t = u.shape[0]
    n_seg = S5_SEGMENTS
    seg_len = t // n_seg
    jc = min(64, seg_len)
    u_rows = u.reshape(n_seg, seg_len, S5_WIDTH).transpose(1, 0, 2).reshape(t, S5_WIDTH)
    zero = jnp.zeros((n_seg, S5_LANES), F32)
    y_loc, end_re, end_im = _s5_scan(u_rows, zero, zero, sp, n_seg, jc)
    y_rows, f_re, f_im = _s5_fix(y_loc, end_re, end_im, sp, n_seg, jc)
    y = y_rows.reshape(seg_len, n_seg, S5_WIDTH).transpose(1, 0, 2).reshape(t, S5_WIDTH)
    return y, f_re, f_im


def _s5_sample(u3, h0_re, h0_im, sp):
    b, t, _ = u3.shape
    u_rows = u3.transpose(1, 0, 2).reshape(t * b, S5_WIDTH)
    y_rows, h_re, h_im = _s5_scan(u_rows, h0_re, h0_im, sp, b, t)
    return y_rows.reshape(t, b, S5_WIDTH).transpose(1, 0, 2), h_re, h_im


def _merge_kernel(x_ref, oc_ref, os_ref, ow_ref, om_ref, y5_ref, ga_ref, gm_ref, wglu_ref, bglu_ref,
                  wn_ref, wm_ref, w5_ref, wo_ref, g_ref, b_ref, o_ref):
    qw = NSA_Q_W
    ga = ga_ref[...]
    o_nsa = (_sigmoid(ga[:, :qw]) * oc_ref[...] + _sigmoid(ga[:, qw:2 * qw]) * os_ref[...]
             + _sigmoid(ga[:, 2 * qw:]) * ow_ref[...])
    z = _gelu(y5_ref[...])
    o_s5 = z * _sigmoid(jnp.dot(_bf(z), wglu_ref[...], preferred_element_type=F32) + bglu_ref[...])
    gm = gm_ref[...]
    d = D_MODEL
    merged = (_sigmoid(gm[:, :d]) * jnp.dot(_bf(o_nsa), wn_ref[...], preferred_element_type=F32)
              + _sigmoid(gm[:, d:2 * d]) * jnp.dot(_bf(om_ref[...]), wm_ref[...], preferred_element_type=F32)
              + _sigmoid(gm[:, 2 * d:]) * jnp.dot(_bf(o_s5), w5_ref[...], preferred_element_type=F32))
    mix = jnp.dot(_bf(merged), wo_ref[...], preferred_element_type=F32)
    o_ref[...] = _layer_norm(DN_ALPHA * x_ref[...] + mix, g_ref[...], b_ref[...])


def _merge(x2, o_c, o_s, o_w, o_m, y5, proj2, lp, tm):
    m = x2.shape[0]
    row = lambda n: pl.BlockSpec((tm, n), lambda i: (i, 0))
    full = lambda a: pl.BlockSpec(a.shape, lambda i: (0,) * a.ndim)
    ws = [lp["s5_w_glu"].astype(BF16), lp["s5_b_glu"].reshape(1, S5_WIDTH),
          lp["w_br_nsa"].astype(BF16), lp["w_br_moba"].astype(BF16), lp["w_br_s5"].astype(BF16),
          lp["w_out"].astype(BF16), lp["ln1_g"].reshape(1, D_MODEL), lp["ln1_b"].reshape(1, D_MODEL)]
    gaw, gmw = 3 * NSA_Q_W, N_BRANCH * D_MODEL
    return pl.pallas_call(
        _merge_kernel,
        grid=(m // tm,),
        in_specs=[row(D_MODEL), row(NSA_Q_W), row(NSA_Q_W), row(NSA_Q_W), row(MOBA_Q_W), row(S5_WIDTH),
                  pl.BlockSpec((tm, gaw), lambda i: (i, PC_GA // gaw)),
                  pl.BlockSpec((tm, gmw), lambda i: (i, PC_GM // gmw))] + [full(a) for a in ws],
        out_specs=row(D_MODEL),
        out_shape=jax.ShapeDtypeStruct((m, D_MODEL), F32),
        compiler_params=_cparams("parallel"),
        name="merge",
    )(x2, o_c, o_s, o_w, o_m, y5, proj2, proj2, *ws)


def _router_kernel(x_ref, w_ref, b_ref, o_ref):
    logits = _dot3(x_ref[...], w_ref[...]) + b_ref[...]
    lane = lax.broadcasted_iota(jnp.int32, logits.shape, 1)
    lanef = lane.astype(F32)
    logits = jnp.where(lane < N_EXPERTS, logits, -jnp.inf)
    sel = _top_select(logits, lanef, MOE_TOP_K)
    m = jnp.max(logits, axis=-1, keepdims=True)
    e = jnp.where(sel > 0.5, jnp.exp(logits - m), 0.0)
    o_ref[...] = e / jnp.sum(e, axis=-1, keepdims=True)


def _router(x2, w_router, b_router, tm):
    m = x2.shape[0]
    w = jnp.pad(w_router, ((0, 0), (0, LANES - N_EXPERTS)))
    bias = jnp.pad(b_router, (0, LANES - N_EXPERTS)).reshape(1, LANES)
    return pl.pallas_call(
        _router_kernel,
        grid=(m // tm,),
        in_specs=[pl.BlockSpec((tm, D_MODEL), lambda i: (i, 0)),
                  pl.BlockSpec(w.shape, lambda i: (0, 0)), pl.BlockSpec(bias.shape, lambda i: (0, 0))],
        out_specs=pl.BlockSpec((tm, LANES), lambda i: (i, 0)),
        out_shape=jax.ShapeDtypeStruct((m, LANES), F32),
        compiler_params=_cparams("parallel"),
        name="router",
    )(x2, w, bias)


def _ffn_kernel(*refs, use_comb):
    if use_comb:
        x_ref, comb_ref, wg_ref, wu_ref, wd_ref, g_ref, b_ref, o_ref, acc_sc = refs
    else:
        x_ref, wg_ref, wu_ref, wd_ref, g_ref, b_ref, o_ref, acc_sc = refs
    f = pl.program_id(1)

    @pl.when(f == 0)
    def _():
        acc_sc[...] = jnp.zeros(acc_sc.shape, F32)

    xb = _bf(x_ref[...])
    wg = wg_ref[0] if use_comb else wg_ref[...]
    wu = wu_ref[0] if use_comb else wu_ref[...]
    wd = wd_ref[0] if use_comb else wd_ref[...]
    gate = jnp.dot(xb, _bf(wg), preferred_element_type=F32)
    up = jnp.dot(xb, _bf(wu), preferred_element_type=F32)
    h = gate * _sigmoid(gate) * up
    if use_comb:
        comb = comb_ref[...]
        lane = lax.broadcasted_iota(jnp.int32, comb.shape, 1)
        h = h * jnp.sum(jnp.where(lane == f, comb, 0.0), axis=-1, keepdims=True)
    acc_sc[...] += jnp.dot(_bf(h), _bf(wd), preferred_element_type=F32)

    @pl.when(f == pl.num_programs(1) - 1)
    def _():
        o_ref[...] = _layer_norm(DN_ALPHA * x_ref[...] + acc_sc[...], g_ref[...], b_ref[...])


def _ffn(x2, comb, wg, wu, wd, ln_g, ln_b, tm, tf):
    m = x2.shape[0]
    use_comb = comb is not None
    if use_comb:
        n_f = wg.shape[0]
        w_specs = [pl.BlockSpec((1,) + wg.shape[1:], lambda i, f: (f, 0, 0)),
                   pl.BlockSpec((1,) + wu.shape[1:], lambda i, f: (f, 0, 0)),
                   pl.BlockSpec((1,) + wd.shape[1:], lambda i, f: (f, 0, 0))]
    else:
        n_f = wg.shape[1] // tf
        w_specs = [pl.BlockSpec((D_MODEL, tf), lambda i, f: (0, f)),
                   pl.BlockSpec((D_MODEL, tf), lambda i, f: (0, f)),
                   pl.BlockSpec((tf, D_MODEL), lambda i, f: (f, 0))]
    vec = pl.BlockSpec((1, D_MODEL), lambda i, f: (0, 0))
    in_specs = [pl.BlockSpec((tm, D_MODEL), lambda i, f: (i, 0))]
    args = [x2]
    if use_comb:
        in_specs.append(pl.BlockSpec((tm, LANES), lambda i, f: (i, 0)))
        args.append(comb)
    return pl.pallas_call(
        functools.partial(_ffn_kernel, use_comb=use_comb),
        grid=(m // tm, n_f),
        in_specs=in_specs + w_specs + [vec, vec],
        out_specs=pl.BlockSpec((tm, D_MODEL), lambda i, f: (i, 0)),
        out_shape=jax.ShapeDtypeStruct((m, D_MODEL), F32),
        scratch_shapes=[pltpu.VMEM((tm, D_MODEL), F32)],
        compiler_params=_cparams("parallel", "arbitrary"),
        name="ffn",
    )(*args, wg.astype(BF16), wu.astype(BF16), wd.astype(BF16), ln_g.reshape(1, D_MODEL), ln_b.reshape(1, D_MODEL))


def _token_mixer(x3, t_real, qpos0, past, lp, sp, h0_re, h0_im):
    b, tp, _ = x3.shape
    rows = b * tp
    x2 = x3.reshape(rows, D_MODEL)
    tm = min(512, rows)
    proj2 = _matmul(x2, lp["w_proj"], tm, 256)
    proj3 = proj2.reshape(b, tp, PC_END)
    kva = proj3[:, :t_real, PC_KVA:PC_END]
    new_cmp, new_sel, new_win = (kva[:, :, i * 2 * NSA_KV_W:(i + 1) * 2 * NSA_KV_W] for i in range(3))
    new_moba = proj3[:, :t_real, PC_KVB:PC_KVA]
    cw = _cmp_weights(lp["phi1_k"], lp["phi1_v"])
    cmp_args = (lp["pe_k"], lp["pe_v"], lp["phi1_k"], lp["phi1_v"], lp["phi2_k"], lp["phi2_v"])
    rw_a, rw_b = 2 * NSA_KV_W, 2 * MOBA_KV_W

    if past is None:
        assert qpos0 == 0 and b == 1 and t_real == tp and t_real % MOBA_BLOCK == 0
        tq = 128
        tk = 512 if tp % 512 == 0 else tp
        ab = _cmp_ab_linear(new_cmp.reshape(t_real // NSA_CMP_STRIDE, NSA_CMP_STRIDE * rw_a), cw)
        kc = _cmp_finish(ab, 1, *cmp_args)
        o_c, sel_mask = _cmp_attention(proj3, kc, 0, t_real, tq)
        n_kt = tp // tk
        o_s = _flash_linear(proj3, PC_QA // (NSA_HEADS * NSA_KV_W), proj3, (PC_KVA + rw_a) // rw_a, sel_mask,
                            _attn_cfg("sel"), tq=tq, tk=tk, qpos0=0, kpos0=0, n_steps=n_kt)
        n_win = min(n_kt, (NSA_WINDOW + tq - 2) // tk + 2)
        o_w = _flash_linear(proj3, PC_QA // (NSA_HEADS * NSA_KV_W), proj3, (PC_KVA + 2 * rw_a) // rw_a, None,
                            _attn_cfg("win"), tq=tq, tk=tk, qpos0=0, kpos0=0, n_steps=n_win)
        kmean = _moba_kmean(proj3, None)
        mb_mask = _moba_gate(proj3, kmean, 0, t_real, tq)
        o_m = _flash_linear(proj3, PC_QB // (MOBA_HEADS * MOBA_KV_W), proj3, PC_KVB // rw_b, mb_mask,
                            _attn_cfg("moba"), tq=tq, tk=tk, qpos0=0, kpos0=0, n_steps=n_kt)
        y5, h_re, h_im = _s5_prompt(proj2[:, PC_U:PC_U + S5_WIDTH], sp)
        keep = min(NSA_WINDOW, t_real)
        win_state = new_win[:, t_real - keep:]
    else:
        assert t_real < NSA_CMP_STRIDE and qpos0 % MOBA_BLOCK == 0 and tp % SUBLANES == 0
        pt = past["page_table"]
        ab = _cmp_ab_paged(past["cmp"], pt, cw)
        kc = _cmp_finish(ab, b, *cmp_args)
        o_c, sel_mask = _cmp_attention(proj3, kc, qpos0, t_real, tp)
        o_s = _flash_paged(proj3, PC_QA // (NSA_HEADS * NSA_KV_W), (PC_KVA + rw_a) // rw_a, past["sel"], pt,
                           sel_mask, _attn_cfg("sel"), qpos0=qpos0)
        win_buf = past["win"]
        wb = win_buf.shape[1]
        kv_win = jnp.concatenate([win_buf, proj3[:, :, PC_KVA + 2 * rw_a:PC_END]], axis=1)
        o_w = _flash_linear(proj3, PC_QA // (NSA_HEADS * NSA_KV_W), kv_win, 0, None, _attn_cfg("win"),
                            tq=tp, tk=kv_win.shape[1], qpos0=qpos0, kpos0=qpos0 - wb, n_steps=1)
        kmean = _moba_kmean(past["moba"], pt)
        mb_mask = _moba_gate(proj3, kmean, qpos0, t_real, tp)
        o_m = _flash_paged(proj3, PC_QB // (MOBA_HEADS * MOBA_KV_W), PC_KVB // rw_b, past["moba"], pt, mb_mask,
                           _attn_cfg("moba"), qpos0=qpos0)
        y5r, h_re, h_im = _s5_sample(proj3[:, :t_real, PC_U:PC_U + S5_WIDTH], h0_re, h0_im, sp)
        y5 = jnp.pad(y5r, ((0, 0), (0, tp - t_real), (0, 0))).reshape(rows, S5_WIDTH)
        win_state = jnp.concatenate([win_buf, new_win], axis=1)[:, -wb:]

    flat = lambda a: a.reshape(rows, a.shape[-1])
    x1 = _merge(x2, flat(o_c), flat(o_s), flat(o_w), flat(o_m), y5, proj2, lp, min(256, rows))
    return x1, (new_cmp, new_sel, win_state, new_moba, h_re, h_im)


def _layer(l, x3, t_real, qpos0, past, h0_re, h0_im, lp, sp, ffn_p):
    b, tp, _ = x3.shape
    x1, st = _token_mixer(x3, t_real, qpos0, past, lp, sp, h0_re, h0_im)
    tm = min(512, x1.shape[0])
    if l % 2 == 0:
        x2 = _ffn(x1, None, ffn_p["wg"], ffn_p["wu"], ffn_p["wd"], lp["ln2_g"], lp["ln2_b"], tm, ffn_p["tf"])
    else:
        comb = _router(x1, ffn_p["w_router"], ffn_p["b_router"], tm)
        x2 = _ffn(x1, comb, ffn_p["wg"], ffn_p["wu"], ffn_p["wd"], lp["ln2_g"], lp["ln2_b"], tm, None)
    return x2.reshape(b, tp, D_MODEL), st


def kernel(x_prompt, x_sample, cache_nsa_cmp, cache_nsa_sel, cache_nsa_win, cache_moba, state_s5_re, state_s5_im,
           page_table, w_in, pe_k, phi1_k, phi2_k, pe_v, phi1_v, phi2_v, s5_lambda_re, s5_lambda_im, s5_log_dt,
           s5_b_re, s5_b_im, s5_c_re, s5_c_im, s5_d, s5_w_glu, s5_b_glu, w_br_nsa, w_br_moba, w_br_s5, w_out,
           ln1_g, ln1_b, ln2_g, ln2_b, ffn_w_gate, ffn_w_up, ffn_w_down, moe_w_router, moe_b_router,
           moe_w_gate, moe_w_up, moe_w_down):
    depth = w_in.shape[0]
    bp, seq, _ = x_prompt.shape
    bs, dec_seq, _ = x_sample.shape
    n_pool = cache_moba.shape[1]
    past_len = page_table.shape[1] * PAGE_SIZE
    tps = _round_up(dec_seq, SUBLANES)
    y_p = x_prompt
    y_s = jnp.pad(x_sample, ((0, 0), (0, tps - dec_seq), (0, 0)))
    rw_a, rw_b = 2 * NSA_KV_W, 2 * MOBA_KV_W
    p_st, s_st = [], []
    for l in range(depth):
        lp = dict(w_proj=_proj_weight(w_in[l]), pe_k=pe_k[l], phi1_k=phi1_k[l], phi2_k=phi2_k[l], pe_v=pe_v[l],
                  phi1_v=phi1_v[l], phi2_v=phi2_v[l], s5_lambda_re=s5_lambda_re[l], s5_lambda_im=s5_lambda_im[l],
                  s5_log_dt=s5_log_dt[l], s5_b_re=s5_b_re[l], s5_b_im=s5_b_im[l], s5_c_re=s5_c_re[l],
                  s5_c_im=s5_c_im[l], s5_d=s5_d[l], s5_w_glu=s5_w_glu[l], s5_b_glu=s5_b_glu[l],
                  w_br_nsa=w_br_nsa[l], w_br_moba=w_br_moba[l], w_br_s5=w_br_s5[l], w_out=w_out[l],
                  ln1_g=ln1_g[l], ln1_b=ln1_b[l], ln2_g=ln2_g[l], ln2_b=ln2_b[l])
        sp = _s5_params(lp)
        if l % 2 == 0:
            d_ff = ffn_w_gate.shape[2]
            tf = d_ff // 2 if (d_ff // 2) % LANES == 0 else d_ff
            ffn_p = dict(wg=ffn_w_gate[l // 2], wu=ffn_w_up[l // 2], wd=ffn_w_down[l // 2], tf=tf)
        else:
            ffn_p = dict(wg=moe_w_gate[l // 2], wu=moe_w_up[l // 2], wd=moe_w_down[l // 2],
                         w_router=moe_w_router[l // 2], b_router=moe_b_router[l // 2])
        y_p, st = _layer(l, y_p, seq, 0, None, None, None, lp, sp, ffn_p)
        p_st.append(st)
        past = dict(page_table=page_table,
                    cmp=cache_nsa_cmp[l].reshape(n_pool, PAGE_SIZE, rw_a),
                    sel=cache_nsa_sel[l].reshape(n_pool, PAGE_SIZE, rw_a),
                    moba=cache_moba[l].reshape(n_pool, PAGE_SIZE, rw_b),
                    win=cache_nsa_win[l].reshape(bs, -1, rw_a))
        y_s, st = _layer(l, y_s, dec_seq, past_len, past, state_s5_re[l].reshape(bs, S5_LANES),
                         state_s5_im[l].reshape(bs, S5_LANES), lp, sp, ffn_p)
        s_st.append(st)

    def stk(states, j, tail):
        return jnp.stack([st[j].reshape(st[j].shape[:2] + tail) if tail else st[j] for st in states], axis=0)

    row_a = (2, NSA_KV_HEADS, HEAD_DIM)
    row_b = (2, MOBA_KV_HEADS, HEAD_DIM)

    def s5_state(states, j, b):
        return jnp.stack([st[j].reshape(b, S5_GROUPS, S5_STATE) for st in states], axis=0)

    return (y_p, y_s[:, :dec_seq],
            stk(p_st, 0, row_a), stk(p_st, 1, row_a), stk(p_st, 2, row_a), stk(p_st, 3, row_b),
            s5_state(p_st, 4, bp), s5_state(p_st, 5, bp),
            stk(s_st, 0, row_a), stk(s_st, 1, row_a), stk(s_st, 2, row_a), stk(s_st, 3, row_b),
            s5_state(s_st, 4, bs), s5_state(s_st, 5, bs))
```

```python
import functools
import math

import numpy as np
import jax
import jax.numpy as jnp
from jax import lax
from jax.experimental import pallas as pl
from jax.experimental.pallas import tpu as pltpu

F32 = jnp.float32
BF16 = jnp.bfloat16

D_MODEL = 1024
HEAD_DIM = 64
PAGE_SIZE = 128
NSA_HEADS = 8
NSA_KV_HEADS = 2
NSA_CMP_LEN = 32
NSA_CMP_STRIDE = 16
NSA_PHI_HIDDEN = 128
NSA_SEL_BLOCK = 64
NSA_SEL_TOPK = 16
NSA_N_LOCAL = 2
NSA_WINDOW = 512
MOBA_HEADS = 8
MOBA_KV_HEADS = 4
MOBA_BLOCK = 256
MOBA_TOPK = 3
S5_GROUPS = 32
S5_GROUP_CH = 16
S5_WIDTH = S5_GROUPS * S5_GROUP_CH
S5_STATE = 64
S5_LANES = S5_GROUPS * S5_STATE
N_BRANCH = 3
N_EXPERTS = 8
MOE_TOP_K = 2
LN_EPS = 1e-5
DEPTH = 2
DN_ALPHA = (2.0 * DEPTH) ** 0.25

NSA_Q_W = NSA_HEADS * HEAD_DIM
NSA_KV_W = NSA_KV_HEADS * HEAD_DIM
MOBA_Q_W = MOBA_HEADS * HEAD_DIM
MOBA_KV_W = MOBA_KV_HEADS * HEAD_DIM
IN_SPLITS = (NSA_Q_W, 6 * NSA_KV_W, 3 * NSA_HEADS, MOBA_Q_W, 2 * MOBA_KV_W, S5_WIDTH, N_BRANCH * D_MODEL)
IN_OFFS = tuple(int(v) for v in np.cumsum((0,) + IN_SPLITS))

LANES = 128
SUBLANES = 8
VMEM_LIMIT = 56 * 1024 * 1024

PC_QB = 0
PC_QA = PC_QB + MOBA_HEADS * MOBA_KV_W
PC_GM = PC_QA + NSA_HEADS * NSA_KV_W
PC_GA = PC_GM + N_BRANCH * D_MODEL
PC_U = PC_GA + 3 * NSA_Q_W
PC_KVB = PC_U + S5_WIDTH
PC_KVA = PC_KVB + 2 * MOBA_KV_W
PC_END = PC_KVA + 6 * NSA_KV_W

NEG = -1e30
SLOPES_A = tuple(float(2.0 ** (-8.0 * (i + 1) / NSA_HEADS)) for i in range(NSA_HEADS))
SLOPES_B = tuple(float(2.0 ** (-8.0 * (i + 1) / MOBA_HEADS)) for i in range(MOBA_HEADS))


def _cparams(*sem):
    return pltpu.CompilerParams(dimension_semantics=sem, vmem_limit_bytes=VMEM_LIMIT)


def _round_up(n, m):
    return -(-n // m) * m


def _bf(x):
    return x.astype(BF16)


def _dot(a, b):
    return jnp.dot(_bf(a), _bf(b), preferred_element_type=F32)


def _dot_nt(a, b):
    return lax.dot_general(_bf(a), _bf(b), (((1,), (1,)), ((), ())), preferred_element_type=F32)


def _split(a):
    hi = a.astype(BF16)
    lo = (a - hi.astype(F32)).astype(BF16)
    return hi, lo


def _dot3(a, b):
    ah, al = _split(a)
    bh, bl = _split(b)
    d = functools.partial(jnp.dot, preferred_element_type=F32)
    return d(ah, bh) + (d(ah, bl) + d(al, bh))


def _dot3_nt(a, b):
    ah, al = _split(a)
    bh, bl = _split(b)
    d = functools.partial(lax.dot_general, dimension_numbers=(((1,), (1,)), ((), ())),
                          preferred_element_type=F32)
    return d(ah, bh) + (d(ah, bl) + d(al, bh))


def _dot2_exact_rhs(a, b_bf16):
    ah, al = _split(a)
    d = functools.partial(jnp.dot, preferred_element_type=F32)
    return d(ah, b_bf16) + d(al, b_bf16)


def _sigmoid(x):
    return 1.0 / (1.0 + jnp.exp(-x))


def _gelu(x):
    c = math.sqrt(2.0 / math.pi)
    return x * (0.5 * (1.0 + jnp.tanh(c * (x + 0.044715 * (x * x * x)))))


def _layer_norm(v, g, b):
    mu = jnp.mean(v, axis=-1, keepdims=True)
    vc = v - mu
    var = jnp.mean(vc * vc, axis=-1, keepdims=True)
    return vc * lax.rsqrt(var + LN_EPS) * g + b


def _mm_kernel(x_ref, w_ref, o_ref):
    o_ref[...] = jnp.dot(_bf(x_ref[...]), w_ref[...], preferred_element_type=F32)


def _matmul(x, w, tm, tn):
    m, k = x.shape
    n = w.shape[1]
    return pl.pallas_call(
        _mm_kernel,
        grid=(m // tm, n // tn),
        in_specs=[pl.BlockSpec((tm, k), lambda i, j: (i, 0)),
                  pl.BlockSpec((k, tn), lambda i, j: (0, j))],
        out_specs=pl.BlockSpec((tm, tn), lambda i, j: (i, j)),
        out_shape=jax.ShapeDtypeStruct((m, n), F32),
        compiler_params=_cparams("parallel", "arbitrary"),
        name="proj",
    )(x, w)


def _proj_weight(w_in):
    scale = HEAD_DIM ** -0.5
    qa = w_in[:, IN_OFFS[0]:IN_OFFS[1]].reshape(D_MODEL, NSA_HEADS, 1, HEAD_DIM) * scale
    oh_a = np.zeros((NSA_HEADS, NSA_KV_HEADS, 1), np.float32)
    for h in range(NSA_HEADS):
        oh_a[h, h // (NSA_HEADS // NSA_KV_HEADS)] = 1.0
    qa = (qa * oh_a[None]).reshape(D_MODEL, NSA_HEADS * NSA_KV_W)
    qb = w_in[:, IN_OFFS[3]:IN_OFFS[4]].reshape(D_MODEL, MOBA_HEADS, 1, HEAD_DIM) * scale
    oh_b = np.zeros((MOBA_HEADS, MOBA_KV_HEADS, 1), np.float32)
    for h in range(MOBA_HEADS):
        oh_b[h, h // (MOBA_HEADS // MOBA_KV_HEADS)] = 1.0
    qb = (qb * oh_b[None]).reshape(D_MODEL, MOBA_HEADS * MOBA_KV_W)
    ga = w_in[:, IN_OFFS[2]:IN_OFFS[3]].reshape(D_MODEL, NSA_HEADS, 3).transpose(0, 2, 1)
    ga = jnp.broadcast_to(ga[..., None], (D_MODEL, 3, NSA_HEADS, HEAD_DIM)).reshape(D_MODEL, 3 * NSA_Q_W)
    gm = w_in[:, IN_OFFS[6]:IN_OFFS[7]]
    u = w_in[:, IN_OFFS[5]:IN_OFFS[6]]
    kvb = w_in[:, IN_OFFS[4]:IN_OFFS[5]]
    kva = w_in[:, IN_OFFS[1]:IN_OFFS[2]]
    return jnp.concatenate([qb, qa, gm, ga, u, kvb, kva], axis=1).astype(BF16)


def _cmpab_kernel(x_ref, w_ref, o_ref):
    row_w = 2 * NSA_KV_W
    x = x_ref[...]
    for kv in range(2):
        xk = jnp.concatenate(
            [x[:, r * row_w + kv * NSA_KV_W: r * row_w + (kv + 1) * NSA_KV_W] for r in range(NSA_CMP_STRIDE)],
            axis=1)
        o_ref[:, kv * 4 * NSA_PHI_HIDDEN:(kv + 1) * 4 * NSA_PHI_HIDDEN] = _dot(xk, w_ref[kv])


def _cmp_weights(phi1_k, phi1_v):
    eye = np.eye(NSA_KV_HEADS, dtype=np.float32)
    out = []
    for phi in (phi1_k, phi1_v):
        halves = []
        for half in range(NSA_CMP_LEN // NSA_CMP_STRIDE):
            w = phi[half * NSA_CMP_STRIDE * HEAD_DIM:(half + 1) * NSA_CMP_STRIDE * HEAD_DIM]
            w = w.reshape(NSA_CMP_STRIDE, 1, HEAD_DIM, 1, NSA_PHI_HIDDEN)
            w = w * eye[None, :, None, :, None]
            halves.append(w.reshape(NSA_CMP_STRIDE * NSA_KV_W, NSA_KV_HEADS * NSA_PHI_HIDDEN))
        out.append(jnp.concatenate(halves, axis=1))
    return jnp.stack(out).astype(BF16)


def _cmp_ab_linear(x, w):
    mc = x.shape[0]
    tmc = min(mc, 256)
    n_out = 8 * NSA_PHI_HIDDEN
    return pl.pallas_call(
        _cmpab_kernel,
        grid=(mc // tmc,),
        in_specs=[pl.BlockSpec((tmc, x.shape[1]), lambda i: (i, 0)),
                  pl.BlockSpec(w.shape, lambda i: (0, 0, 0))],
        out_specs=pl.BlockSpec((tmc, n_out), lambda i: (i, 0)),
        out_shape=jax.ShapeDtypeStruct((mc, n_out), F32),
        compiler_params=_cparams("parallel"),
        name="cmp_ab",
    )(x, w)


CMP_PAGES_PER_STEP = 16


def _cmpab_t_kernel(*refs, npp):
    page_refs, w_ref, o_ref, pk_sc, pv_sc = refs[1:1 + npp], refs[1 + npp], refs[2 + npp], refs[3 + npp], refs[4 + npp]
    w = NSA_KV_W
    p = PAGE_SIZE
    eye = jnp.where(lax.broadcasted_iota(jnp.int32, (p, p), 0) == lax.broadcasted_iota(jnp.int32, (p, p), 1),
                    1.0, 0.0).astype(BF16)
    for k in range(npp):
        page_t = page_refs[k][0]
        pk_sc[k * p:(k + 1) * p, :] = _dot_nt(eye, page_t[:w])
        pv_sc[k * p:(k + 1) * p, :] = _dot_nt(eye, page_t[w:])
    m = npp * (p // NSA_CMP_STRIDE)
    n_half = 4 * NSA_PHI_HIDDEN
    for kv, sc in ((0, pk_sc), (1, pv_sc)):
        acc = None
        for r in range(NSA_CMP_STRIDE):
            part = _dot(sc[pl.ds(r, m, stride=NSA_CMP_STRIDE), :], w_ref[kv, r])
            acc = part if acc is None else acc + part
        o_ref[:, kv * n_half:(kv + 1) * n_half] = acc


def _cmp_ab_paged(pool_t, base, page_table, w):
    b, n_pages = page_table.shape
    cpp = PAGE_SIZE // NSA_CMP_STRIDE
    npp = math.gcd(CMP_PAGES_PER_STEP, n_pages)
    steps = n_pages // npp
    n_out = 8 * NSA_PHI_HIDDEN
    w4 = w.reshape(2, NSA_CMP_STRIDE, NSA_KV_W, w.shape[-1])

    def x_map(k):
        return lambda bi, s, pt: (base + pt[bi * n_pages + s * npp + k], 0, 0)

    grid_spec = pltpu.PrefetchScalarGridSpec(
        num_scalar_prefetch=1,
        grid=(b, steps),
        in_specs=[pl.BlockSpec((1,) + pool_t.shape[1:], x_map(k)) for k in range(npp)]
        + [pl.BlockSpec(w4.shape, lambda bi, s, pt: (0, 0, 0, 0))],
        out_specs=pl.BlockSpec((npp * cpp, n_out), lambda bi, s, pt: (bi * steps + s, 0)),
        scratch_shapes=[pltpu.VMEM((npp * PAGE_SIZE, NSA_KV_W), F32), pltpu.VMEM((npp * PAGE_SIZE, NSA_KV_W), F32)],
    )
    return pl.pallas_call(
        functools.partial(_cmpab_t_kernel, npp=npp),
        grid_spec=grid_spec,
        out_shape=jax.ShapeDtypeStruct((b * n_pages * cpp, n_out), F32),
        compiler_params=_cparams("parallel", "arbitrary"),
        name="cmp_ab_paged",
    )(page_table.reshape(-1), *([pool_t] * npp), w4)


def _cmpfin_kernel(ab_ref, pe_ref, w1_ref, w2_ref, o_ref):
    mc = ab_ref.shape[0]
    hid2 = NSA_KV_HEADS * NSA_PHI_HIDDEN
    for kv in range(2):
        c = _dot(pe_ref[kv], w1_ref[kv])[0:1]
        c = jnp.concatenate([c] * NSA_KV_HEADS, axis=1)
        a = ab_ref[:, kv * 2 * hid2: kv * 2 * hid2 + hid2]
        bnext = pltpu.roll(ab_ref[:, kv * 2 * hid2 + hid2:(kv + 1) * 2 * hid2], mc - 1, 0)
        hpre = a + bnext + c
        o_ref[:, kv * NSA_KV_W:(kv + 1) * NSA_KV_W] = _dot(_gelu(hpre), w2_ref[kv])


def _cmp_finish(ab, nb, pe_k, pe_v, phi1_k, phi1_v, phi2_k, phi2_v):
    mc = ab.shape[0] // nb
    flat = NSA_CMP_LEN * HEAD_DIM
    pe = jnp.stack([jnp.broadcast_to(p.reshape(1, flat), (SUBLANES, flat)) for p in (pe_k, pe_v)])
    w1 = jnp.stack([phi1_k, phi1_v]).astype(BF16)
    eye = np.eye(NSA_KV_HEADS, dtype=np.float32)
    w2 = jnp.stack([(p[None, :, None, :] * eye[:, None, :, None]).reshape(NSA_KV_HEADS * NSA_PHI_HIDDEN, NSA_KV_W)
                    for p in (phi2_k, phi2_v)]).astype(BF16)
    out = pl.pallas_call(
        _cmpfin_kernel,
        grid=(nb,),
        in_specs=[pl.BlockSpec((mc, ab.shape[1]), lambda i: (i, 0)),
                  pl.BlockSpec(pe.shape, lambda i: (0, 0, 0)),
                  pl.BlockSpec(w1.shape, lambda i: (0, 0, 0)),
                  pl.BlockSpec(w2.shape, lambda i: (0, 0, 0))],
        out_specs=pl.BlockSpec((mc, 2 * NSA_KV_W), lambda i: (i, 0)),
        out_shape=jax.ShapeDtypeStruct((nb * mc, 2 * NSA_KV_W), F32),
        compiler_params=_cparams("parallel"),
        name="cmp_finish",
    )(ab, pe, w1, w2)
    return out.reshape(nb, mc, 2 * NSA_KV_W)


def _stack_heads(q, n_heads, w):
    return jnp.concatenate([_bf(q[:, h * w:(h + 1) * w]) for h in range(n_heads)], axis=0)


def _place_heads(o_heads, n_heads, n_groups, o_ref):
    hpg = n_heads // n_groups
    tq = o_heads[0].shape[0]
    lane = lax.broadcasted_iota(jnp.int32, (tq, LANES), 1)
    for pair in range(n_heads // 2):
        pieces = []
        for h in (2 * pair, 2 * pair + 1):
            src = (h // hpg) * HEAD_DIM
            piece = o_heads[h][:, (src // LANES) * LANES:(src // LANES + 1) * LANES]
            if src % LANES != (h % 2) * HEAD_DIM:
                piece = pltpu.roll(piece, HEAD_DIM, 1)
            pieces.append(piece)
        o_ref[0, :, pair * LANES:(pair + 1) * LANES] = jnp.where(lane < HEAD_DIM, pieces[0], pieces[1])


def _top_select(score, lanef, n_pick):
    sel = jnp.zeros(score.shape, F32)
    for _ in range(n_pick):
        m = jnp.max(score, axis=-1, keepdims=True)
        idx = jnp.min(jnp.where(score == m, lanef, 1e9), axis=-1, keepdims=True)
        pick = lanef == idx
        sel = jnp.where(pick & (m > -jnp.inf), 1.0, sel)
        score = jnp.where(pick, -jnp.inf, score)
    return sel


def _cmpattn_kernel(q_ref, kc_ref, ov_ref, o_ref, sel_ref, *, tq, qpos0, n_cmp, n_sel_blk):
    w = NSA_KV_W
    hpg = NSA_HEADS // NSA_KV_HEADS
    ncp = kc_ref.shape[1]
    nbp = ov_ref.shape[1]
    qbase = qpos0 + pl.program_id(1) * tq
    qs = _stack_heads(q_ref[0], NSA_HEADS, w)
    kc = kc_ref[0]
    s_all = _dot_nt(qs, kc[:, :w])
    c_idx = lax.broadcasted_iota(jnp.int32, (1, ncp), 1)
    row = lax.broadcasted_iota(jnp.int32, (tq, 1), 0)
    cend_rel = c_idx * NSA_CMP_STRIDE + (NSA_CMP_LEN - 1) - qbase
    valid = (row - cend_rel >= 0) & (c_idx < n_cmp)
    colf = cend_rel.astype(F32)
    ps, psums = [], []
    for h in range(NSA_HEADS):
        s = jnp.where(valid, s_all[h * tq:(h + 1) * tq] + SLOPES_A[h] * colf, NEG)
        m = jnp.max(s, axis=-1, keepdims=True)
        e = jnp.where(valid, jnp.exp(s - m), 0.0)
        p = e / jnp.maximum(jnp.sum(e, axis=-1, keepdims=True), 1e-30)
        ps.append(_bf(p))
        if h % hpg == 0:
            psums.append(p)
        else:
            psums[-1] = psums[-1] + p
    o_all = jnp.dot(jnp.concatenate(ps, axis=0), _bf(kc[:, w:]), preferred_element_type=F32)
    _place_heads([o_all[h * tq:(h + 1) * tq] for h in range(NSA_HEADS)], NSA_HEADS, NSA_KV_HEADS, o_ref)

    lane = lax.broadcasted_iota(jnp.int32, (tq, nbp), 1)
    lanef = lane.astype(F32)
    cur = (qbase + row) // NSA_SEL_BLOCK
    forced = (lane == 0) | ((lane <= cur) & (lane > cur - NSA_N_LOCAL))
    validb = (lane <= cur) & (lane < n_sel_blk)
    n_top = min(NSA_SEL_TOPK, n_sel_blk)
    for g in range(NSA_KV_HEADS):
        imp = _dot2_exact_rhs(psums[g], ov_ref[...])
        score = jnp.where(forced, jnp.inf, jnp.where(validb, imp, -jnp.inf))
        sel_ref[0, g] = _top_select(score, lanef, n_top)


def _cmp_attention(proj3, kc, qpos0, t_real, tq):
    b, tp, _ = proj3.shape
    ncp = kc.shape[1]
    seq = qpos0 + t_real
    n_cmp = seq // NSA_CMP_STRIDE - NSA_CMP_LEN // NSA_CMP_STRIDE + 1
    n_sel_blk = -(-seq // NSA_SEL_BLOCK)
    nbp = _round_up(n_sel_blk, LANES)
    cs = np.arange(ncp)[:, None] * NSA_CMP_STRIDE
    ss = np.arange(nbp)[None, :] * NSA_SEL_BLOCK
    ov = ((cs + NSA_CMP_LEN > ss) & (cs < ss + NSA_SEL_BLOCK)
          & (np.arange(ncp)[:, None] < n_cmp) & (np.arange(nbp)[None, :] < n_sel_blk))
    ov = jnp.asarray(ov.astype(np.float32), BF16)
    qw = NSA_HEADS * NSA_KV_W
    return pl.pallas_call(
        functools.partial(_cmpattn_kernel, tq=tq, qpos0=qpos0, n_cmp=n_cmp, n_sel_blk=n_sel_blk),
        grid=(b, tp // tq),
        in_specs=[pl.BlockSpec((1, tq, qw), lambda bi, i: (bi, i, PC_QA // qw)),
                  pl.BlockSpec((1, ncp, 2 * NSA_KV_W), lambda bi, i: (bi, 0, 0)),
                  pl.BlockSpec(ov.shape, lambda bi, i: (0, 0))],
        out_specs=[pl.BlockSpec((1, tq, NSA_Q_W), lambda bi, i: (bi, i, 0)),
                   pl.BlockSpec((1, NSA_KV_HEADS, tq, nbp), lambda bi, i: (bi, 0, i, 0))],
        out_shape=[jax.ShapeDtypeStruct((b, tp, NSA_Q_W), F32),
                   jax.ShapeDtypeStruct((b, NSA_KV_HEADS, tp, nbp), F32)],
        compiler_params=_cparams("parallel", "arbitrary"),
        name="cmp_attn",
    )(proj3, kc, ov)


def _kmean_kernel(*refs, n_x):
    x_refs, o_ref = refs[:n_x], refs[n_x]
    outs = [jnp.sum(r[0], axis=0, keepdims=True) * (1.0 / MOBA_BLOCK) for r in x_refs]
    o_ref[0, 0] = outs[0] if len(outs) == 1 else jnp.concatenate(outs, axis=0)


KMEAN_BLOCKS_PER_STEP = 4


def _moba_kmean(rows3, page_table, base=0):
    w = MOBA_KV_W
    if page_table is None:
        b, t, _ = rows3.shape
        n_blk = t // MOBA_BLOCK
        bps = math.gcd(KMEAN_BLOCKS_PER_STEP, n_blk)
        steps = n_blk // bps
        out = pl.pallas_call(
            functools.partial(_kmean_kernel, n_x=bps),
            grid=(b, steps),
            in_specs=[pl.BlockSpec((1, MOBA_BLOCK, w), (lambda bi, s, k=k: (bi, s * bps + k, PC_KVB // w)))
                      for k in range(bps)],
            out_specs=pl.BlockSpec((1, 1, bps, w), lambda bi, s: (bi, s, 0, 0)),
            out_shape=jax.ShapeDtypeStruct((b, steps, bps, w), F32),
            compiler_params=_cparams("parallel", "arbitrary"),
            name="kmean",
        )(*([rows3] * bps))
        return out.reshape(b, n_blk, w)
    b, n_pages = page_table.shape
    ppb = MOBA_BLOCK // PAGE_SIZE
    n_blk = n_pages // ppb
    bps = math.gcd(KMEAN_BLOCKS_PER_STEP, n_blk)
    steps = n_blk // bps
    npp = bps * ppb

    def x_map(k):
        return lambda bi, s, pt: (base + pt[bi * n_pages + s * npp + k], 0, 0)

    grid_spec = pltpu.PrefetchScalarGridSpec(
        num_scalar_prefetch=1,
        grid=(b, steps),
        in_specs=[pl.BlockSpec((1, w, PAGE_SIZE), x_map(k)) for k in range(npp)],
        out_specs=pl.BlockSpec((1, 1, bps, w), lambda bi, s, pt: (bi, s, 0, 0)),
    )
    out = pl.pallas_call(
        functools.partial(_kmean_t_kernel, npp=npp),
        grid_spec=grid_spec,
        out_shape=jax.ShapeDtypeStruct((b, steps, bps, w), F32),
        compiler_params=_cparams("parallel", "arbitrary"),
        name="kmean_paged",
    )(page_table.reshape(-1), *([rows3] * npp))
    return out.reshape(b, n_blk, w)


def _kmean_t_kernel(*refs, npp):
    page_refs, o_ref = refs[1:1 + npp], refs[1 + npp]
    ppb = MOBA_BLOCK // PAGE_SIZE
    ones = jnp.ones((SUBLANES, PAGE_SIZE), BF16)
    d = functools.partial(lax.dot_general, dimension_numbers=(((1,), (1,)), ((), ())), preferred_element_type=F32)
    outs = []
    for i in range(npp // ppb):
        acc = None
        for k in range(ppb):
            hi, lo = _split(page_refs[i * ppb + k][0])
            part = d(ones, hi) + d(ones, lo)
            acc = part if acc is None else acc + part
        outs.append(acc[0:1] * (1.0 / MOBA_BLOCK))
    o_ref[0, 0] = outs[0] if len(outs) == 1 else jnp.concatenate(outs, axis=0)


def _mobagate_kernel(q_ref, km_ref, m_ref, *, tq, qpos0, k_top, as_bias):
    w = MOBA_KV_W
    nbp = km_ref.shape[1]
    qbase = qpos0 + pl.program_id(1) * tq
    q = q_ref[0]
    qs = jnp.concatenate([q[:, h * w:(h + 1) * w] for h in range(MOBA_HEADS)], axis=0)
    gs = _dot3_nt(qs, km_ref[0])
    lane = lax.broadcasted_iota(jnp.int32, (tq, nbp), 1)
    lanef = lane.astype(F32)
    own = (qbase + lax.broadcasted_iota(jnp.int32, (tq, 1), 0)) // MOBA_BLOCK
    for h in range(MOBA_HEADS):
        score = jnp.where(lane < own, gs[h * tq:(h + 1) * tq], -jnp.inf)
        sel = jnp.where(lane == own, 1.0, _top_select(score, lanef, k_top))
        m_ref[0, h] = jnp.where(sel > 0.5, 0.0, NEG) if as_bias else sel


def _moba_gate(proj3, kmean, qpos0, t_real, tq, as_bias):
    b, tp, _ = proj3.shape
    n_mb = -(-(qpos0 + t_real) // MOBA_BLOCK)
    nbp = _round_up(n_mb, LANES)
    km = jnp.pad(kmean, ((0, 0), (0, nbp - kmean.shape[1]), (0, 0)))
    qw = MOBA_HEADS * MOBA_KV_W
    return pl.pallas_call(
        functools.partial(_mobagate_kernel, tq=tq, qpos0=qpos0, k_top=min(MOBA_TOPK, n_mb), as_bias=as_bias),
        grid=(b, tp // tq),
        in_specs=[pl.BlockSpec((1, tq, qw), lambda bi, i: (bi, i, PC_QB // qw)),
                  pl.BlockSpec((1, nbp, MOBA_KV_W), lambda bi, i: (bi, 0, 0))],
        out_specs=pl.BlockSpec((1, MOBA_HEADS, tq, nbp), lambda bi, i: (bi, 0, i, 0)),
        out_shape=jax.ShapeDtypeStruct((b, MOBA_HEADS, tp, nbp), F32),
        compiler_params=_cparams("parallel", "arbitrary"),
        name="moba_gate",
    )(proj3, km)


def _attend_tile(qs, k, v, mask, kbase, qbase, m_sc, l_sc, acc_sc, alpha_sc, *, n_heads, n_groups, n_mask, tq,
                 blk_shift, window, slopes, rowbias, causal=True, blk=None):
    tk = k.shape[0]
    s_all = _dot_nt(qs, k)
    delta = qbase - kbase
    col1 = lax.broadcasted_iota(jnp.int32, (1, tk), 1)
    colf = (col1 - delta).astype(F32)
    ok = None
    if causal or window is not None:
        dd = lax.broadcasted_iota(jnp.int32, (tq, tk), 0) - lax.broadcasted_iota(jnp.int32, (tq, tk), 1)
        ok = dd >= -delta
        if window is not None:
            ok = ok & (dd < window - delta)
    expand_mask = n_mask and not rowbias
    if expand_mask:
        nbp = mask.shape[-1]
        blk_col = (kbase + col1) >> blk_shift
        expand = jnp.where(lax.broadcasted_iota(jnp.int32, (nbp, tk), 0) == blk_col, 1.0, 0.0).astype(BF16)
        mexp = jnp.dot(_bf(mask.reshape(n_mask * tq, nbp)), expand, preferred_element_type=F32)
    if rowbias:
        nbp = mask.shape[-1]
        pick = jnp.where(lax.broadcasted_iota(jnp.int32, (nbp, LANES), 0) == blk, 1.0, 0.0).astype(BF16)
        rowb = jnp.dot(_bf(mask.reshape(n_heads * tq, nbp)), pick, preferred_element_type=F32)
    hpm = n_heads // n_mask if n_mask else n_heads
    ps = []
    valid = ok
    for h in range(n_heads):
        rows = slice(h * tq, (h + 1) * tq)
        s = s_all[rows] + slopes[h] * colf
        if rowbias:
            s = s + _lane_rep(rowb[rows], tk)
        elif expand_mask and h % hpm == 0:
            g = h // hpm
            valid = mexp[g * tq:(g + 1) * tq] > 0.5
            if ok is not None:
                valid = ok & valid
        if valid is not None:
            s = jnp.where(valid, s, NEG)
        m_old = m_sc[rows]
        m_new = jnp.maximum(m_old, jnp.max(s, axis=-1, keepdims=True))
        alpha_sc[rows] = jnp.exp(m_old - m_new)
        m_sc[rows] = m_new
        ps.append(_bf(jnp.exp(s - _lane_rep(m_new, tk))))
    p_all = jnp.concatenate(ps, axis=0)
    alpha = alpha_sc[...]
    l_sc[...] = alpha * l_sc[...] + jnp.dot(p_all, jnp.ones((tk, LANES), BF16), preferred_element_type=F32)
    acc_sc[...] = (_lane_rep(alpha, acc_sc.shape[1]) * acc_sc[...]
                   + jnp.dot(p_all, _bf(v), preferred_element_type=F32))


def _lane_rep(x, n):
    return x if n == LANES else jnp.concatenate([x] * (n // LANES), axis=1)


def _attn_finish(m_sc, l_sc, acc_sc, o_ref, n_heads, n_groups, tq):
    w = acc_sc.shape[1]
    o = jnp.where(_lane_rep(m_sc[...], w) > 0.5 * NEG, acc_sc[...] / _lane_rep(l_sc[...], w), 0.0)
    _place_heads([o[h * tq:(h + 1) * tq] for h in range(n_heads)], n_heads, n_groups, o_ref)


def _attn_init(q_ref, q_sc, m_sc, l_sc, acc_sc, n_heads, w):
    q_sc[...] = _stack_heads(q_ref[0], n_heads, w)
    m_sc[...] = jnp.full(m_sc.shape, NEG, F32)
    l_sc[...] = jnp.zeros(l_sc.shape, F32)
    acc_sc[...] = jnp.zeros(acc_sc.shape, F32)


def _kv_tile_range(qi, *, tq, tk, qpos0, kpos0, window, n_kt):
    qlo = qpos0 + qi * tq
    last = jnp.minimum((qlo + tq - 1 - kpos0) // tk, n_kt - 1)
    if window is None:
        first = 0
    else:
        first = jnp.maximum(qlo - window + 1 - kpos0, 0) // tk
    return first, last


def _flash_kernel(*refs, cfg, rng):
    if cfg["n_mask"]:
        q_ref, kv_ref, mask_ref, o_ref, q_sc, m_sc, l_sc, acc_sc, alpha_sc = refs
    else:
        q_ref, kv_ref, o_ref, q_sc, m_sc, l_sc, acc_sc, alpha_sc = refs
        mask_ref = None
    w, tq, tk = cfg["n_groups"] * HEAD_DIM, rng["tq"], rng["tk"]
    qi, j = pl.program_id(1), pl.program_id(2)
    first, last = _kv_tile_range(qi, **rng)

    @pl.when(j == 0)
    def _():
        _attn_init(q_ref, q_sc, m_sc, l_sc, acc_sc, cfg["n_heads"], w)

    jt = first + j
    kbase = rng["kpos0"] + jt * tk
    qbase = rng["qpos0"] + qi * tq

    def tile(causal):
        kv = kv_ref[0]
        _attend_tile(q_sc[...], kv[:, :w], kv[:, w:], None if mask_ref is None else mask_ref[0], kbase, qbase,
                     m_sc, l_sc, acc_sc, alpha_sc, tq=tq, causal=causal, blk=kbase >> cfg["blk_shift"], **cfg)

    if cfg["window"] is None:
        fully_past = kbase + tk <= qbase
        pl.when((jt <= last) & fully_past)(lambda: tile(False))
        pl.when((jt <= last) & jnp.logical_not(fully_past))(lambda: tile(True))
    else:
        pl.when(jt <= last)(lambda: tile(True))

    @pl.when(j == pl.num_programs(2) - 1)
    def _():
        _attn_finish(m_sc, l_sc, acc_sc, o_ref, cfg["n_heads"], cfg["n_groups"], tq)


def _flash_linear(q_arr, q_col, kv_arr, kv_col, mask, cfg, *, tq, tk, qpos0, kpos0, n_steps):
    b, tp, _ = q_arr.shape
    w = cfg["n_groups"] * HEAD_DIM
    n_heads = cfg["n_heads"]
    n_kt = kv_arr.shape[1] // tk
    rng = dict(tq=tq, tk=tk, qpos0=qpos0, kpos0=kpos0, window=cfg["window"], n_kt=n_kt)

    def kv_map(bi, i, j):
        first, last = _kv_tile_range(i, **rng)
        return (bi, jnp.minimum(first + j, last), kv_col)

    in_specs = [pl.BlockSpec((1, tq, n_heads * w), lambda bi, i, j: (bi, i, q_col)),
                pl.BlockSpec((1, tk, 2 * w), kv_map)]
    args = [q_arr, kv_arr]
    if cfg["n_mask"]:
        in_specs.append(pl.BlockSpec((1, cfg["n_mask"], tq, mask.shape[-1]), lambda bi, i, j: (bi, 0, i, 0)))
        args.append(mask)
    return pl.pallas_call(
        functools.partial(_flash_kernel, cfg=cfg, rng=rng),
        grid=(b, tp // tq, n_steps),
        in_specs=in_specs,
        out_specs=pl.BlockSpec((1, tq, n_heads * HEAD_DIM), lambda bi, i, j: (bi, i, 0)),
        out_shape=jax.ShapeDtypeStruct((b, tp, n_heads * HEAD_DIM), F32),
        scratch_shapes=[pltpu.VMEM((n_heads * tq, w), BF16), pltpu.VMEM((n_heads * tq, LANES), F32),
                        pltpu.VMEM((n_heads * tq, LANES), F32), pltpu.VMEM((n_heads * tq, w), F32),
                        pltpu.VMEM((n_heads * tq, LANES), F32)],
        compiler_params=_cparams("parallel", "parallel", "arbitrary"),
        name="flash",
    )(*args)


DECODE_PAGES_PER_STEP = 16


def _decode_update(qs, k_op, v_op, transposed, kbase, qpos_col, slope_col, mrow, m_sc, l_sc, acc_sc, *,
                   blk_shift, window, qpos0):
    s = jnp.dot(qs, _bf(k_op), preferred_element_type=F32) if transposed else _dot_nt(qs, k_op)
    tk = s.shape[1]
    kpos = kbase + lax.broadcasted_iota(jnp.int32, (1, tk), 1)
    d = qpos_col - kpos
    ok = d >= 0
    if window is not None:
        ok = ok & (d < window)
    if mrow is not None:
        nbp = mrow.shape[1]
        expand = jnp.where(lax.broadcasted_iota(jnp.int32, (nbp, tk), 0) == (kpos >> blk_shift), 1.0, 0.0)
        ok = ok & (jnp.dot(mrow, expand.astype(BF16), preferred_element_type=F32) > 0.5)
    s = jnp.where(ok, s + slope_col * (kpos - qpos0).astype(F32), NEG)
    m_old = m_sc[...]
    m_new = jnp.maximum(m_old, jnp.max(s, axis=-1, keepdims=True))
    alpha = jnp.exp(m_old - m_new)
    p = _bf(jnp.exp(s - m_new))
    l_sc[...] = alpha * l_sc[...] + jnp.sum(p.astype(F32), axis=-1, keepdims=True)
    m_sc[...] = m_new
    pv = _dot_nt(p, v_op) if transposed else jnp.dot(p, _bf(v_op), preferred_element_type=F32)
    acc_sc[...] = alpha * acc_sc[...] + pv


def _decode_kernel(*refs, cfg, npp, tq, qpos0, kpos0, tile_rows):
    has_mask = bool(cfg["n_mask"])
    page_refs = refs[2:2 + npp]
    q_ref, tail_ref = refs[1], refs[2 + npp]
    rest = refs[3 + npp:]
    if has_mask:
        mask_ref, o_ref, q_sc, m_sc, l_sc, acc_sc, slope_sc, mrow_sc = rest
    else:
        o_ref, q_sc, m_sc, l_sc, acc_sc, slope_sc = rest
    n_heads = cfg["n_heads"]
    w = cfg["n_groups"] * HEAD_DIM
    j = pl.program_id(1)
    n_steps = pl.num_programs(1)
    row = lax.broadcasted_iota(jnp.int32, (n_heads * tq, 1), 0)

    @pl.when(j == 0)
    def _():
        _attn_init(q_ref, q_sc, m_sc, l_sc, acc_sc, n_heads, w)
        slope = jnp.zeros((n_heads * tq, 1), F32)
        for h in range(n_heads):
            slope = jnp.where(row // tq == h, cfg["slopes"][h], slope)
        slope_sc[...] = slope
        if has_mask:
            hpm = n_heads // cfg["n_mask"]
            mrow_sc[...] = _bf(jnp.concatenate([mask_ref[0, h // hpm] for h in range(n_heads)], axis=0))

    qpos_col = qpos0 + row % tq
    kw = dict(blk_shift=cfg["blk_shift"], window=cfg["window"], qpos0=qpos0)
    mrow = mrow_sc[...] if has_mask else None

    @pl.when(j < n_steps - 1)
    def _():
        kt = [r[0, :w, :] for r in page_refs]
        vt = [r[0, w:, :] for r in page_refs]
        kt = kt[0] if npp == 1 else jnp.concatenate(kt, axis=1)
        vt = vt[0] if npp == 1 else jnp.concatenate(vt, axis=1)
        _decode_update(q_sc[...], kt, vt, True, kpos0 + j * (npp * tile_rows), qpos_col, slope_sc[...], mrow,
                       m_sc, l_sc, acc_sc, **kw)

    @pl.when(j == n_steps - 1)
    def _():
        tail = tail_ref[0]
        _decode_update(q_sc[...], tail[:, :w], tail[:, w:], False, qpos0, qpos_col, slope_sc[...], mrow,
                       m_sc, l_sc, acc_sc, **kw)
        o = jnp.where(m_sc[...] > 0.5 * NEG, acc_sc[...] / l_sc[...], 0.0)
        _place_heads([o[h * tq:(h + 1) * tq] for h in range(n_heads)], n_heads, cfg["n_groups"], o_ref)


def _decode_attn(proj3, q_col, tail_col, tiles_t, table, base, mask, cfg, *, qpos0, kpos0):
    b, tq, _ = proj3.shape
    w = cfg["n_groups"] * HEAD_DIM
    n_heads = cfg["n_heads"]
    n_tiles = table.shape[1]
    tile_rows = tiles_t.shape[2]
    npp = math.gcd(DECODE_PAGES_PER_STEP, n_tiles)
    steps = n_tiles // npp
    r = n_heads * tq

    def tile_map(k):
        return lambda bi, j, tb: (base + tb[bi * n_tiles + jnp.minimum(j, steps - 1) * npp + k], 0, 0)

    in_specs = ([pl.BlockSpec((1, tq, n_heads * w), lambda bi, j, tb: (bi, 0, q_col))]
                + [pl.BlockSpec((1, 2 * w, tile_rows), tile_map(k)) for k in range(npp)]
                + [pl.BlockSpec((1, tq, 2 * w), lambda bi, j, tb: (bi, 0, tail_col))])
    args = [table.reshape(-1), proj3] + [tiles_t] * npp + [proj3]
    scratch = [pltpu.VMEM((r, w), BF16), pltpu.VMEM((r, 1), F32), pltpu.VMEM((r, 1), F32), pltpu.VMEM((r, w), F32),
               pltpu.VMEM((r, 1), F32)]
    if cfg["n_mask"]:
        in_specs.append(pl.BlockSpec((1, cfg["n_mask"], tq, mask.shape[-1]), lambda bi, j, tb: (bi, 0, 0, 0)))
        args.append(mask)
        scratch.append(pltpu.VMEM((r, mask.shape[-1]), BF16))
    grid_spec = pltpu.PrefetchScalarGridSpec(
        num_scalar_prefetch=1,
        grid=(b, steps + 1),
        in_specs=in_specs,
        out_specs=pl.BlockSpec((1, tq, n_heads * HEAD_DIM), lambda bi, j, tb: (bi, 0, 0)),
        scratch_shapes=scratch,
    )
    return pl.pallas_call(
        functools.partial(_decode_kernel, cfg=cfg, npp=npp, tq=tq, qpos0=qpos0, kpos0=kpos0, tile_rows=tile_rows),
        grid_spec=grid_spec,
        out_shape=jax.ShapeDtypeStruct((b, tq, n_heads * HEAD_DIM), F32),
        compiler_params=_cparams("parallel", "arbitrary"),
        name="decode_attn",
    )(*args)


def _attn_cfg(kind, rowbias=False):
    if kind == "moba":
        return dict(n_heads=MOBA_HEADS, n_groups=MOBA_KV_HEADS, n_mask=MOBA_HEADS, rowbias=rowbias,
                    blk_shift=int(math.log2(MOBA_BLOCK)), window=None, slopes=SLOPES_B)
    if kind == "sel":
        return dict(n_heads=NSA_HEADS, n_groups=NSA_KV_HEADS, n_mask=NSA_KV_HEADS, rowbias=False,
                    blk_shift=int(math.log2(NSA_SEL_BLOCK)), window=None, slopes=SLOPES_A)
    return dict(n_heads=NSA_HEADS, n_groups=NSA_KV_HEADS, n_mask=0, rowbias=False, blk_shift=0,
                window=NSA_WINDOW, slopes=SLOPES_A)


S5_CHUNK = 4 * LANES
S5_GCHUNK = S5_CHUNK // S5_STATE


def _s5_params(lp):
    lr, li = lp["s5_lambda_re"], lp["s5_lambda_im"]
    dt = jnp.exp(lp["s5_log_dt"])[:, None]
    mag = jnp.exp(lr * dt)
    a_re = mag * jnp.cos(li * dt)
    a_im = mag * jnp.sin(li * dt)
    den = lr * lr + li * li
    f_re = ((a_re - 1.0) * lr + a_im * li) / den
    f_im = (a_im * lr - (a_re - 1.0) * li) / den
    b_re, b_im = lp["s5_b_re"], lp["s5_b_im"]
    bb_re = f_re[..., None] * b_re - f_im[..., None] * b_im
    bb_im = f_re[..., None] * b_im + f_im[..., None] * b_re
    n_ch = S5_GROUPS // S5_GCHUNK
    eye = np.eye(S5_GCHUNK, dtype=np.float32)

    def in_w(bb):
        x = bb.reshape(n_ch, S5_GCHUNK, S5_STATE, S5_GROUP_CH).transpose(0, 1, 3, 2)
        x = x[:, :, :, None, :] * eye[None, :, None, :, None]
        return x.reshape(n_ch, S5_GCHUNK * S5_GROUP_CH, S5_CHUNK)

    def out_w(c):
        x = c.reshape(n_ch, S5_GCHUNK, S5_GROUP_CH, S5_STATE).transpose(0, 1, 3, 2)
        x = x[:, :, :, None, :] * eye[None, :, None, :, None]
        return x.reshape(n_ch, S5_CHUNK, S5_GCHUNK * S5_GROUP_CH).astype(BF16)

    return dict(a_re=a_re.reshape(1, S5_LANES), a_im=a_im.reshape(1, S5_LANES),
                bw_re=in_w(bb_re), bw_im=in_w(bb_im),
                cw_re=out_w(lp["s5_c_re"]), cw_im=out_w(lp["s5_c_im"]),
                d=lp["s5_d"].reshape(1, S5_WIDTH))


def _s5_out_proj(h_re, h_im, cwr_ref, cwi_ref):
    cols = []
    for c in range(S5_LANES // S5_CHUNK):
        sl = slice(c * S5_CHUNK, (c + 1) * S5_CHUNK)
        cols.append(jnp.dot(_bf(h_re[:, sl]), cwr_ref[c], preferred_element_type=F32)
                    - jnp.dot(_bf(h_im[:, sl]), cwi_ref[c], preferred_element_type=F32))
    return jnp.concatenate(cols, axis=1)


def _s5scan_kernel(u_ref, h0r_ref, h0i_ref, ar_ref, ai_ref, bwr_ref, bwi_ref, cwr_ref, cwi_ref, d_ref,
                   y_ref, hr_ref, hi_ref, bur_sc, bui_sc, hr_sc, hi_sc, *, n_par, n_j):
    @pl.when(pl.program_id(0) == 0)
    def _():
        hr_sc[...] = h0r_ref[...]
        hi_sc[...] = h0i_ref[...]

    u = u_ref[...]
    gw = S5_GCHUNK * S5_GROUP_CH
    for c in range(S5_LANES // S5_CHUNK):
        uc = u[:, c * gw:(c + 1) * gw]
        bur_sc[:, c * S5_CHUNK:(c + 1) * S5_CHUNK] = _dot3(uc, bwr_ref[c])
        bui_sc[:, c * S5_CHUNK:(c + 1) * S5_CHUNK] = _dot3(uc, bwi_ref[c])

    for c in range(S5_LANES // S5_CHUNK):
        sl = slice(c * S5_CHUNK, (c + 1) * S5_CHUNK)
        a_re = jnp.broadcast_to(ar_ref[:, sl], (n_par, S5_CHUNK))
        a_im = jnp.broadcast_to(ai_ref[:, sl], (n_par, S5_CHUNK))

        def step(jj, carry):
            h_re, h_im = carry
            rows = pl.ds(pl.multiple_of(jj * n_par, n_par), n_par)
            n_re = a_re * h_re - a_im * h_im + bur_sc[rows, sl]
            n_im = a_re * h_im + a_im * h_re + bui_sc[rows, sl]
            bur_sc[rows, sl] = n_re
            bui_sc[rows, sl] = n_im
            return n_re, n_im

        h_re, h_im = lax.fori_loop(0, n_j, step, (hr_sc[:, sl], hi_sc[:, sl]))
        hr_sc[:, sl] = h_re
        hi_sc[:, sl] = h_im

    y_ref[...] = _s5_out_proj(bur_sc[...], bui_sc[...], cwr_ref, cwi_ref) + d_ref[...] * u
    hr_ref[...] = hr_sc[...]
    hi_ref[...] = hi_sc[...]


def _s5_scan(u_rows, h0_re, h0_im, sp, n_par, jc):
    rows = u_rows.shape[0]
    n_j_total = rows // n_par
    steps = n_j_total // jc
    full = lambda a: pl.BlockSpec(a.shape, lambda i: (0,) * a.ndim)
    ins = [u_rows, h0_re, h0_im, sp["a_re"], sp["a_im"], sp["bw_re"], sp["bw_im"], sp["cw_re"], sp["cw_im"], sp["d"]]
    return pl.pallas_call(
        functools.partial(_s5scan_kernel, n_par=n_par, n_j=jc),
        grid=(steps,),
        in_specs=[pl.BlockSpec((jc * n_par, S5_WIDTH), lambda i: (i, 0))] + [full(a) for a in ins[1:]],
        out_specs=[pl.BlockSpec((jc * n_par, S5_WIDTH), lambda i: (i, 0)),
                   pl.BlockSpec((n_par, S5_LANES), lambda i: (0, 0)),
                   pl.BlockSpec((n_par, S5_LANES), lambda i: (0, 0))],
        out_shape=[jax.ShapeDtypeStruct((rows, S5_WIDTH), F32),
                   jax.ShapeDtypeStruct((n_par, S5_LANES), F32),
                   jax.ShapeDtypeStruct((n_par, S5_LANES), F32)],
        scratch_shapes=[pltpu.VMEM((jc * n_par, S5_LANES), F32), pltpu.VMEM((jc * n_par, S5_LANES), F32),
                        pltpu.VMEM((n_par, S5_LANES), F32), pltpu.VMEM((n_par, S5_LANES), F32)],
        compiler_params=_cparams("arbitrary"),
        name="s5_scan",
    )(*ins)


def _s5fix_kernel(y_ref, er_ref, ei_ref, ar_ref, ai_ref, cwr_ref, cwi_ref, o_ref, fr_ref, fi_ref,
                  cr_sc, ci_sc, pr_sc, pi_sc, xr_sc, xi_sc, *, n_seg, n_j, n_j_total):
    a_re = ar_ref[...]
    a_im = ai_ref[...]

    @pl.when(pl.program_id(0) == 0)
    def _():
        def pw_step(_, carry):
            p_re, p_im = carry
            return a_re * p_re - a_im * p_im, a_re * p_im + a_im * p_re

        al_re, al_im = lax.fori_loop(0, n_j_total - 1, pw_step, (a_re, a_im))
        c_re = jnp.zeros((1, S5_LANES), F32)
        c_im = jnp.zeros((1, S5_LANES), F32)
        for s in range(n_seg):
            cr_sc[s:s + 1, :] = c_re
            ci_sc[s:s + 1, :] = c_im
            e_re, e_im = er_ref[s:s + 1, :], ei_ref[s:s + 1, :]
            c_re, c_im = (e_re + al_re * c_re - al_im * c_im, e_im + al_re * c_im + al_im * c_re)
        fr_ref[...] = c_re
        fi_ref[...] = c_im
        pr_sc[...] = jnp.broadcast_to(a_re, pr_sc.shape)
        pi_sc[...] = jnp.broadcast_to(a_im, pi_sc.shape)

    for c in range(S5_LANES // S5_CHUNK):
        sl = slice(c * S5_CHUNK, (c + 1) * S5_CHUNK)
        ab_re = jnp.broadcast_to(a_re[:, sl], (n_seg, S5_CHUNK))
        ab_im = jnp.broadcast_to(a_im[:, sl], (n_seg, S5_CHUNK))
        c_re, c_im = cr_sc[:, sl], ci_sc[:, sl]

        def step(jj, carry):
            p_re, p_im = carry
            rows = pl.ds(pl.multiple_of(jj * n_seg, n_seg), n_seg)
            xr_sc[rows, sl] = p_re * c_re - p_im * c_im
            xi_sc[rows, sl] = p_re * c_im + p_im * c_re
            return ab_re * p_re - ab_im * p_im, ab_re * p_im + ab_im * p_re

        p_re, p_im = lax.fori_loop(0, n_j, step, (pr_sc[:, sl], pi_sc[:, sl]))
        pr_sc[:, sl] = p_re
        pi_sc[:, sl] = p_im

    o_ref[...] = y_ref[...] + _s5_out_proj(xr_sc[...], xi_sc[...], cwr_ref, cwi_ref)


def _s5_fix(y_rows, end_re, end_im, sp, n_seg, jc):
    rows = y_rows.shape[0]
    n_j_total = rows // n_seg
    steps = n_j_total // jc
    full = lambda a: pl.BlockSpec(a.shape, lambda i: (0,) * a.ndim)
    ins = [y_rows, end_re, end_im, sp["a_re"], sp["a_im"], sp["cw_re"], sp["cw_im"]]
    seg = lambda: pltpu.VMEM((n_seg, S5_LANES), F32)
    return pl.pallas_call(
        functools.partial(_s5fix_kernel, n_seg=n_seg, n_j=jc, n_j_total=n_j_total),
        grid=(steps,),
        in_specs=[pl.BlockSpec((jc * n_seg, S5_WIDTH), lambda i: (i, 0))] + [full(a) for a in ins[1:]],
        out_specs=[pl.BlockSpec((jc * n_seg, S5_WIDTH), lambda i: (i, 0)),
                   pl.BlockSpec((1, S5_LANES), lambda i: (0, 0)),
                   pl.BlockSpec((1, S5_LANES), lambda i: (0, 0))],
        out_shape=[jax.ShapeDtypeStruct((rows, S5_WIDTH), F32),
                   jax.ShapeDtypeStruct((1, S5_LANES), F32),
                   jax.ShapeDtypeStruct((1, S5_LANES), F32)],
        scratch_shapes=[seg(), seg(), seg(), seg(),
                        pltpu.VMEM((jc * n_seg, S5_LANES), F32), pltpu.VMEM((jc * n_seg, S5_LANES), F32)],
        compiler_params=_cparams("arbitrary"),
        name="s5_fix",
    )(*ins)


S5_SEGMENTS = SUBLANES


def _s5_prompt(u, sp):
    t = u.shape[0]
    n_seg = S5_SEGMENTS
    seg_len = t // n_seg
    jc = min(64, seg_len)
    u_rows = u.reshape(n_seg, seg_len, S5_WIDTH).transpose(1, 0, 2).reshape(t, S5_WIDTH)
    zero = jnp.zeros((n_seg, S5_LANES), F32)
    y_loc, end_re, end_im = _s5_scan(u_rows, zero, zero, sp, n_seg, jc)
    y_rows, f_re, f_im = _s5_fix(y_loc, end_re, end_im, sp, n_seg, jc)
    y = y_rows.reshape(seg_len, n_seg, S5_WIDTH).transpose(1, 0, 2).reshape(t, S5_WIDTH)
    return y, f_re, f_im


def _s5_sample(u3, h0_re, h0_im, sp):
    b, t, _ = u3.shape
    u_rows = u3.transpose(1, 0, 2).reshape(t * b, S5_WIDTH)
    y_rows, h_re, h_im = _s5_scan(u_rows, h0_re, h0_im, sp, b, t)
    return y_rows.reshape(t, b, S5_WIDTH).transpose(1, 0, 2), h_re, h_im


def _merge_kernel(x_ref, oc_ref, os_ref, ow_ref, om_ref, y5_ref, ga_ref, gm_ref, wglu_ref, bglu_ref,
                  wn_ref, wm_ref, w5_ref, wo_ref, g_ref, b_ref, o_ref):
    qw = NSA_Q_W
    ga = ga_ref[...]
    o_nsa = (_sigmoid(ga[:, :qw]) * oc_ref[...] + _sigmoid(ga[:, qw:2 * qw]) * os_ref[...]
             + _sigmoid(ga[:, 2 * qw:]) * ow_ref[...])
    z = _gelu(y5_ref[...])
    o_s5 = z * _sigmoid(jnp.dot(_bf(z), wglu_ref[...], preferred_element_type=F32) + bglu_ref[...])
    gm = gm_ref[...]
    d = D_MODEL
    merged = (_sigmoid(gm[:, :d]) * jnp.dot(_bf(o_nsa), wn_ref[...], preferred_element_type=F32)
              + _sigmoid(gm[:, d:2 * d]) * jnp.dot(_bf(om_ref[...]), wm_ref[...], preferred_element_type=F32)
              + _sigmoid(gm[:, 2 * d:]) * jnp.dot(_bf(o_s5), w5_ref[...], preferred_element_type=F32))
    mix = jnp.dot(_bf(merged), wo_ref[...], preferred_element_type=F32)
    o_ref[...] = _layer_norm(DN_ALPHA * x_ref[...] + mix, g_ref[...], b_ref[...])


def _merge(x2, o_c, o_s, o_w, o_m, y5, proj2, lp, tm):
    m = x2.shape[0]
    row = lambda n: pl.BlockSpec((tm, n), lambda i: (i, 0))
    full = lambda a: pl.BlockSpec(a.shape, lambda i: (0,) * a.ndim)
    ws = [lp["s5_w_glu"].astype(BF16), lp["s5_b_glu"].reshape(1, S5_WIDTH),
          lp["w_br_nsa"].astype(BF16), lp["w_br_moba"].astype(BF16), lp["w_br_s5"].astype(BF16),
          lp["w_out"].astype(BF16), lp["ln1_g"].reshape(1, D_MODEL), lp["ln1_b"].reshape(1, D_MODEL)]
    gaw, gmw = 3 * NSA_Q_W, N_BRANCH * D_MODEL
    return pl.pallas_call(
        _merge_kernel,
        grid=(m // tm,),
        in_specs=[row(D_MODEL), row(NSA_Q_W), row(NSA_Q_W), row(NSA_Q_W), row(MOBA_Q_W), row(S5_WIDTH),
                  pl.BlockSpec((tm, gaw), lambda i: (i, PC_GA // gaw)),
                  pl.BlockSpec((tm, gmw), lambda i: (i, PC_GM // gmw))] + [full(a) for a in ws],
        out_specs=row(D_MODEL),
        out_shape=jax.ShapeDtypeStruct((m, D_MODEL), F32),
        compiler_params=_cparams("parallel"),
        name="merge",
    )(x2, o_c, o_s, o_w, o_m, y5, proj2, proj2, *ws)


def _router_kernel(x_ref, w_ref, b_ref, o_ref):
    logits = _dot3(x_ref[...], w_ref[...]) + b_ref[...]
    lane = lax.broadcasted_iota(jnp.int32, logits.shape, 1)
    lanef = lane.astype(F32)
    logits = jnp.where(lane < N_EXPERTS, logits, -jnp.inf)
    sel = _top_select(logits, lanef, MOE_TOP_K)
    m = jnp.max(logits, axis=-1, keepdims=True)
    e = jnp.where(sel > 0.5, jnp.exp(logits - m), 0.0)
    o_ref[...] = e / jnp.sum(e, axis=-1, keepdims=True)


def _router(x2, w_router, b_router, tm):
    m = x2.shape[0]
    w = jnp.pad(w_router, ((0, 0), (0, LANES - N_EXPERTS)))
    bias = jnp.pad(b_router, (0, LANES - N_EXPERTS)).reshape(1, LANES)
    return pl.pallas_call(
        _router_kernel,
        grid=(m // tm,),
        in_specs=[pl.BlockSpec((tm, D_MODEL), lambda i: (i, 0)),
                  pl.BlockSpec(w.shape, lambda i: (0, 0)), pl.BlockSpec(bias.shape, lambda i: (0, 0))],
        out_specs=pl.BlockSpec((tm, LANES), lambda i: (i, 0)),
        out_shape=jax.ShapeDtypeStruct((m, LANES), F32),
        compiler_params=_cparams("parallel"),
        name="router",
    )(x2, w, bias)


def _ffn_kernel(*refs, use_comb):
    if use_comb:
        x_ref, comb_ref, wg_ref, wu_ref, wd_ref, g_ref, b_ref, o_ref, acc_sc = refs
    else:
        x_ref, wg_ref, wu_ref, wd_ref, g_ref, b_ref, o_ref, acc_sc = refs
    f = pl.program_id(1)

    @pl.when(f == 0)
    def _():
        acc_sc[...] = jnp.zeros(acc_sc.shape, F32)

    xb = _bf(x_ref[...])
    wg = wg_ref[0] if use_comb else wg_ref[...]
    wu = wu_ref[0] if use_comb else wu_ref[...]
    wd = wd_ref[0] if use_comb else wd_ref[...]
    gate = jnp.dot(xb, _bf(wg), preferred_element_type=F32)
    up = jnp.dot(xb, _bf(wu), preferred_element_type=F32)
    h = gate * _sigmoid(gate) * up
    if use_comb:
        comb = comb_ref[...]
        lane = lax.broadcasted_iota(jnp.int32, comb.shape, 1)
        h = h * jnp.sum(jnp.where(lane == f, comb, 0.0), axis=-1, keepdims=True)
    acc_sc[...] += jnp.dot(_bf(h), _bf(wd), preferred_element_type=F32)

    @pl.when(f == pl.num_programs(1) - 1)
    def _():
        o_ref[...] = _layer_norm(DN_ALPHA * x_ref[...] + acc_sc[...], g_ref[...], b_ref[...])


def _ffn(x2, comb, wg, wu, wd, ln_g, ln_b, tm, tf):
    m = x2.shape[0]
    use_comb = comb is not None
    if use_comb:
        n_f = wg.shape[0]
        w_specs = [pl.BlockSpec((1,) + wg.shape[1:], lambda i, f: (f, 0, 0)),
                   pl.BlockSpec((1,) + wu.shape[1:], lambda i, f: (f, 0, 0)),
                   pl.BlockSpec((1,) + wd.shape[1:], lambda i, f: (f, 0, 0))]
    else:
        n_f = wg.shape[1] // tf
        w_specs = [pl.BlockSpec((D_MODEL, tf), lambda i, f: (0, f)),
                   pl.BlockSpec((D_MODEL, tf), lambda i, f: (0, f)),
                   pl.BlockSpec((tf, D_MODEL), lambda i, f: (f, 0))]
    vec = pl.BlockSpec((1, D_MODEL), lambda i, f: (0, 0))
    in_specs = [pl.BlockSpec((tm, D_MODEL), lambda i, f: (i, 0))]
    args = [x2]
    if use_comb:
        in_specs.append(pl.BlockSpec((tm, LANES), lambda i, f: (i, 0)))
        args.append(comb)
    return pl.pallas_call(
        functools.partial(_ffn_kernel, use_comb=use_comb),
        grid=(m // tm, n_f),
        in_specs=in_specs + w_specs + [vec, vec],
        out_specs=pl.BlockSpec((tm, D_MODEL), lambda i, f: (i, 0)),
        out_shape=jax.ShapeDtypeStruct((m, D_MODEL), F32),
        scratch_shapes=[pltpu.VMEM((tm, D_MODEL), F32)],
        compiler_params=_cparams("parallel", "arbitrary"),
        name="ffn",
    )(*args, wg.astype(BF16), wu.astype(BF16), wd.astype(BF16), ln_g.reshape(1, D_MODEL), ln_b.reshape(1, D_MODEL))


def _token_mixer(x3, t_real, qpos0, past, lp, sp, h0_re, h0_im):
    b, tp, _ = x3.shape
    rows = b * tp
    x2 = x3.reshape(rows, D_MODEL)
    tm = min(512, rows)
    proj2 = _matmul(x2, lp["w_proj"], tm, 256)
    proj3 = proj2.reshape(b, tp, PC_END)
    kva = proj3[:, :t_real, PC_KVA:PC_END]
    new_cmp, new_sel, new_win = (kva[:, :, i * 2 * NSA_KV_W:(i + 1) * 2 * NSA_KV_W] for i in range(3))
    new_moba = proj3[:, :t_real, PC_KVB:PC_KVA]
    cw = _cmp_weights(lp["phi1_k"], lp["phi1_v"])
    cmp_args = (lp["pe_k"], lp["pe_v"], lp["phi1_k"], lp["phi1_v"], lp["phi2_k"], lp["phi2_v"])
    rw_a, rw_b = 2 * NSA_KV_W, 2 * MOBA_KV_W

    qa_col = PC_QA // (NSA_HEADS * NSA_KV_W)
    qb_col = PC_QB // (MOBA_HEADS * MOBA_KV_W)
    if past is None:
        assert qpos0 == 0 and b == 1 and t_real == tp and t_real % MOBA_BLOCK == 0
        tq_sel = 128
        tq = 256 if tp % 256 == 0 else 128
        tk = 512 if tp % 512 == 0 else tp
        ab = _cmp_ab_linear(new_cmp.reshape(t_real // NSA_CMP_STRIDE, NSA_CMP_STRIDE * rw_a), cw)
        kc = _cmp_finish(ab, 1, *cmp_args)
        o_c, sel_mask = _cmp_attention(proj3, kc, 0, t_real, tq_sel)
        n_kt = tp // tk
        o_s = _flash_linear(proj3, qa_col, proj3, (PC_KVA + rw_a) // rw_a, sel_mask,
                            _attn_cfg("sel"), tq=tq, tk=tk, qpos0=0, kpos0=0, n_steps=n_kt)
        n_win = min(n_kt, (NSA_WINDOW + tq - 2) // tk + 2)
        o_w = _flash_linear(proj3, qa_col, proj3, (PC_KVA + 2 * rw_a) // rw_a, None,
                            _attn_cfg("win"), tq=tq, tk=tk, qpos0=0, kpos0=0, n_steps=n_win)
        kmean = _moba_kmean(proj3, None)
        mb_bias = _moba_gate(proj3, kmean, 0, t_real, tq_sel, True)
        o_m = _flash_linear(proj3, qb_col, proj3, PC_KVB // rw_b, mb_bias, _attn_cfg("moba", rowbias=True),
                            tq=tq, tk=MOBA_BLOCK, qpos0=0, kpos0=0, n_steps=tp // MOBA_BLOCK)
        y5, h_re, h_im = _s5_prompt(proj2[:, PC_U:PC_U + S5_WIDTH], sp)
        keep = min(NSA_WINDOW, t_real)
        win_state = new_win[:, t_real - keep:]
    else:
        assert t_real < NSA_CMP_STRIDE and qpos0 % MOBA_BLOCK == 0 and tp % SUBLANES == 0
        pt, base = past["page_table"], past["page_base"]
        ab = _cmp_ab_paged(past["cmp_t"], base, pt, cw)
        kc = _cmp_finish(ab, b, *cmp_args)
        o_c, sel_mask = _cmp_attention(proj3, kc, qpos0, t_real, tp)
        o_s = _decode_attn(proj3, qa_col, (PC_KVA + rw_a) // rw_a, past["sel_t"], pt, base, sel_mask,
                           _attn_cfg("sel"), qpos0=qpos0, kpos0=0)
        wb = past["win_t"].shape[2]
        o_w = _decode_attn(proj3, qa_col, (PC_KVA + 2 * rw_a) // rw_a, past["win_t"],
                           jnp.arange(b, dtype=jnp.int32).reshape(b, 1), past["win_base"], None,
                           _attn_cfg("win"), qpos0=qpos0, kpos0=qpos0 - wb)
        kmean = _moba_kmean(past["moba_t"], pt, base)
        mb_mask = _moba_gate(proj3, kmean, qpos0, t_real, tp, False)
        o_m = _decode_attn(proj3, qb_col, PC_KVB // rw_b, past["moba_t"], pt, base, mb_mask,
                           _attn_cfg("moba"), qpos0=qpos0, kpos0=0)
        y5r, h_re, h_im = _s5_sample(proj3[:, :t_real, PC_U:PC_U + S5_WIDTH], h0_re, h0_im, sp)
        y5 = jnp.pad(y5r, ((0, 0), (0, tp - t_real), (0, 0))).reshape(rows, S5_WIDTH)
        win_state = jnp.concatenate([past["win_rows"], new_win], axis=1)[:, -wb:]

    flat = lambda a: a.reshape(rows, a.shape[-1])
    x1 = _merge(x2, flat(o_c), flat(o_s), flat(o_w), flat(o_m), y5, proj2, lp, min(256, rows))
    return x1, (new_cmp, new_sel, win_state, new_moba, h_re, h_im)


def _layer(l, x3, t_real, qpos0, past, h0_re, h0_im, lp, sp, ffn_p):
    b, tp, _ = x3.shape
    x1, st = _token_mixer(x3, t_real, qpos0, past, lp, sp, h0_re, h0_im)
    tm = min(512, x1.shape[0])
    if l % 2 == 0:
        x2 = _ffn(x1, None, ffn_p["wg"], ffn_p["wu"], ffn_p["wd"], lp["ln2_g"], lp["ln2_b"], tm, ffn_p["tf"])
    else:
        comb = _router(x1, ffn_p["w_router"], ffn_p["b_router"], tm)
        x2 = _ffn(x1, comb, ffn_p["wg"], ffn_p["wu"], ffn_p["wd"], lp["ln2_g"], lp["ln2_b"], tm, None)
    return x2.reshape(b, tp, D_MODEL), st


def kernel(x_prompt, x_sample, cache_nsa_cmp, cache_nsa_sel, cache_nsa_win, cache_moba, state_s5_re, state_s5_im,
           page_table, w_in, pe_k, phi1_k, phi2_k, pe_v, phi1_v, phi2_v, s5_lambda_re, s5_lambda_im, s5_log_dt,
           s5_b_re, s5_b_im, s5_c_re, s5_c_im, s5_d, s5_w_glu, s5_b_glu, w_br_nsa, w_br_moba, w_br_s5, w_out,
           ln1_g, ln1_b, ln2_g, ln2_b, ffn_w_gate, ffn_w_up, ffn_w_down, moe_w_router, moe_b_router,
           moe_w_gate, moe_w_up, moe_w_down):
    depth = w_in.shape[0]
    bp, seq, _ = x_prompt.shape
    bs, dec_seq, _ = x_sample.shape
    n_pool = cache_moba.shape[1]
    past_len = page_table.shape[1] * PAGE_SIZE
    tps = _round_up(dec_seq, SUBLANES)
    y_p = x_prompt
    y_s = jnp.pad(x_sample, ((0, 0), (0, tps - dec_seq), (0, 0)))
    rw_a, rw_b = 2 * NSA_KV_W, 2 * MOBA_KV_W

    def tiles_t(cache):
        d, n, r = cache.shape[:3]
        return cache.transpose(0, 1, 3, 4, 5, 2).reshape(d * n, -1, r)

    cmp_t, sel_t, moba_t, win_t = (tiles_t(c) for c in (cache_nsa_cmp, cache_nsa_sel, cache_moba, cache_nsa_win))
    p_st, s_st = [], []
    for l in range(depth):
        lp = dict(w_proj=_proj_weight(w_in[l]), pe_k=pe_k[l], phi1_k=phi1_k[l], phi2_k=phi2_k[l], pe_v=pe_v[l],
                  phi1_v=phi1_v[l], phi2_v=phi2_v[l], s5_lambda_re=s5_lambda_re[l], s5_lambda_im=s5_lambda_im[l],
                  s5_log_dt=s5_log_dt[l], s5_b_re=s5_b_re[l], s5_b_im=s5_b_im[l], s5_c_re=s5_c_re[l],
                  s5_c_im=s5_c_im[l], s5_d=s5_d[l], s5_w_glu=s5_w_glu[l], s5_b_glu=s5_b_glu[l],
                  w_br_nsa=w_br_nsa[l], w_br_moba=w_br_moba[l], w_br_s5=w_br_s5[l], w_out=w_out[l],
                  ln1_g=ln1_g[l], ln1_b=ln1_b[l], ln2_g=ln2_g[l], ln2_b=ln2_b[l])
        sp = _s5_params(lp)
        if l % 2 == 0:
            d_ff = ffn_w_gate.shape[2]
            tf = d_ff // 2 if (d_ff // 2) % LANES == 0 else d_ff
            ffn_p = dict(wg=ffn_w_gate[l // 2], wu=ffn_w_up[l // 2], wd=ffn_w_down[l // 2], tf=tf)
        else:
            ffn_p = dict(wg=moe_w_gate[l // 2], wu=moe_w_up[l // 2], wd=moe_w_down[l // 2],
                         w_router=moe_w_router[l // 2], b_router=moe_b_router[l // 2])
        y_p, st = _layer(l, y_p, seq, 0, None, None, None, lp, sp, ffn_p)
        p_st.append(st)
        past = dict(page_table=page_table, page_base=l * n_pool, cmp_t=cmp_t, sel_t=sel_t, moba_t=moba_t,
                    win_t=win_t, win_base=l * bs, win_rows=cache_nsa_win[l].reshape(bs, -1, rw_a))
        y_s, st = _layer(l, y_s, dec_seq, past_len, past, state_s5_re[l].reshape(bs, S5_LANES),
                         state_s5_im[l].reshape(bs, S5_LANES), lp, sp, ffn_p)
        s_st.append(st)

    def stk(states, j, tail):
        return jnp.stack([st[j].reshape(st[j].shape[:2] + tail) if tail else st[j] for st in states], axis=0)

    row_a = (2, NSA_KV_HEADS, HEAD_DIM)
    row_b = (2, MOBA_KV_HEADS, HEAD_DIM)

    def s5_state(states, j, b):
        return jnp.stack([st[j].reshape(b, S5_GROUPS, S5_STATE) for st in states], axis=0)

    return (y_p, y_s[:, :dec_seq],
            stk(p_st, 0, row_a), stk(p_st, 1, row_a), stk(p_st, 2, row_a), stk(p_st, 3, row_b),
            s5_state(p_st, 4, bp), s5_state(p_st, 5, bp),
            stk(s_st, 0, row_a), stk(s_st, 1, row_a), stk(s_st, 2, row_a), stk(s_st, 3, row_b),
            s5_state(s_st, 4, bs), s5_state(s_st, 5, bs))
```

```python
import functools
import math

import numpy as np
import jax
import jax.numpy as jnp
from jax import lax
from jax.experimental import pallas as pl
from jax.experimental.pallas import tpu as pltpu

F32 = jnp.float32
BF16 = jnp.bfloat16

D_MODEL = 1024
HEAD_DIM = 64
PAGE_SIZE = 128
NSA_HEADS = 8
NSA_KV_HEADS = 2
NSA_CMP_LEN = 32
NSA_CMP_STRIDE = 16
NSA_PHI_HIDDEN = 128
NSA_SEL_BLOCK = 64
NSA_SEL_TOPK = 16
NSA_N_LOCAL = 2
NSA_WINDOW = 512
MOBA_HEADS = 8
MOBA_KV_HEADS = 4
MOBA_BLOCK = 256
MOBA_TOPK = 3
S5_GROUPS = 32
S5_GROUP_CH = 16
S5_WIDTH = S5_GROUPS * S5_GROUP_CH
S5_STATE = 64
S5_LANES = S5_GROUPS * S5_STATE
N_BRANCH = 3
N_EXPERTS = 8
MOE_TOP_K = 2
LN_EPS = 1e-5
DEPTH = 2
DN_ALPHA = (2.0 * DEPTH) ** 0.25

NSA_Q_W = NSA_HEADS * HEAD_DIM
NSA_KV_W = NSA_KV_HEADS * HEAD_DIM
MOBA_Q_W = MOBA_HEADS * HEAD_DIM
MOBA_KV_W = MOBA_KV_HEADS * HEAD_DIM
IN_SPLITS = (NSA_Q_W, 6 * NSA_KV_W, 3 * NSA_HEADS, MOBA_Q_W, 2 * MOBA_KV_W, S5_WIDTH, N_BRANCH * D_MODEL)
IN_OFFS = tuple(int(v) for v in np.cumsum((0,) + IN_SPLITS))

LANES = 128
SUBLANES = 8
VMEM_LIMIT = 56 * 1024 * 1024

PC_QB = 0
PC_QA = PC_QB + MOBA_HEADS * MOBA_KV_W
PC_GM = PC_QA + NSA_HEADS * NSA_KV_W
PC_GA = PC_GM + N_BRANCH * D_MODEL
PC_U = PC_GA + 3 * NSA_Q_W
PC_KVB = PC_U + S5_WIDTH
PC_KVA = PC_KVB + 2 * MOBA_KV_W
PC_END = PC_KVA + 6 * NSA_KV_W

NEG = -1e30
SLOPES_A = tuple(float(2.0 ** (-8.0 * (i + 1) / NSA_HEADS)) for i in range(NSA_HEADS))
SLOPES_B = tuple(float(2.0 ** (-8.0 * (i + 1) / MOBA_HEADS)) for i in range(MOBA_HEADS))


def _cparams(*sem):
    return pltpu.CompilerParams(dimension_semantics=sem, vmem_limit_bytes=VMEM_LIMIT)


def _round_up(n, m):
    return -(-n // m) * m


def _bf(x):
    return x.astype(BF16)


def _dot(a, b):
    return jnp.dot(_bf(a), _bf(b), preferred_element_type=F32)


def _dot_nt(a, b):
    return lax.dot_general(_bf(a), _bf(b), (((1,), (1,)), ((), ())), preferred_element_type=F32)


def _split(a):
    hi = a.astype(BF16)
    lo = (a - hi.astype(F32)).astype(BF16)
    return hi, lo


def _dot3(a, b):
    ah, al = _split(a)
    bh, bl = _split(b)
    d = functools.partial(jnp.dot, preferred_element_type=F32)
    return d(ah, bh) + (d(ah, bl) + d(al, bh))


def _dot3_nt(a, b):
    ah, al = _split(a)
    bh, bl = _split(b)
    d = functools.partial(lax.dot_general, dimension_numbers=(((1,), (1,)), ((), ())),
                          preferred_element_type=F32)
    return d(ah, bh) + (d(ah, bl) + d(al, bh))


def _dot2_exact_rhs(a, b_bf16):
    ah, al = _split(a)
    d = functools.partial(jnp.dot, preferred_element_type=F32)
    return d(ah, b_bf16) + d(al, b_bf16)


def _sigmoid(x):
    return 1.0 / (1.0 + jnp.exp(-x))


def _gelu(x):
    c = math.sqrt(2.0 / math.pi)
    return x * (0.5 * (1.0 + jnp.tanh(c * (x + 0.044715 * (x * x * x)))))


def _layer_norm(v, g, b):
    mu = jnp.mean(v, axis=-1, keepdims=True)
    vc = v - mu
    var = jnp.mean(vc * vc, axis=-1, keepdims=True)
    return vc * lax.rsqrt(var + LN_EPS) * g + b


def _mm_kernel(x_ref, w_ref, o_ref):
    o_ref[...] = jnp.dot(_bf(x_ref[...]), w_ref[...], preferred_element_type=F32)


def _matmul(x, w, tm, tn):
    m, k = x.shape
    n = w.shape[1]
    return pl.pallas_call(
        _mm_kernel,
        grid=(n // tn, m // tm),
        in_specs=[pl.BlockSpec((tm, k), lambda j, i: (i, 0)),
                  pl.BlockSpec((k, tn), lambda j, i: (0, j))],
        out_specs=pl.BlockSpec((tm, tn), lambda j, i: (i, j)),
        out_shape=jax.ShapeDtypeStruct((m, n), F32),
        compiler_params=_cparams("parallel", "arbitrary"),
        name="proj",
    )(x, w)


def _proj_weight(w_in):
    scale = HEAD_DIM ** -0.5
    qa = w_in[:, IN_OFFS[0]:IN_OFFS[1]].reshape(D_MODEL, NSA_HEADS, 1, HEAD_DIM) * scale
    oh_a = np.zeros((NSA_HEADS, NSA_KV_HEADS, 1), np.float32)
    for h in range(NSA_HEADS):
        oh_a[h, h // (NSA_HEADS // NSA_KV_HEADS)] = 1.0
    qa = (qa * oh_a[None]).reshape(D_MODEL, NSA_HEADS * NSA_KV_W)
    qb = w_in[:, IN_OFFS[3]:IN_OFFS[4]].reshape(D_MODEL, MOBA_HEADS, 1, HEAD_DIM) * scale
    oh_b = np.zeros((MOBA_HEADS, MOBA_KV_HEADS, 1), np.float32)
    for h in range(MOBA_HEADS):
        oh_b[h, h // (MOBA_HEADS // MOBA_KV_HEADS)] = 1.0
    qb = (qb * oh_b[None]).reshape(D_MODEL, MOBA_HEADS * MOBA_KV_W)
    ga = w_in[:, IN_OFFS[2]:IN_OFFS[3]].reshape(D_MODEL, NSA_HEADS, 3).transpose(0, 2, 1)
    ga = jnp.broadcast_to(ga[..., None], (D_MODEL, 3, NSA_HEADS, HEAD_DIM)).reshape(D_MODEL, 3 * NSA_Q_W)
    gm = w_in[:, IN_OFFS[6]:IN_OFFS[7]]
    u = w_in[:, IN_OFFS[5]:IN_OFFS[6]]
    kvb = w_in[:, IN_OFFS[4]:IN_OFFS[5]]
    kva = w_in[:, IN_OFFS[1]:IN_OFFS[2]]
    return jnp.concatenate([qb, qa, gm, ga, u, kvb, kva], axis=1).astype(BF16)


def _cmpab_kernel(x_ref, w_ref, o_ref):
    row_w = 2 * NSA_KV_W
    x = x_ref[...]
    for kv in range(2):
        xk = jnp.concatenate(
            [x[:, r * row_w + kv * NSA_KV_W: r * row_w + (kv + 1) * NSA_KV_W] for r in range(NSA_CMP_STRIDE)],
            axis=1)
        o_ref[:, kv * 4 * NSA_PHI_HIDDEN:(kv + 1) * 4 * NSA_PHI_HIDDEN] = _dot(xk, w_ref[kv])


def _cmp_weights(phi1_k, phi1_v):
    eye = np.eye(NSA_KV_HEADS, dtype=np.float32)
    out = []
    for phi in (phi1_k, phi1_v):
        halves = []
        for half in range(NSA_CMP_LEN // NSA_CMP_STRIDE):
            w = phi[half * NSA_CMP_STRIDE * HEAD_DIM:(half + 1) * NSA_CMP_STRIDE * HEAD_DIM]
            w = w.reshape(NSA_CMP_STRIDE, 1, HEAD_DIM, 1, NSA_PHI_HIDDEN)
            w = w * eye[None, :, None, :, None]
            halves.append(w.reshape(NSA_CMP_STRIDE * NSA_KV_W, NSA_KV_HEADS * NSA_PHI_HIDDEN))
        out.append(jnp.concatenate(halves, axis=1))
    return jnp.stack(out).astype(BF16)


def _cmp_ab_linear(x, w):
    mc = x.shape[0]
    tmc = min(mc, 256)
    n_out = 8 * NSA_PHI_HIDDEN
    return pl.pallas_call(
        _cmpab_kernel,
        grid=(mc // tmc,),
        in_specs=[pl.BlockSpec((tmc, x.shape[1]), lambda i: (i, 0)),
                  pl.BlockSpec(w.shape, lambda i: (0, 0, 0))],
        out_specs=pl.BlockSpec((tmc, n_out), lambda i: (i, 0)),
        out_shape=jax.ShapeDtypeStruct((mc, n_out), F32),
        compiler_params=_cparams("parallel"),
        name="cmp_ab",
    )(x, w)


CMP_PAGES_PER_STEP = 16


def _cmpab_t_kernel(*refs, npp):
    page_refs, w_ref, o_ref, pk_sc, pv_sc = refs[1:1 + npp], refs[1 + npp], refs[2 + npp], refs[3 + npp], refs[4 + npp]
    w = NSA_KV_W
    p = PAGE_SIZE
    eye = jnp.where(lax.broadcasted_iota(jnp.int32, (p, p), 0) == lax.broadcasted_iota(jnp.int32, (p, p), 1),
                    1.0, 0.0).astype(BF16)
    for k in range(npp):
        page_t = page_refs[k][0]
        pk_sc[k * p:(k + 1) * p, :] = _dot_nt(eye, page_t[:w])
        pv_sc[k * p:(k + 1) * p, :] = _dot_nt(eye, page_t[w:])
    m = npp * (p // NSA_CMP_STRIDE)
    n_half = 4 * NSA_PHI_HIDDEN
    for kv, sc in ((0, pk_sc), (1, pv_sc)):
        acc = None
        for r in range(NSA_CMP_STRIDE):
            part = _dot(sc[pl.ds(r, m, stride=NSA_CMP_STRIDE), :], w_ref[kv, r])
            acc = part if acc is None else acc + part
        o_ref[:, kv * n_half:(kv + 1) * n_half] = acc


def _cmp_ab_paged(pool_t, base, page_table, w):
    b, n_pages = page_table.shape
    cpp = PAGE_SIZE // NSA_CMP_STRIDE
    npp = math.gcd(CMP_PAGES_PER_STEP, n_pages)
    steps = n_pages // npp
    n_out = 8 * NSA_PHI_HIDDEN
    w4 = w.reshape(2, NSA_CMP_STRIDE, NSA_KV_W, w.shape[-1])

    def x_map(k):
        return lambda bi, s, pt: (base + pt[bi * n_pages + s * npp + k], 0, 0)

    grid_spec = pltpu.PrefetchScalarGridSpec(
        num_scalar_prefetch=1,
        grid=(b, steps),
        in_specs=[pl.BlockSpec((1,) + pool_t.shape[1:], x_map(k)) for k in range(npp)]
        + [pl.BlockSpec(w4.shape, lambda bi, s, pt: (0, 0, 0, 0))],
        out_specs=pl.BlockSpec((npp * cpp, n_out), lambda bi, s, pt: (bi * steps + s, 0)),
        scratch_shapes=[pltpu.VMEM((npp * PAGE_SIZE, NSA_KV_W), F32), pltpu.VMEM((npp * PAGE_SIZE, NSA_KV_W), F32)],
    )
    return pl.pallas_call(
        functools.partial(_cmpab_t_kernel, npp=npp),
        grid_spec=grid_spec,
        out_shape=jax.ShapeDtypeStruct((b * n_pages * cpp, n_out), F32),
        compiler_params=_cparams("parallel", "arbitrary"),
        name="cmp_ab_paged",
    )(page_table.reshape(-1), *([pool_t] * npp), w4)


def _cmpfin_kernel(ab_ref, pe_ref, w1_ref, w2_ref, o_ref):
    mc = ab_ref.shape[0]
    hid2 = NSA_KV_HEADS * NSA_PHI_HIDDEN
    for kv in range(2):
        c = _dot(pe_ref[kv], w1_ref[kv])[0:1]
        c = jnp.concatenate([c] * NSA_KV_HEADS, axis=1)
        a = ab_ref[:, kv * 2 * hid2: kv * 2 * hid2 + hid2]
        bnext = pltpu.roll(ab_ref[:, kv * 2 * hid2 + hid2:(kv + 1) * 2 * hid2], mc - 1, 0)
        hpre = a + bnext + c
        o_ref[:, kv * NSA_KV_W:(kv + 1) * NSA_KV_W] = _dot(_gelu(hpre), w2_ref[kv])


def _cmp_finish(ab, nb, pe_k, pe_v, phi1_k, phi1_v, phi2_k, phi2_v):
    mc = ab.shape[0] // nb
    flat = NSA_CMP_LEN * HEAD_DIM
    pe = jnp.stack([jnp.broadcast_to(p.reshape(1, flat), (SUBLANES, flat)) for p in (pe_k, pe_v)])
    w1 = jnp.stack([phi1_k, phi1_v]).astype(BF16)
    eye = np.eye(NSA_KV_HEADS, dtype=np.float32)
    w2 = jnp.stack([(p[None, :, None, :] * eye[:, None, :, None]).reshape(NSA_KV_HEADS * NSA_PHI_HIDDEN, NSA_KV_W)
                    for p in (phi2_k, phi2_v)]).astype(BF16)
    out = pl.pallas_call(
        _cmpfin_kernel,
        grid=(nb,),
        in_specs=[pl.BlockSpec((mc, ab.shape[1]), lambda i: (i, 0)),
                  pl.BlockSpec(pe.shape, lambda i: (0, 0, 0)),
                  pl.BlockSpec(w1.shape, lambda i: (0, 0, 0)),
                  pl.BlockSpec(w2.shape, lambda i: (0, 0, 0))],
        out_specs=pl.BlockSpec((mc, 2 * NSA_KV_W), lambda i: (i, 0)),
        out_shape=jax.ShapeDtypeStruct((nb * mc, 2 * NSA_KV_W), F32),
        compiler_params=_cparams("parallel"),
        name="cmp_finish",
    )(ab, pe, w1, w2)
    return out.reshape(nb, mc, 2 * NSA_KV_W)


def _stack_heads(q, n_heads, w):
    return jnp.concatenate([_bf(q[:, h * w:(h + 1) * w]) for h in range(n_heads)], axis=0)


def _place_heads(o_heads, n_heads, n_groups, o_ref):
    hpg = n_heads // n_groups
    tq = o_heads[0].shape[0]
    lane = lax.broadcasted_iota(jnp.int32, (tq, LANES), 1)
    for pair in range(n_heads // 2):
        pieces = []
        for h in (2 * pair, 2 * pair + 1):
            src = (h // hpg) * HEAD_DIM
            piece = o_heads[h][:, (src // LANES) * LANES:(src // LANES + 1) * LANES]
            if src % LANES != (h % 2) * HEAD_DIM:
                piece = pltpu.roll(piece, HEAD_DIM, 1)
            pieces.append(piece)
        o_ref[0, :, pair * LANES:(pair + 1) * LANES] = jnp.where(lane < HEAD_DIM, pieces[0], pieces[1])


def _top_select(score, lanef, n_pick):
    sel = jnp.zeros(score.shape, F32)
    for _ in range(n_pick):
        m = jnp.max(score, axis=-1, keepdims=True)
        idx = jnp.min(jnp.where(score == m, lanef, 1e9), axis=-1, keepdims=True)
        pick = lanef == idx
        sel = jnp.where(pick & (m > -jnp.inf), 1.0, sel)
        score = jnp.where(pick, -jnp.inf, score)
    return sel


def _cmpattn_kernel(q_ref, kc_ref, ov_ref, o_ref, sel_ref, act_ref, *, tq, qpos0, n_cmp, n_sel_blk, act_blocks):
    w = NSA_KV_W
    hpg = NSA_HEADS // NSA_KV_HEADS
    ncp = kc_ref.shape[1]
    nbp = ov_ref.shape[1]
    qbase = qpos0 + pl.program_id(1) * tq
    qs = _stack_heads(q_ref[0], NSA_HEADS, w)
    kc = kc_ref[0]
    s_all = _dot_nt(qs, kc[:, :w])
    c_idx = lax.broadcasted_iota(jnp.int32, (1, ncp), 1)
    row = lax.broadcasted_iota(jnp.int32, (tq, 1), 0)
    cend_rel = c_idx * NSA_CMP_STRIDE + (NSA_CMP_LEN - 1) - qbase
    valid = (row - cend_rel >= 0) & (c_idx < n_cmp)
    colf = cend_rel.astype(F32)
    ps, psums = [], []
    for h in range(NSA_HEADS):
        s = jnp.where(valid, s_all[h * tq:(h + 1) * tq] + SLOPES_A[h] * colf, NEG)
        m = jnp.max(s, axis=-1, keepdims=True)
        e = jnp.where(valid, jnp.exp(s - m), 0.0)
        p = e / jnp.maximum(jnp.sum(e, axis=-1, keepdims=True), 1e-30)
        ps.append(_bf(p))
        if h % hpg == 0:
            psums.append(p)
        else:
            psums[-1] = psums[-1] + p
    o_all = jnp.dot(jnp.concatenate(ps, axis=0), _bf(kc[:, w:]), preferred_element_type=F32)
    _place_heads([o_all[h * tq:(h + 1) * tq] for h in range(NSA_HEADS)], NSA_HEADS, NSA_KV_HEADS, o_ref)

    lane = lax.broadcasted_iota(jnp.int32, (tq, nbp), 1)
    lanef = lane.astype(F32)
    cur = (qbase + row) // NSA_SEL_BLOCK
    forced = (lane == 0) | ((lane <= cur) & (lane > cur - NSA_N_LOCAL))
    validb = (lane <= cur) & (lane < n_sel_blk)
    n_top = min(NSA_SEL_TOPK, n_sel_blk)
    any_blk = None
    for g in range(NSA_KV_HEADS):
        imp = _dot2_exact_rhs(psums[g], ov_ref[...])
        score = jnp.where(forced, jnp.inf, jnp.where(validb, imp, -jnp.inf))
        sel = _top_select(score, lanef, n_top)
        sel_ref[0, g] = sel
        col_any = jnp.max(sel, axis=0, keepdims=True)
        any_blk = col_any if any_blk is None else jnp.maximum(any_blk, col_any)
    pool = jnp.where((lax.broadcasted_iota(jnp.int32, (nbp, LANES), 0) // act_blocks)
                     == lax.broadcasted_iota(jnp.int32, (nbp, LANES), 1), 1.0, 0.0).astype(BF16)
    act_ref[0, 0] = jnp.dot(_bf(jnp.broadcast_to(any_blk, (SUBLANES, nbp))), pool, preferred_element_type=F32)


def _cmp_attention(proj3, kc, qpos0, t_real, tq, act_blocks=SUBLANES):
    b, tp, _ = proj3.shape
    ncp = kc.shape[1]
    seq = qpos0 + t_real
    n_cmp = seq // NSA_CMP_STRIDE - NSA_CMP_LEN // NSA_CMP_STRIDE + 1
    n_sel_blk = -(-seq // NSA_SEL_BLOCK)
    nbp = _round_up(n_sel_blk, LANES)
    cs = np.arange(ncp)[:, None] * NSA_CMP_STRIDE
    ss = np.arange(nbp)[None, :] * NSA_SEL_BLOCK
    ov = ((cs + NSA_CMP_LEN > ss) & (cs < ss + NSA_SEL_BLOCK)
          & (np.arange(ncp)[:, None] < n_cmp) & (np.arange(nbp)[None, :] < n_sel_blk))
    ov = jnp.asarray(ov.astype(np.float32), BF16)
    qw = NSA_HEADS * NSA_KV_W
    return pl.pallas_call(
        functools.partial(_cmpattn_kernel, tq=tq, qpos0=qpos0, n_cmp=n_cmp, n_sel_blk=n_sel_blk,
                          act_blocks=act_blocks),
        grid=(b, tp // tq),
        in_specs=[pl.BlockSpec((1, tq, qw), lambda bi, i: (bi, i, PC_QA // qw)),
                  pl.BlockSpec((1, ncp, 2 * NSA_KV_W), lambda bi, i: (bi, 0, 0)),
                  pl.BlockSpec(ov.shape, lambda bi, i: (0, 0))],
        out_specs=[pl.BlockSpec((1, tq, NSA_Q_W), lambda bi, i: (bi, i, 0)),
                   pl.BlockSpec((1, NSA_KV_HEADS, tq, nbp), lambda bi, i: (bi, 0, i, 0)),
                   pl.BlockSpec((1, 1, SUBLANES, LANES), lambda bi, i: (bi, i, 0, 0))],
        out_shape=[jax.ShapeDtypeStruct((b, tp, NSA_Q_W), F32),
                   jax.ShapeDtypeStruct((b, NSA_KV_HEADS, tp, nbp), F32),
                   jax.ShapeDtypeStruct((b, tp // tq, SUBLANES, LANES), F32)],
        compiler_params=_cparams("parallel", "arbitrary"),
        name="cmp_attn",
    )(proj3, kc, ov)


def _kmean_kernel(*refs, n_x):
    x_refs, o_ref = refs[:n_x], refs[n_x]
    outs = [jnp.sum(r[0], axis=0, keepdims=True) * (1.0 / MOBA_BLOCK) for r in x_refs]
    o_ref[0, 0] = outs[0] if len(outs) == 1 else jnp.concatenate(outs, axis=0)


KMEAN_BLOCKS_PER_STEP = 16


def _moba_kmean(rows3, page_table, base=0):
    w = MOBA_KV_W
    if page_table is None:
        b, t, _ = rows3.shape
        n_blk = t // MOBA_BLOCK
        bps = math.gcd(KMEAN_BLOCKS_PER_STEP, n_blk)
        steps = n_blk // bps
        out = pl.pallas_call(
            functools.partial(_kmean_kernel, n_x=bps),
            grid=(b, steps),
            in_specs=[pl.BlockSpec((1, MOBA_BLOCK, w), (lambda bi, s, k=k: (bi, s * bps + k, PC_KVB // w)))
                      for k in range(bps)],
            out_specs=pl.BlockSpec((1, 1, bps, w), lambda bi, s: (bi, s, 0, 0)),
            out_shape=jax.ShapeDtypeStruct((b, steps, bps, w), F32),
            compiler_params=_cparams("parallel", "arbitrary"),
            name="kmean",
        )(*([rows3] * bps))
        return out.reshape(b, n_blk, w)
    b, n_pages = page_table.shape
    ppb = MOBA_BLOCK // PAGE_SIZE
    n_blk = n_pages // ppb
    bps = math.gcd(KMEAN_BLOCKS_PER_STEP, n_blk)
    steps = n_blk // bps
    npp = bps * ppb

    def x_map(k):
        return lambda bi, s, pt: (base + pt[bi * n_pages + s * npp + k], 0, 0)

    grid_spec = pltpu.PrefetchScalarGridSpec(
        num_scalar_prefetch=1,
        grid=(b, steps),
        in_specs=[pl.BlockSpec((1, w, PAGE_SIZE), x_map(k)) for k in range(npp)],
        out_specs=pl.BlockSpec((1, 1, bps, w), lambda bi, s, pt: (bi, s, 0, 0)),
    )
    out = pl.pallas_call(
        functools.partial(_kmean_t_kernel, npp=npp),
        grid_spec=grid_spec,
        out_shape=jax.ShapeDtypeStruct((b, steps, bps, w), F32),
        compiler_params=_cparams("parallel", "arbitrary"),
        name="kmean_paged",
    )(page_table.reshape(-1), *([rows3] * npp))
    return out.reshape(b, n_blk, w)


def _kmean_t_kernel(*refs, npp):
    page_refs, o_ref = refs[1:1 + npp], refs[1 + npp]
    ppb = MOBA_BLOCK // PAGE_SIZE
    ones = jnp.ones((SUBLANES, PAGE_SIZE), BF16)
    d = functools.partial(lax.dot_general, dimension_numbers=(((1,), (1,)), ((), ())), preferred_element_type=F32)
    outs = []
    for i in range(npp // ppb):
        acc = None
        for k in range(ppb):
            hi, lo = _split(page_refs[i * ppb + k][0])
            part = d(ones, hi) + d(ones, lo)
            acc = part if acc is None else acc + part
        outs.append(acc[0:1] * (1.0 / MOBA_BLOCK))
    o_ref[0, 0] = outs[0] if len(outs) == 1 else jnp.concatenate(outs, axis=0)


def _mobagate_kernel(q_ref, km_ref, m_ref, *, tq, qpos0, k_top, as_bias):
    w = MOBA_KV_W
    nbp = km_ref.shape[1]
    qbase = qpos0 + pl.program_id(1) * tq
    q = q_ref[0]
    qs = jnp.concatenate([q[:, h * w:(h + 1) * w] for h in range(MOBA_HEADS)], axis=0)
    gs = _dot3_nt(qs, km_ref[0])
    lane = lax.broadcasted_iota(jnp.int32, (tq, nbp), 1)
    lanef = lane.astype(F32)
    own = (qbase + lax.broadcasted_iota(jnp.int32, (tq, 1), 0)) // MOBA_BLOCK
    for h in range(MOBA_HEADS):
        score = jnp.where(lane < own, gs[h * tq:(h + 1) * tq], -jnp.inf)
        sel = jnp.where(lane == own, 1.0, _top_select(score, lanef, k_top))
        m_ref[0, h] = jnp.where(sel > 0.5, 0.0, NEG) if as_bias else sel


def _moba_gate(proj3, kmean, qpos0, t_real, tq, as_bias):
    b, tp, _ = proj3.shape
    n_mb = -(-(qpos0 + t_real) // MOBA_BLOCK)
    nbp = _round_up(n_mb, LANES)
    km = jnp.pad(kmean, ((0, 0), (0, nbp - kmean.shape[1]), (0, 0)))
    qw = MOBA_HEADS * MOBA_KV_W
    return pl.pallas_call(
        functools.partial(_mobagate_kernel, tq=tq, qpos0=qpos0, k_top=min(MOBA_TOPK, n_mb), as_bias=as_bias),
        grid=(b, tp // tq),
        in_specs=[pl.BlockSpec((1, tq, qw), lambda bi, i: (bi, i, PC_QB // qw)),
                  pl.BlockSpec((1, nbp, MOBA_KV_W), lambda bi, i: (bi, 0, 0))],
        out_specs=pl.BlockSpec((1, MOBA_HEADS, tq, nbp), lambda bi, i: (bi, 0, i, 0)),
        out_shape=jax.ShapeDtypeStruct((b, MOBA_HEADS, tp, nbp), F32),
        compiler_params=_cparams("parallel", "arbitrary"),
        name="moba_gate",
    )(proj3, km)


def _attend_tile(qs, k, v, mask, kbase, qbase, m_sc, l_sc, acc_sc, alpha_sc, *, n_heads, n_groups, n_mask, tq,
                 blk_shift, window, slopes, rowbias, causal=True, blk=None):
    tk = k.shape[0]
    s_all = _dot_nt(qs, k)
    delta = qbase - kbase
    col1 = lax.broadcasted_iota(jnp.int32, (1, tk), 1)
    colf = (col1 - delta).astype(F32)
    ok = None
    if causal or window is not None:
        dd = lax.broadcasted_iota(jnp.int32, (tq, tk), 0) - lax.broadcasted_iota(jnp.int32, (tq, tk), 1)
        ok = dd >= -delta
        if window is not None:
            ok = ok & (dd < window - delta)
    expand_mask = n_mask and not rowbias
    if expand_mask:
        nbp = mask.shape[-1]
        blk_col = (kbase + col1) >> blk_shift
        expand = jnp.where(lax.broadcasted_iota(jnp.int32, (nbp, tk), 0) == blk_col, 1.0, 0.0).astype(BF16)
        mexp = jnp.dot(_bf(mask.reshape(n_mask * tq, nbp)), expand, preferred_element_type=F32)
    if rowbias:
        nbp = mask.shape[-1]
        pick = jnp.where(lax.broadcasted_iota(jnp.int32, (nbp, LANES), 0) == blk, 1.0, 0.0).astype(BF16)
        rowb = jnp.dot(_bf(mask.reshape(n_heads * tq, nbp)), pick, preferred_element_type=F32)
    hpm = n_heads // n_mask if n_mask else n_heads
    ps = []
    valid = ok
    for h in range(n_heads):
        rows = slice(h * tq, (h + 1) * tq)
        s = s_all[rows] + slopes[h] * colf
        if rowbias:
            s = s + _lane_rep(rowb[rows], tk)
        elif expand_mask and h % hpm == 0:
            g = h // hpm
            valid = mexp[g * tq:(g + 1) * tq] > 0.5
            if ok is not None:
                valid = ok & valid
        if valid is not None:
            s = jnp.where(valid, s, NEG)
        m_old = m_sc[rows]
        m_new = jnp.maximum(m_old, jnp.max(s, axis=-1, keepdims=True))
        alpha_sc[rows] = jnp.exp(m_old - m_new)
        m_sc[rows] = m_new
        ps.append(_bf(jnp.exp(s - _lane_rep(m_new, tk))))
    p_all = jnp.concatenate(ps, axis=0)
    alpha = alpha_sc[...]
    l_sc[...] = alpha * l_sc[...] + jnp.dot(p_all, jnp.ones((tk, LANES), BF16), preferred_element_type=F32)
    acc_sc[...] = (_lane_rep(alpha, acc_sc.shape[1]) * acc_sc[...]
                   + jnp.dot(p_all, _bf(v), preferred_element_type=F32))


def _lane_rep(x, n):
    return x if n == LANES else jnp.concatenate([x] * (n // LANES), axis=1)


def _attn_finish(m_sc, l_sc, acc_sc, o_ref, n_heads, n_groups, tq):
    w = acc_sc.shape[1]
    o = jnp.where(_lane_rep(m_sc[...], w) > 0.5 * NEG, acc_sc[...] / _lane_rep(l_sc[...], w), 0.0)
    _place_heads([o[h * tq:(h + 1) * tq] for h in range(n_heads)], n_heads, n_groups, o_ref)


def _attn_init(q_ref, q_sc, m_sc, l_sc, acc_sc, n_heads, w):
    q_sc[...] = _stack_heads(q_ref[0], n_heads, w)
    m_sc[...] = jnp.full(m_sc.shape, NEG, F32)
    l_sc[...] = jnp.zeros(l_sc.shape, F32)
    acc_sc[...] = jnp.zeros(acc_sc.shape, F32)


def _kv_tile_range(qi, *, tq, tk, qpos0, kpos0, window, n_kt):
    qlo = qpos0 + qi * tq
    last = jnp.minimum((qlo + tq - 1 - kpos0) // tk, n_kt - 1)
    if window is None:
        first = 0
    else:
        first = jnp.maximum(qlo - window + 1 - kpos0, 0) // tk
    return first, last


def _flash_kernel(*refs, cfg, rng):
    if cfg["n_mask"]:
        q_ref, kv_ref, mask_ref, o_ref, q_sc, m_sc, l_sc, acc_sc, alpha_sc = refs
    else:
        q_ref, kv_ref, o_ref, q_sc, m_sc, l_sc, acc_sc, alpha_sc = refs
        mask_ref = None
    w, tq, tk = cfg["n_groups"] * HEAD_DIM, rng["tq"], rng["tk"]
    qi, j = pl.program_id(1), pl.program_id(2)
    first, last = _kv_tile_range(qi, **rng)

    @pl.when(j == 0)
    def _():
        _attn_init(q_ref, q_sc, m_sc, l_sc, acc_sc, cfg["n_heads"], w)

    jt = first + j
    kbase = rng["kpos0"] + jt * tk
    qbase = rng["qpos0"] + qi * tq

    def tile(causal):
        kv = kv_ref[0]
        _attend_tile(q_sc[...], kv[:, :w], kv[:, w:], None if mask_ref is None else mask_ref[0], kbase, qbase,
                     m_sc, l_sc, acc_sc, alpha_sc, tq=tq, causal=causal, blk=kbase >> cfg["blk_shift"], **cfg)

    if cfg["window"] is None:
        fully_past = kbase + tk <= qbase
        pl.when((jt <= last) & fully_past)(lambda: tile(False))
        pl.when((jt <= last) & jnp.logical_not(fully_past))(lambda: tile(True))
    else:
        pl.when(jt <= last)(lambda: tile(True))

    @pl.when(j == pl.num_programs(2) - 1)
    def _():
        _attn_finish(m_sc, l_sc, acc_sc, o_ref, cfg["n_heads"], cfg["n_groups"], tq)


def _flash_linear(q_arr, q_col, kv_arr, kv_col, mask, cfg, *, tq, tk, qpos0, kpos0, n_steps):
    b, tp, _ = q_arr.shape
    w = cfg["n_groups"] * HEAD_DIM
    n_heads = cfg["n_heads"]
    n_kt = kv_arr.shape[1] // tk
    rng = dict(tq=tq, tk=tk, qpos0=qpos0, kpos0=kpos0, window=cfg["window"], n_kt=n_kt)

    def kv_map(bi, i, j):
        first, last = _kv_tile_range(i, **rng)
        return (bi, jnp.minimum(first + j, last), kv_col)

    in_specs = [pl.BlockSpec((1, tq, n_heads * w), lambda bi, i, j: (bi, i, q_col)),
                pl.BlockSpec((1, tk, 2 * w), kv_map)]
    args = [q_arr, kv_arr]
    if cfg["n_mask"]:
        in_specs.append(pl.BlockSpec((1, cfg["n_mask"], tq, mask.shape[-1]), lambda bi, i, j: (bi, 0, i, 0)))
        args.append(mask)
    return pl.pallas_call(
        functools.partial(_flash_kernel, cfg=cfg, rng=rng),
        grid=(b, tp // tq, n_steps),
        in_specs=in_specs,
        out_specs=pl.BlockSpec((1, tq, n_heads * HEAD_DIM), lambda bi, i, j: (bi, i, 0)),
        out_shape=jax.ShapeDtypeStruct((b, tp, n_heads * HEAD_DIM), F32),
        scratch_shapes=[pltpu.VMEM((n_heads * tq, w), BF16), pltpu.VMEM((n_heads * tq, LANES), F32),
                        pltpu.VMEM((n_heads * tq, LANES), F32), pltpu.VMEM((n_heads * tq, w), F32),
                        pltpu.VMEM((n_heads * tq, LANES), F32)],
        compiler_params=_cparams("parallel", "parallel", "arbitrary"),
        name="flash",
    )(*args)


def _worklist(act, tq, tk):
    nq, n_kt = act.shape
    i = np.arange(nq)[:, None]
    jt = np.arange(n_kt)[None, :]
    diag = np.minimum((i * tq + tq - 1) // tk, n_kt - 1)
    causal = jt <= diag
    s_max = int(causal.sum())
    need = (act | jnp.asarray(jt == diag)) & jnp.asarray(causal)
    key = jnp.where(need, jnp.asarray(i * n_kt + jt, jnp.int32), nq * n_kt).reshape(-1)
    key = jnp.sort(key)[:s_max]
    n_need = jnp.sum(need.astype(jnp.int32))
    pos = jnp.arange(s_max, dtype=jnp.int32)
    valid = pos < n_need
    key = jnp.where(valid, key, jnp.take(key, n_need - 1))
    qi, kt = key // n_kt, key % n_kt
    first = valid & ((pos == 0) | (qi != jnp.roll(qi, 1)))
    last = valid & ((pos == n_need - 1) | (qi != jnp.roll(qi, -1)))
    return jnp.concatenate([qi, kt, first.astype(jnp.int32), last.astype(jnp.int32),
                            valid.astype(jnp.int32)]).astype(jnp.int32), s_max


def _flash_list_kernel(wl_ref, q_ref, kv_ref, mask_ref, o_ref, q_sc, m_sc, l_sc, acc_sc, alpha_sc, *,
                       cfg, tq, tk, s_max):
    s = pl.program_id(0)
    w = cfg["n_groups"] * HEAD_DIM
    qbase = wl_ref[s] * tq
    kbase = wl_ref[s_max + s] * tk
    valid = wl_ref[4 * s_max + s] == 1

    @pl.when(wl_ref[2 * s_max + s] == 1)
    def _():
        _attn_init(q_ref, q_sc, m_sc, l_sc, acc_sc, cfg["n_heads"], w)

    def tile(causal):
        kv = kv_ref[0]
        _attend_tile(q_sc[...], kv[:, :w], kv[:, w:], mask_ref[0], kbase, qbase, m_sc, l_sc, acc_sc, alpha_sc,
                     tq=tq, causal=causal, **cfg)

    fully_past = kbase + tk <= qbase
    pl.when(valid & fully_past)(lambda: tile(False))
    pl.when(valid & jnp.logical_not(fully_past))(lambda: tile(True))

    @pl.when(wl_ref[3 * s_max + s] == 1)
    def _():
        _attn_finish(m_sc, l_sc, acc_sc, o_ref, cfg["n_heads"], cfg["n_groups"], tq)


def _flash_worklist(q_arr, q_col, kv_arr, kv_col, mask, act, cfg, *, tq, tk):
    b, tp, _ = q_arr.shape
    assert b == 1 and not cfg["rowbias"] and cfg["window"] is None
    w = cfg["n_groups"] * HEAD_DIM
    n_heads = cfg["n_heads"]
    wl, s_max = _worklist(act, tq, tk)
    grid_spec = pltpu.PrefetchScalarGridSpec(
        num_scalar_prefetch=1,
        grid=(s_max,),
        in_specs=[pl.BlockSpec((1, tq, n_heads * w), lambda s, wl: (0, wl[s], q_col)),
                  pl.BlockSpec((1, tk, 2 * w), lambda s, wl: (0, wl[s_max + s], kv_col)),
                  pl.BlockSpec((1, cfg["n_mask"], tq, mask.shape[-1]), lambda s, wl: (0, 0, wl[s], 0))],
        out_specs=pl.BlockSpec((1, tq, n_heads * HEAD_DIM), lambda s, wl: (0, wl[s], 0)),
        scratch_shapes=[pltpu.VMEM((n_heads * tq, w), BF16), pltpu.VMEM((n_heads * tq, LANES), F32),
                        pltpu.VMEM((n_heads * tq, LANES), F32), pltpu.VMEM((n_heads * tq, w), F32),
                        pltpu.VMEM((n_heads * tq, LANES), F32)],
    )
    return pl.pallas_call(
        functools.partial(_flash_list_kernel, cfg=cfg, tq=tq, tk=tk, s_max=s_max),
        grid_spec=grid_spec,
        out_shape=jax.ShapeDtypeStruct((b, tp, n_heads * HEAD_DIM), F32),
        compiler_params=_cparams("arbitrary"),
        name="flash_list",
    )(wl, q_arr, kv_arr, mask)


DECODE_PAGES_PER_STEP = 16


def _decode_update(qs, k_op, v_op, transposed, kbase, qpos_col, slope_col, mrow, m_sc, l_sc, acc_sc, *,
                   blk_shift, window, qpos0):
    s = jnp.dot(qs, _bf(k_op), preferred_element_type=F32) if transposed else _dot_nt(qs, k_op)
    tk = s.shape[1]
    kpos = kbase + lax.broadcasted_iota(jnp.int32, (1, tk), 1)
    d = qpos_col - kpos
    ok = d >= 0
    if window is not None:
        ok = ok & (d < window)
    if mrow is not None:
        nbp = mrow.shape[1]
        expand = jnp.where(lax.broadcasted_iota(jnp.int32, (nbp, tk), 0) == (kpos >> blk_shift), 1.0, 0.0)
        ok = ok & (jnp.dot(mrow, expand.astype(BF16), preferred_element_type=F32) > 0.5)
    s = jnp.where(ok, s + slope_col * (kpos - qpos0).astype(F32), NEG)
    m_old = m_sc[...]
    m_new = jnp.maximum(m_old, jnp.max(s, axis=-1, keepdims=True))
    alpha = jnp.exp(m_old - m_new)
    p = _bf(jnp.exp(s - m_new))
    l_sc[...] = alpha * l_sc[...] + jnp.sum(p.astype(F32), axis=-1, keepdims=True)
    m_sc[...] = m_new
    pv = _dot_nt(p, v_op) if transposed else jnp.dot(p, _bf(v_op), preferred_element_type=F32)
    acc_sc[...] = alpha * acc_sc[...] + pv


def _decode_kernel(*refs, cfg, npp, tq, qpos0, kpos0, tile_rows):
    has_mask = bool(cfg["n_mask"])
    page_refs = refs[2:2 + npp]
    q_ref, tail_ref = refs[1], refs[2 + npp]
    rest = refs[3 + npp:]
    if has_mask:
        mask_ref, o_ref, q_sc, m_sc, l_sc, acc_sc, slope_sc, mrow_sc = rest
    else:
        o_ref, q_sc, m_sc, l_sc, acc_sc, slope_sc = rest
    n_heads = cfg["n_heads"]
    w = cfg["n_groups"] * HEAD_DIM
    j = pl.program_id(1)
    n_steps = pl.num_programs(1)
    row = lax.broadcasted_iota(jnp.int32, (n_heads * tq, 1), 0)

    @pl.when(j == 0)
    def _():
        _attn_init(q_ref, q_sc, m_sc, l_sc, acc_sc, n_heads, w)
        slope = jnp.zeros((n_heads * tq, 1), F32)
        for h in range(n_heads):
            slope = jnp.where(row // tq == h, cfg["slopes"][h], slope)
        slope_sc[...] = slope
        if has_mask:
            hpm = n_heads // cfg["n_mask"]
            mrow_sc[...] = _bf(jnp.concatenate([mask_ref[0, h // hpm] for h in range(n_heads)], axis=0))

    qpos_col = qpos0 + row % tq
    kw = dict(blk_shift=cfg["blk_shift"], window=cfg["window"], qpos0=qpos0)
    mrow = mrow_sc[...] if has_mask else None

    @pl.when(j < n_steps - 1)
    def _():
        kt = [r[0, :w, :] for r in page_refs]
        vt = [r[0, w:, :] for r in page_refs]
        kt = kt[0] if npp == 1 else jnp.concatenate(kt, axis=1)
        vt = vt[0] if npp == 1 else jnp.concatenate(vt, axis=1)
        _decode_update(q_sc[...], kt, vt, True, kpos0 + j * (npp * tile_rows), qpos_col, slope_sc[...], mrow,
                       m_sc, l_sc, acc_sc, **kw)

    @pl.when(j == n_steps - 1)
    def _():
        tail = tail_ref[0]
        _decode_update(q_sc[...], tail[:, :w], tail[:, w:], False, qpos0, qpos_col, slope_sc[...], mrow,
                       m_sc, l_sc, acc_sc, **kw)
        o = jnp.where(m_sc[...] > 0.5 * NEG, acc_sc[...] / l_sc[...], 0.0)
        _place_heads([o[h * tq:(h + 1) * tq] for h in range(n_heads)], n_heads, cfg["n_groups"], o_ref)


def _decode_attn(proj3, q_col, tail_col, tiles_t, table, base, mask, cfg, *, qpos0, kpos0):
    b, tq, _ = proj3.shape
    w = cfg["n_groups"] * HEAD_DIM
    n_heads = cfg["n_heads"]
    n_tiles = table.shape[1]
    tile_rows = tiles_t.shape[2]
    npp = math.gcd(DECODE_PAGES_PER_STEP, n_tiles)
    steps = n_tiles // npp
    r = n_heads * tq

    def tile_map(k):
        return lambda bi, j, tb: (base + tb[bi * n_tiles + jnp.minimum(j, steps - 1) * npp + k], 0, 0)

    in_specs = ([pl.BlockSpec((1, tq, n_heads * w), lambda bi, j, tb: (bi, 0, q_col))]
                + [pl.BlockSpec((1, 2 * w, tile_rows), tile_map(k)) for k in range(npp)]
                + [pl.BlockSpec((1, tq, 2 * w), lambda bi, j, tb: (bi, 0, tail_col))])
    args = [table.reshape(-1), proj3] + [tiles_t] * npp + [proj3]
    scratch = [pltpu.VMEM((r, w), BF16), pltpu.VMEM((r, 1), F32), pltpu.VMEM((r, 1), F32), pltpu.VMEM((r, w), F32),
               pltpu.VMEM((r, 1), F32)]
    if cfg["n_mask"]:
        in_specs.append(pl.BlockSpec((1, cfg["n_mask"], tq, mask.shape[-1]), lambda bi, j, tb: (bi, 0, 0, 0)))
        args.append(mask)
        scratch.append(pltpu.VMEM((r, mask.shape[-1]), BF16))
    grid_spec = pltpu.PrefetchScalarGridSpec(
        num_scalar_prefetch=1,
        grid=(b, steps + 1),
        in_specs=in_specs,
        out_specs=pl.BlockSpec((1, tq, n_heads * HEAD_DIM), lambda bi, j, tb: (bi, 0, 0)),
        scratch_shapes=scratch,
    )
    return pl.pallas_call(
        functools.partial(_decode_kernel, cfg=cfg, npp=npp, tq=tq, qpos0=qpos0, kpos0=kpos0, tile_rows=tile_rows),
        grid_spec=grid_spec,
        out_shape=jax.ShapeDtypeStruct((b, tq, n_heads * HEAD_DIM), F32),
        compiler_params=_cparams("parallel", "arbitrary"),
        name="decode_attn",
    )(*args)


def _attn_cfg(kind, rowbias=False):
    if kind == "moba":
        return dict(n_heads=MOBA_HEADS, n_groups=MOBA_KV_HEADS, n_mask=MOBA_HEADS, rowbias=rowbias,
                    blk_shift=int(math.log2(MOBA_BLOCK)), window=None, slopes=SLOPES_B)
    if kind == "sel":
        return dict(n_heads=NSA_HEADS, n_groups=NSA_KV_HEADS, n_mask=NSA_KV_HEADS, rowbias=False,
                    blk_shift=int(math.log2(NSA_SEL_BLOCK)), window=None, slopes=SLOPES_A)
    return dict(n_heads=NSA_HEADS, n_groups=NSA_KV_HEADS, n_mask=0, rowbias=False, blk_shift=0,
                window=NSA_WINDOW, slopes=SLOPES_A)


S5_CHUNK = 4 * LANES
S5_GCHUNK = S5_CHUNK // S5_STATE


def _s5_params(lp):
    lr, li = lp["s5_lambda_re"], lp["s5_lambda_im"]
    dt = jnp.exp(lp["s5_log_dt"])[:, None]
    mag = jnp.exp(lr * dt)
    a_re = mag * jnp.cos(li * dt)
    a_im = mag * jnp.sin(li * dt)
    den = lr * lr + li * li
    f_re = ((a_re - 1.0) * lr + a_im * li) / den
    f_im = (a_im * lr - (a_re - 1.0) * li) / den
    b_re, b_im = lp["s5_b_re"], lp["s5_b_im"]
    bb_re = f_re[..., None] * b_re - f_im[..., None] * b_im
    bb_im = f_re[..., None] * b_im + f_im[..., None] * b_re
    n_ch = S5_GROUPS // S5_GCHUNK
    eye = np.eye(S5_GCHUNK, dtype=np.float32)

    def in_w(bb):
        x = bb.reshape(n_ch, S5_GCHUNK, S5_STATE, S5_GROUP_CH).transpose(0, 1, 3, 2)
        x = x[:, :, :, None, :] * eye[None, :, None, :, None]
        return x.reshape(n_ch, S5_GCHUNK * S5_GROUP_CH, S5_CHUNK)

    def out_w(c):
        x = c.reshape(n_ch, S5_GCHUNK, S5_GROUP_CH, S5_STATE).transpose(0, 1, 3, 2)
        x = x[:, :, :, None, :] * eye[None, :, None, :, None]
        return x.reshape(n_ch, S5_CHUNK, S5_GCHUNK * S5_GROUP_CH).astype(BF16)

    return dict(a_re=a_re.reshape(1, S5_LANES), a_im=a_im.reshape(1, S5_LANES),
                bw_re=in_w(bb_re), bw_im=in_w(bb_im),
                cw_re=out_w(lp["s5_c_re"]), cw_im=out_w(lp["s5_c_im"]),
                d=lp["s5_d"].reshape(1, S5_WIDTH))


def _s5_out_proj(h_re, h_im, cwr_ref, cwi_ref):
    cols = []
    for c in range(S5_LANES // S5_CHUNK):
        sl = slice(c * S5_CHUNK, (c + 1) * S5_CHUNK)
        cols.append(jnp.dot(_bf(h_re[:, sl]), cwr_ref[c], preferred_element_type=F32)
                    - jnp.dot(_bf(h_im[:, sl]), cwi_ref[c], preferred_element_type=F32))
    return jnp.concatenate(cols, axis=1)


def _s5scan_kernel(u_ref, h0r_ref, h0i_ref, ar_ref, ai_ref, bwr_ref, bwi_ref, cwr_ref, cwi_ref, d_ref,
                   y_ref, hr_ref, hi_ref, bur_sc, bui_sc, hr_sc, hi_sc, *, n_par, n_j):
    @pl.when(pl.program_id(0) == 0)
    def _():
        hr_sc[...] = h0r_ref[...]
        hi_sc[...] = h0i_ref[...]

    u = u_ref[...]
    gw = S5_GCHUNK * S5_GROUP_CH
    for c in range(S5_LANES // S5_CHUNK):
        uc = u[:, c * gw:(c + 1) * gw]
        bur_sc[:, c * S5_CHUNK:(c + 1) * S5_CHUNK] = _dot3(uc, bwr_ref[c])
        bui_sc[:, c * S5_CHUNK:(c + 1) * S5_CHUNK] = _dot3(uc, bwi_ref[c])

    for c in range(S5_LANES // S5_CHUNK):
        sl = slice(c * S5_CHUNK, (c + 1) * S5_CHUNK)
        a_re = jnp.broadcast_to(ar_ref[:, sl], (n_par, S5_CHUNK))
        a_im = jnp.broadcast_to(ai_ref[:, sl], (n_par, S5_CHUNK))

        def step(jj, carry):
            h_re, h_im = carry
            rows = pl.ds(pl.multiple_of(jj * n_par, n_par), n_par)
            n_re = a_re * h_re - a_im * h_im + bur_sc[rows, sl]
            n_im = a_re * h_im + a_im * h_re + bui_sc[rows, sl]
            bur_sc[rows, sl] = n_re
            bui_sc[rows, sl] = n_im
            return n_re, n_im

        h_re, h_im = lax.fori_loop(0, n_j, step, (hr_sc[:, sl], hi_sc[:, sl]))
        hr_sc[:, sl] = h_re
        hi_sc[:, sl] = h_im

    y_ref[...] = _s5_out_proj(bur_sc[...], bui_sc[...], cwr_ref, cwi_ref) + d_ref[...] * u
    hr_ref[...] = hr_sc[...]
    hi_ref[...] = hi_sc[...]


def _s5_scan(u_rows, h0_re, h0_im, sp, n_par, jc):
    rows = u_rows.shape[0]
    n_j_total = rows // n_par
    steps = n_j_total // jc
    full = lambda a: pl.BlockSpec(a.shape, lambda i: (0,) * a.ndim)
    ins = [u_rows, h0_re, h0_im, sp["a_re"], sp["a_im"], sp["bw_re"], sp["bw_im"], sp["cw_re"], sp["cw_im"], sp["d"]]
    return pl.pallas_call(
        functools.partial(_s5scan_kernel, n_par=n_par, n_j=jc),
        grid=(steps,),
        in_specs=[pl.BlockSpec((jc * n_par, S5_WIDTH), lambda i: (i, 0))] + [full(a) for a in ins[1:]],
        out_specs=[pl.BlockSpec((jc * n_par, S5_WIDTH), lambda i: (i, 0)),
                   pl.BlockSpec((n_par, S5_LANES), lambda i: (0, 0)),
                   pl.BlockSpec((n_par, S5_LANES), lambda i: (0, 0))],
        out_shape=[jax.ShapeDtypeStruct((rows, S5_WIDTH), F32),
                   jax.ShapeDtypeStruct((n_par, S5_LANES), F32),
                   jax.ShapeDtypeStruct((n_par, S5_LANES), F32)],
        scratch_shapes=[pltpu.VMEM((jc * n_par, S5_LANES), F32), pltpu.VMEM((jc * n_par, S5_LANES), F32),
                        pltpu.VMEM((n_par, S5_LANES), F32), pltpu.VMEM((n_par, S5_LANES), F32)],
        compiler_params=_cparams("arbitrary"),
        name="s5_scan",
    )(*ins)


def _s5fix_kernel(y_ref, er_ref, ei_ref, ar_ref, ai_ref, cwr_ref, cwi_ref, o_ref, fr_ref, fi_ref,
                  cr_sc, ci_sc, pr_sc, pi_sc, xr_sc, xi_sc, *, n_seg, n_j, n_j_total):
    a_re = ar_ref[...]
    a_im = ai_ref[...]

    @pl.when(pl.program_id(0) == 0)
    def _():
        def pw_step(_, carry):
            p_re, p_im = carry
            return a_re * p_re - a_im * p_im, a_re * p_im + a_im * p_re

        al_re, al_im = lax.fori_loop(0, n_j_total - 1, pw_step, (a_re, a_im))
        c_re = jnp.zeros((1, S5_LANES), F32)
        c_im = jnp.zeros((1, S5_LANES), F32)
        for s in range(n_seg):
            cr_sc[s:s + 1, :] = c_re
            ci_sc[s:s + 1, :] = c_im
            e_re, e_im = er_ref[s:s + 1, :], ei_ref[s:s + 1, :]
            c_re, c_im = (e_re + al_re * c_re - al_im * c_im, e_im + al_re * c_im + al_im * c_re)
        fr_ref[...] = c_re
        fi_ref[...] = c_im
        pr_sc[...] = jnp.broadcast_to(a_re, pr_sc.shape)
        pi_sc[...] = jnp.broadcast_to(a_im, pi_sc.shape)

    for c in range(S5_LANES // S5_CHUNK):
        sl = slice(c * S5_CHUNK, (c + 1) * S5_CHUNK)
        ab_re = jnp.broadcast_to(a_re[:, sl], (n_seg, S5_CHUNK))
        ab_im = jnp.broadcast_to(a_im[:, sl], (n_seg, S5_CHUNK))
        c_re, c_im = cr_sc[:, sl], ci_sc[:, sl]

        def step(jj, carry):
            p_re, p_im = carry
            rows = pl.ds(pl.multiple_of(jj * n_seg, n_seg), n_seg)
            xr_sc[rows, sl] = p_re * c_re - p_im * c_im
            xi_sc[rows, sl] = p_re * c_im + p_im * c_re
            return ab_re * p_re - ab_im * p_im, ab_re * p_im + ab_im * p_re

        p_re, p_im = lax.fori_loop(0, n_j, step, (pr_sc[:, sl], pi_sc[:, sl]))
        pr_sc[:, sl] = p_re
        pi_sc[:, sl] = p_im

    o_ref[...] = y_ref[...] + _s5_out_proj(xr_sc[...], xi_sc[...], cwr_ref, cwi_ref)


def _s5_fix(y_rows, end_re, end_im, sp, n_seg, jc):
    rows = y_rows.shape[0]
    n_j_total = rows // n_seg
    steps = n_j_total // jc
    full = lambda a: pl.BlockSpec(a.shape, lambda i: (0,) * a.ndim)
    ins = [y_rows, end_re, end_im, sp["a_re"], sp["a_im"], sp["cw_re"], sp["cw_im"]]
    seg = lambda: pltpu.VMEM((n_seg, S5_LANES), F32)
    return pl.pallas_call(
        functools.partial(_s5fix_kernel, n_seg=n_seg, n_j=jc, n_j_total=n_j_total),
        grid=(steps,),
        in_specs=[pl.BlockSpec((jc * n_seg, S5_WIDTH), lambda i: (i, 0))] + [full(a) for a in ins[1:]],
        out_specs=[pl.BlockSpec((jc * n_seg, S5_WIDTH), lambda i: (i, 0)),
                   pl.BlockSpec((1, S5_LANES), lambda i: (0, 0)),
                   pl.BlockSpec((1, S5_LANES), lambda i: (0, 0))],
        out_shape=[jax.ShapeDtypeStruct((rows, S5_WIDTH), F32),
                   jax.ShapeDtypeStruct((1, S5_LANES), F32),
                   jax.ShapeDtypeStruct((1, S5_LANES), F32)],
        scratch_shapes=[seg(), seg(), seg(), seg(),
                        pltpu.VMEM((jc * n_seg, S5_LANES), F32), pltpu.VMEM((jc * n_seg, S5_LANES), F32)],
        compiler_params=_cparams("arbitrary"),
        name="s5_fix",
    )(*ins)


S5_SEGMENTS = SUBLANES


def _s5_prompt(u, sp):
    t = u.shape[0]
    n_seg = S5_SEGMENTS
    seg_len = t // n_seg
    jc = min(64, seg_len)
    u_rows = u.reshape(n_seg, seg_len, S5_WIDTH).transpose(1, 0, 2).reshape(t, S5_WIDTH)
    zero = jnp.zeros((n_seg, S5_LANES), F32)
    y_loc, end_re, end_im = _s5_scan(u_rows, zero, zero, sp, n_seg, jc)
    y_rows, f_re, f_im = _s5_fix(y_loc, end_re, end_im, sp, n_seg, jc)
    y = y_rows.reshape(seg_len, n_seg, S5_WIDTH).transpose(1, 0, 2).reshape(t, S5_WIDTH)
    return y, f_re, f_im


def _s5_sample(u3, h0_re, h0_im, sp):
    b, t, _ = u3.shape
    u_rows = u3.transpose(1, 0, 2).reshape(t * b, S5_WIDTH)
    y_rows, h_re, h_im = _s5_scan(u_rows, h0_re, h0_im, sp, b, t)
    return y_rows.reshape(t, b, S5_WIDTH).transpose(1, 0, 2), h_re, h_im


def _merge_kernel(x_ref, oc_ref, os_ref, ow_ref, om_ref, y5_ref, ga_ref, gm_ref, wglu_ref, bglu_ref,
                  wn_ref, wm_ref, w5_ref, wo_ref, g_ref, b_ref, o_ref):
    qw = NSA_Q_W
    ga = ga_ref[...]
    o_nsa = (_sigmoid(ga[:, :qw]) * oc_ref[...] + _sigmoid(ga[:, qw:2 * qw]) * os_ref[...]
             + _sigmoid(ga[:, 2 * qw:]) * ow_ref[...])
    z = _gelu(y5_ref[...])
    o_s5 = z * _sigmoid(jnp.dot(_bf(z), wglu_ref[...], preferred_element_type=F32) + bglu_ref[...])
    gm = gm_ref[...]
    d = D_MODEL
    merged = (_sigmoid(gm[:, :d]) * jnp.dot(_bf(o_nsa), wn_ref[...], preferred_element_type=F32)
              + _sigmoid(gm[:, d:2 * d]) * jnp.dot(_bf(om_ref[...]), wm_ref[...], preferred_element_type=F32)
              + _sigmoid(gm[:, 2 * d:]) * jnp.dot(_bf(o_s5), w5_ref[...], preferred_element_type=F32))
    mix = jnp.dot(_bf(merged), wo_ref[...], preferred_element_type=F32)
    o_ref[...] = _layer_norm(DN_ALPHA * x_ref[...] + mix, g_ref[...], b_ref[...])


def _merge(x2, o_c, o_s, o_w, o_m, y5, proj2, lp, tm):
    m = x2.shape[0]
    row = lambda n: pl.BlockSpec((tm, n), lambda i: (i, 0))
    full = lambda a: pl.BlockSpec(a.shape, lambda i: (0,) * a.ndim)
    ws = [lp["s5_w_glu"].astype(BF16), lp["s5_b_glu"].reshape(1, S5_WIDTH),
          lp["w_br_nsa"].astype(BF16), lp["w_br_moba"].astype(BF16), lp["w_br_s5"].astype(BF16),
          lp["w_out"].astype(BF16), lp["ln1_g"].reshape(1, D_MODEL), lp["ln1_b"].reshape(1, D_MODEL)]
    gaw, gmw = 3 * NSA_Q_W, N_BRANCH * D_MODEL
    return pl.pallas_call(
        _merge_kernel,
        grid=(m // tm,),
        in_specs=[row(D_MODEL), row(NSA_Q_W), row(NSA_Q_W), row(NSA_Q_W), row(MOBA_Q_W), row(S5_WIDTH),
                  pl.BlockSpec((tm, gaw), lambda i: (i, PC_GA // gaw)),
                  pl.BlockSpec((tm, gmw), lambda i: (i, PC_GM // gmw))] + [full(a) for a in ws],
        out_specs=row(D_MODEL),
        out_shape=jax.ShapeDtypeStruct((m, D_MODEL), F32),
        compiler_params=_cparams("parallel"),
        name="merge",
    )(x2, o_c, o_s, o_w, o_m, y5, proj2, proj2, *ws)


def _router_kernel(x_ref, w_ref, b_ref, o_ref):
    logits = _dot3(x_ref[...], w_ref[...]) + b_ref[...]
    lane = lax.broadcasted_iota(jnp.int32, logits.shape, 1)
    lanef = lane.astype(F32)
    logits = jnp.where(lane < N_EXPERTS, logits, -jnp.inf)
    sel = _top_select(logits, lanef, MOE_TOP_K)
    m = jnp.max(logits, axis=-1, keepdims=True)
    e = jnp.where(sel > 0.5, jnp.exp(logits - m), 0.0)
    o_ref[...] = e / jnp.sum(e, axis=-1, keepdims=True)


def _router(x2, w_router, b_router, tm):
    m = x2.shape[0]
    w = jnp.pad(w_router, ((0, 0), (0, LANES - N_EXPERTS)))
    bias = jnp.pad(b_router, (0, LANES - N_EXPERTS)).reshape(1, LANES)
    return pl.pallas_call(
        _router_kernel,
        grid=(m // tm,),
        in_specs=[pl.BlockSpec((tm, D_MODEL), lambda i: (i, 0)),
                  pl.BlockSpec(w.shape, lambda i: (0, 0)), pl.BlockSpec(bias.shape, lambda i: (0, 0))],
        out_specs=pl.BlockSpec((tm, LANES), lambda i: (i, 0)),
        out_shape=jax.ShapeDtypeStruct((m, LANES), F32),
        compiler_params=_cparams("parallel"),
        name="router",
    )(x2, w, bias)


def _ffn_kernel(*refs, use_comb):
    if use_comb:
        x_ref, comb_ref, wg_ref, wu_ref, wd_ref, g_ref, b_ref, o_ref, acc_sc = refs
    else:
        x_ref, wg_ref, wu_ref, wd_ref, g_ref, b_ref, o_ref, acc_sc = refs
    f = pl.program_id(1)

    @pl.when(f == 0)
    def _():
        acc_sc[...] = jnp.zeros(acc_sc.shape, F32)

    xb = _bf(x_ref[...])
    wg = wg_ref[0] if use_comb else wg_ref[...]
    wu = wu_ref[0] if use_comb else wu_ref[...]
    wd = wd_ref[0] if use_comb else wd_ref[...]
    gate = jnp.dot(xb, _bf(wg), preferred_element_type=F32)
    up = jnp.dot(xb, _bf(wu), preferred_element_type=F32)
    h = gate * _sigmoid(gate) * up
    if use_comb:
        comb = comb_ref[...]
        lane = lax.broadcasted_iota(jnp.int32, comb.shape, 1)
        h = h * jnp.sum(jnp.where(lane == f, comb, 0.0), axis=-1, keepdims=True)
    acc_sc[...] += jnp.dot(_bf(h), _bf(wd), preferred_element_type=F32)

    @pl.when(f == pl.num_programs(1) - 1)
    def _():
        o_ref[...] = _layer_norm(DN_ALPHA * x_ref[...] + acc_sc[...], g_ref[...], b_ref[...])


def _ffn(x2, comb, wg, wu, wd, ln_g, ln_b, tm, tf):
    m = x2.shape[0]
    use_comb = comb is not None
    if use_comb:
        n_f = wg.shape[0]
        w_specs = [pl.BlockSpec((1,) + wg.shape[1:], lambda i, f: (f, 0, 0)),
                   pl.BlockSpec((1,) + wu.shape[1:], lambda i, f: (f, 0, 0)),
                   pl.BlockSpec((1,) + wd.shape[1:], lambda i, f: (f, 0, 0))]
    else:
        n_f = wg.shape[1] // tf
        w_specs = [pl.BlockSpec((D_MODEL, tf), lambda i, f: (0, f)),
                   pl.BlockSpec((D_MODEL, tf), lambda i, f: (0, f)),
                   pl.BlockSpec((tf, D_MODEL), lambda i, f: (f, 0))]
    vec = pl.BlockSpec((1, D_MODEL), lambda i, f: (0, 0))
    in_specs = [pl.BlockSpec((tm, D_MODEL), lambda i, f: (i, 0))]
    args = [x2]
    if use_comb:
        in_specs.append(pl.BlockSpec((tm, LANES), lambda i, f: (i, 0)))
        args.append(comb)
    return pl.pallas_call(
        functools.partial(_ffn_kernel, use_comb=use_comb),
        grid=(m // tm, n_f),
        in_specs=in_specs + w_specs + [vec, vec],
        out_specs=pl.BlockSpec((tm, D_MODEL), lambda i, f: (i, 0)),
        out_shape=jax.ShapeDtypeStruct((m, D_MODEL), F32),
        scratch_shapes=[pltpu.VMEM((tm, D_MODEL), F32)],
        compiler_params=_cparams("parallel", "arbitrary"),
        name="ffn",
    )(*args, wg.astype(BF16), wu.astype(BF16), wd.astype(BF16), ln_g.reshape(1, D_MODEL), ln_b.reshape(1, D_MODEL))


def _token_mixer(x3, t_real, qpos0, past, lp, sp, h0_re, h0_im):
    b, tp, _ = x3.shape
    rows = b * tp
    x2 = x3.reshape(rows, D_MODEL)
    proj2 = _matmul(x2, lp["w_proj"], min(256, rows), PC_END // 2)
    proj3 = proj2.reshape(b, tp, PC_END)
    kva = proj3[:, :t_real, PC_KVA:PC_END]
    new_cmp, new_sel, new_win = (kva[:, :, i * 2 * NSA_KV_W:(i + 1) * 2 * NSA_KV_W] for i in range(3))
    new_moba = proj3[:, :t_real, PC_KVB:PC_KVA]
    cw = _cmp_weights(lp["phi1_k"], lp["phi1_v"])
    cmp_args = (lp["pe_k"], lp["pe_v"], lp["phi1_k"], lp["phi1_v"], lp["phi2_k"], lp["phi2_v"])
    rw_a, rw_b = 2 * NSA_KV_W, 2 * MOBA_KV_W

    qa_col = PC_QA // (NSA_HEADS * NSA_KV_W)
    qb_col = PC_QB // (MOBA_HEADS * MOBA_KV_W)
    if past is None:
        assert qpos0 == 0 and b == 1 and t_real == tp and t_real % MOBA_BLOCK == 0
        tq_sel = 128
        tq = 256 if tp % 256 == 0 else 128
        tk = 512 if tp % 512 == 0 else tp
        ab = _cmp_ab_linear(new_cmp.reshape(t_real // NSA_CMP_STRIDE, NSA_CMP_STRIDE * rw_a), cw)
        kc = _cmp_finish(ab, 1, *cmp_args)
        n_kt = tp // tk
        assert n_kt <= LANES
        o_c, sel_mask, act = _cmp_attention(proj3, kc, 0, t_real, tq_sel, act_blocks=tk // NSA_SEL_BLOCK)
        act = (act[0, :, 0, :n_kt] > 0.5).reshape(tp // tq, tq // tq_sel, n_kt).any(axis=1)
        o_s = _flash_worklist(proj3, qa_col, proj3, (PC_KVA + rw_a) // rw_a, sel_mask, act, _attn_cfg("sel"),
                              tq=tq, tk=tk)
        n_win = min(n_kt, (NSA_WINDOW + tq - 2) // tk + 2)
        o_w = _flash_linear(proj3, qa_col, proj3, (PC_KVA + 2 * rw_a) // rw_a, None,
                            _attn_cfg("win"), tq=tq, tk=tk, qpos0=0, kpos0=0, n_steps=n_win)
        kmean = _moba_kmean(proj3, None)
        mb_bias = _moba_gate(proj3, kmean, 0, t_real, tq_sel, True)
        o_m = _flash_linear(proj3, qb_col, proj3, PC_KVB // rw_b, mb_bias, _attn_cfg("moba", rowbias=True),
                            tq=tq, tk=MOBA_BLOCK, qpos0=0, kpos0=0, n_steps=tp // MOBA_BLOCK)
        y5, h_re, h_im = _s5_prompt(proj2[:, PC_U:PC_U + S5_WIDTH], sp)
        keep = min(NSA_WINDOW, t_real)
        win_state = new_win[:, t_real - keep:]
    else:
        assert t_real < NSA_CMP_STRIDE and qpos0 % MOBA_BLOCK == 0 and tp % SUBLANES == 0
        pt, base = past["page_table"], past["page_base"]
        ab = _cmp_ab_paged(past["cmp_t"], base, pt, cw)
        kc = _cmp_finish(ab, b, *cmp_args)
        o_c, sel_mask, _ = _cmp_attention(proj3, kc, qpos0, t_real, tp)
        o_s = _decode_attn(proj3, qa_col, (PC_KVA + rw_a) // rw_a, past["sel_t"], pt, base, sel_mask,
                           _attn_cfg("sel"), qpos0=qpos0, kpos0=0)
        wb = past["win_t"].shape[2]
        o_w = _decode_attn(proj3, qa_col, (PC_KVA + 2 * rw_a) // rw_a, past["win_t"],
                           jnp.arange(b, dtype=jnp.int32).reshape(b, 1), past["win_base"], None,
                           _attn_cfg("win"), qpos0=qpos0, kpos0=qpos0 - wb)
        kmean = _moba_kmean(past["moba_t"], pt, base)
        mb_mask = _moba_gate(proj3, kmean, qpos0, t_real, tp, False)
        o_m = _decode_attn(proj3, qb_col, PC_KVB // rw_b, past["moba_t"], pt, base, mb_mask,
                           _attn_cfg("moba"), qpos0=qpos0, kpos0=0)
        y5r, h_re, h_im = _s5_sample(proj3[:, :t_real, PC_U:PC_U + S5_WIDTH], h0_re, h0_im, sp)
        y5 = jnp.pad(y5r, ((0, 0), (0, tp - t_real), (0, 0))).reshape(rows, S5_WIDTH)
        win_state = jnp.concatenate([past["win_rows"], new_win], axis=1)[:, -wb:]

    flat = lambda a: a.reshape(rows, a.shape[-1])
    x1 = _merge(x2, flat(o_c), flat(o_s), flat(o_w), flat(o_m), y5, proj2, lp, min(256, rows))
    return x1, (new_cmp, new_sel, win_state, new_moba, h_re, h_im)


def _layer(l, x3, t_real, qpos0, past, h0_re, h0_im, lp, sp, ffn_p):
    b, tp, _ = x3.shape
    x1, st = _token_mixer(x3, t_real, qpos0, past, lp, sp, h0_re, h0_im)
    tm = min(512, x1.shape[0])
    if l % 2 == 0:
        x2 = _ffn(x1, None, ffn_p["wg"], ffn_p["wu"], ffn_p["wd"], lp["ln2_g"], lp["ln2_b"], tm, ffn_p["tf"])
    else:
        comb = _router(x1, ffn_p["w_router"], ffn_p["b_router"], tm)
        x2 = _ffn(x1, comb, ffn_p["wg"], ffn_p["wu"], ffn_p["wd"], lp["ln2_g"], lp["ln2_b"], tm, None)
    return x2.reshape(b, tp, D_MODEL), st


def kernel(x_prompt, x_sample, cache_nsa_cmp, cache_nsa_sel, cache_nsa_win, cache_moba, state_s5_re, state_s5_im,
           page_table, w_in, pe_k, phi1_k, phi2_k, pe_v, phi1_v, phi2_v, s5_lambda_re, s5_lambda_im, s5_log_dt,
           s5_b_re, s5_b_im, s5_c_re, s5_c_im, s5_d, s5_w_glu, s5_b_glu, w_br_nsa, w_br_moba, w_br_s5, w_out,
           ln1_g, ln1_b, ln2_g, ln2_b, ffn_w_gate, ffn_w_up, ffn_w_down, moe_w_router, moe_b_router,
           moe_w_gate, moe_w_up, moe_w_down):
    depth = w_in.shape[0]
    bp, seq, _ = x_prompt.shape
    bs, dec_seq, _ = x_sample.shape
    n_pool = cache_moba.shape[1]
    past_len = page_table.shape[1] * PAGE_SIZE
    tps = _round_up(dec_seq, SUBLANES)
    y_p = x_prompt
    y_s = jnp.pad(x_sample, ((0, 0), (0, tps - dec_seq), (0, 0)))
    rw_a, rw_b = 2 * NSA_KV_W, 2 * MOBA_KV_W

    def tiles_t(cache):
        d, n, r = cache.shape[:3]
        return cache.transpose(0, 1, 3, 4, 5, 2).reshape(d * n, -1, r)

    cmp_t, sel_t, moba_t, win_t = (tiles_t(c) for c in (cache_nsa_cmp, cache_nsa_sel, cache_moba, cache_nsa_win))
    p_st, s_st = [], []
    for l in range(depth):
        lp = dict(w_proj=_proj_weight(w_in[l]), pe_k=pe_k[l], phi1_k=phi1_k[l], phi2_k=phi2_k[l], pe_v=pe_v[l],
                  phi1_v=phi1_v[l], phi2_v=phi2_v[l], s5_lambda_re=s5_lambda_re[l], s5_lambda_im=s5_lambda_im[l],
                  s5_log_dt=s5_log_dt[l], s5_b_re=s5_b_re[l], s5_b_im=s5_b_im[l], s5_c_re=s5_c_re[l],
                  s5_c_im=s5_c_im[l], s5_d=s5_d[l], s5_w_glu=s5_w_glu[l], s5_b_glu=s5_b_glu[l],
                  w_br_nsa=w_br_nsa[l], w_br_moba=w_br_moba[l], w_br_s5=w_br_s5[l], w_out=w_out[l],
                  ln1_g=ln1_g[l], ln1_b=ln1_b[l], ln2_g=ln2_g[l], ln2_b=ln2_b[l])
        sp = _s5_params(lp)
        if l % 2 == 0:
            d_ff = ffn_w_gate.shape[2]
            tf = d_ff // 2 if (d_ff // 2) % LANES == 0 else d_ff
            ffn_p = dict(wg=ffn_w_gate[l // 2], wu=ffn_w_up[l // 2], wd=ffn_w_down[l // 2], tf=tf)
        else:
            ffn_p = dict(wg=moe_w_gate[l // 2], wu=moe_w_up[l // 2], wd=moe_w_down[l // 2],
                         w_router=moe_w_router[l // 2], b_router=moe_b_router[l // 2])
        y_p, st = _layer(l, y_p, seq, 0, None, None, None, lp, sp, ffn_p)
        p_st.append(st)
        past = dict(page_table=page_table, page_base=l * n_pool, cmp_t=cmp_t, sel_t=sel_t, moba_t=moba_t,
                    win_t=win_t, win_base=l * bs, win_rows=cache_nsa_win[l].reshape(bs, -1, rw_a))
        y_s, st = _layer(l, y_s, dec_seq, past_len, past, state_s5_re[l].reshape(bs, S5_LANES),
                         state_s5_im[l].reshape(bs, S5_LANES), lp, sp, ffn_p)
        s_st.append(st)

    def stk(states, j, tail):
        return jnp.stack([st[j].reshape(st[j].shape[:2] + tail) if tail else st[j] for st in states], axis=0)

    row_a = (2, NSA_KV_HEADS, HEAD_DIM)
    row_b = (2, MOBA_KV_HEADS, HEAD_DIM)

    def s5_state(states, j, b):
        return jnp.stack([st[j].reshape(b, S5_GROUPS, S5_STATE) for st in states], axis=0)

    return (y_p, y_s[:, :dec_seq],
            stk(p_st, 0, row_a), stk(p_st, 1, row_a), stk(p_st, 2, row_a), stk(p_st, 3, row_b),
            s5_state(p_st, 4, bp), s5_state(p_st, 5, bp),
            stk(s_st, 0, row_a), stk(s_st, 1, row_a), stk(s_st, 2, row_a), stk(s_st, 3, row_b),
            s5_state(s_st, 4, bs), s5_state(s_st, 5, bs))
```

```python
import functools
import math

import numpy as np
import jax
import jax.numpy as jnp
from jax import lax
from jax.experimental import pallas as pl
from jax.experimental.pallas import tpu as pltpu

F32 = jnp.float32
BF16 = jnp.bfloat16

D_MODEL = 1024
HEAD_DIM = 64
PAGE_SIZE = 128
NSA_HEADS = 8
NSA_KV_HEADS = 2
NSA_CMP_LEN = 32
NSA_CMP_STRIDE = 16
NSA_PHI_HIDDEN = 128
NSA_SEL_BLOCK = 64
NSA_SEL_TOPK = 16
NSA_N_LOCAL = 2
NSA_WINDOW = 512
MOBA_HEADS = 8
MOBA_KV_HEADS = 4
MOBA_BLOCK = 256
MOBA_TOPK = 3
S5_GROUPS = 32
S5_GROUP_CH = 16
S5_WIDTH = S5_GROUPS * S5_GROUP_CH
S5_STATE = 64
S5_LANES = S5_GROUPS * S5_STATE
N_BRANCH = 3
N_EXPERTS = 8
MOE_TOP_K = 2
LN_EPS = 1e-5
DEPTH = 2
DN_ALPHA = (2.0 * DEPTH) ** 0.25

NSA_Q_W = NSA_HEADS * HEAD_DIM
NSA_KV_W = NSA_KV_HEADS * HEAD_DIM
MOBA_Q_W = MOBA_HEADS * HEAD_DIM
MOBA_KV_W = MOBA_KV_HEADS * HEAD_DIM
IN_SPLITS = (NSA_Q_W, 6 * NSA_KV_W, 3 * NSA_HEADS, MOBA_Q_W, 2 * MOBA_KV_W, S5_WIDTH, N_BRANCH * D_MODEL)
IN_OFFS = tuple(int(v) for v in np.cumsum((0,) + IN_SPLITS))

LANES = 128
SUBLANES = 8
VMEM_LIMIT = 56 * 1024 * 1024

PC_QB = 0
PC_QA = PC_QB + MOBA_HEADS * MOBA_KV_W
PC_GM = PC_QA + NSA_HEADS * NSA_KV_W
PC_GA = PC_GM + N_BRANCH * D_MODEL
PC_U = PC_GA + 3 * NSA_Q_W
PC_KVB = PC_U + S5_WIDTH
PC_KVA = PC_KVB + 2 * MOBA_KV_W
PC_END = PC_KVA + 6 * NSA_KV_W

NEG = -1e30
LOG2E = 1.0 / math.log(2.0)
SLOPES_A = tuple(float(2.0 ** (-8.0 * (i + 1) / NSA_HEADS)) * LOG2E for i in range(NSA_HEADS))
SLOPES_B = tuple(float(2.0 ** (-8.0 * (i + 1) / MOBA_HEADS)) * LOG2E for i in range(MOBA_HEADS))


def _cparams(*sem):
    return pltpu.CompilerParams(dimension_semantics=sem, vmem_limit_bytes=VMEM_LIMIT)


def _round_up(n, m):
    return -(-n // m) * m


def _bf(x):
    return x.astype(BF16)


def _dot(a, b):
    return jnp.dot(_bf(a), _bf(b), preferred_element_type=F32)


def _dot_nt(a, b):
    return lax.dot_general(_bf(a), _bf(b), (((1,), (1,)), ((), ())), preferred_element_type=F32)


def _split(a):
    hi = a.astype(BF16)
    lo = (a - hi.astype(F32)).astype(BF16)
    return hi, lo


def _dot3(a, b):
    ah, al = _split(a)
    bh, bl = _split(b)
    d = functools.partial(jnp.dot, preferred_element_type=F32)
    return d(ah, bh) + (d(ah, bl) + d(al, bh))


def _dot3_nt(a, b):
    ah, al = _split(a)
    bh, bl = _split(b)
    d = functools.partial(lax.dot_general, dimension_numbers=(((1,), (1,)), ((), ())),
                          preferred_element_type=F32)
    return d(ah, bh) + (d(ah, bl) + d(al, bh))


def _dot2_exact_rhs(a, b_bf16):
    ah, al = _split(a)
    d = functools.partial(jnp.dot, preferred_element_type=F32)
    return d(ah, b_bf16) + d(al, b_bf16)


def _sigmoid(x):
    return 1.0 / (1.0 + jnp.exp(-x))


def _gelu(x):
    c = math.sqrt(2.0 / math.pi)
    return x * (0.5 * (1.0 + jnp.tanh(c * (x + 0.044715 * (x * x * x)))))


def _layer_norm(v, g, b):
    mu = jnp.mean(v, axis=-1, keepdims=True)
    vc = v - mu
    var = jnp.mean(vc * vc, axis=-1, keepdims=True)
    return vc * lax.rsqrt(var + LN_EPS) * g + b


def _mm_kernel(x_ref, w_ref, o_ref):
    o_ref[...] = jnp.dot(_bf(x_ref[...]), w_ref[...], preferred_element_type=F32)


def _matmul(x, w, tm, tn):
    m, k = x.shape
    n = w.shape[1]
    return pl.pallas_call(
        _mm_kernel,
        grid=(n // tn, m // tm),
        in_specs=[pl.BlockSpec((tm, k), lambda j, i: (i, 0)),
                  pl.BlockSpec((k, tn), lambda j, i: (0, j))],
        out_specs=pl.BlockSpec((tm, tn), lambda j, i: (i, j)),
        out_shape=jax.ShapeDtypeStruct((m, n), F32),
        compiler_params=_cparams("parallel", "arbitrary"),
        name="proj",
    )(x, w)


def _proj_weight(w_in):
    scale = HEAD_DIM ** -0.5 * LOG2E
    qa = w_in[:, IN_OFFS[0]:IN_OFFS[1]].reshape(D_MODEL, NSA_HEADS, 1, HEAD_DIM) * scale
    oh_a = np.zeros((NSA_HEADS, NSA_KV_HEADS, 1), np.float32)
    for h in range(NSA_HEADS):
        oh_a[h, h // (NSA_HEADS // NSA_KV_HEADS)] = 1.0
    qa = (qa * oh_a[None]).reshape(D_MODEL, NSA_HEADS * NSA_KV_W)
    qb = w_in[:, IN_OFFS[3]:IN_OFFS[4]].reshape(D_MODEL, MOBA_HEADS, 1, HEAD_DIM) * scale
    oh_b = np.zeros((MOBA_HEADS, MOBA_KV_HEADS, 1), np.float32)
    for h in range(MOBA_HEADS):
        oh_b[h, h // (MOBA_HEADS // MOBA_KV_HEADS)] = 1.0
    qb = (qb * oh_b[None]).reshape(D_MODEL, MOBA_HEADS * MOBA_KV_W)
    ga = w_in[:, IN_OFFS[2]:IN_OFFS[3]].reshape(D_MODEL, NSA_HEADS, 3).transpose(0, 2, 1)
    ga = jnp.broadcast_to(ga[..., None], (D_MODEL, 3, NSA_HEADS, HEAD_DIM)).reshape(D_MODEL, 3 * NSA_Q_W)
    gm = w_in[:, IN_OFFS[6]:IN_OFFS[7]]
    u = w_in[:, IN_OFFS[5]:IN_OFFS[6]]
    kvb = w_in[:, IN_OFFS[4]:IN_OFFS[5]]
    kva = w_in[:, IN_OFFS[1]:IN_OFFS[2]]
    return jnp.concatenate([qb, qa, gm, ga, u, kvb, kva], axis=1).astype(BF16)


def _cmpab_kernel(x_ref, w_ref, o_ref):
    row_w = 2 * NSA_KV_W
    x = x_ref[...]
    for kv in range(2):
        xk = jnp.concatenate(
            [x[:, r * row_w + kv * NSA_KV_W: r * row_w + (kv + 1) * NSA_KV_W] for r in range(NSA_CMP_STRIDE)],
            axis=1)
        o_ref[:, kv * 4 * NSA_PHI_HIDDEN:(kv + 1) * 4 * NSA_PHI_HIDDEN] = _dot(xk, w_ref[kv])


def _cmp_weights(phi1_k, phi1_v):
    eye = np.eye(NSA_KV_HEADS, dtype=np.float32)
    out = []
    for phi in (phi1_k, phi1_v):
        halves = []
        for half in range(NSA_CMP_LEN // NSA_CMP_STRIDE):
            w = phi[half * NSA_CMP_STRIDE * HEAD_DIM:(half + 1) * NSA_CMP_STRIDE * HEAD_DIM]
            w = w.reshape(NSA_CMP_STRIDE, 1, HEAD_DIM, 1, NSA_PHI_HIDDEN)
            w = w * eye[None, :, None, :, None]
            halves.append(w.reshape(NSA_CMP_STRIDE * NSA_KV_W, NSA_KV_HEADS * NSA_PHI_HIDDEN))
        out.append(jnp.concatenate(halves, axis=1))
    return jnp.stack(out).astype(BF16)


def _cmp_ab_linear(x, w):
    mc = x.shape[0]
    tmc = min(mc, 256)
    n_out = 8 * NSA_PHI_HIDDEN
    return pl.pallas_call(
        _cmpab_kernel,
        grid=(mc // tmc,),
        in_specs=[pl.BlockSpec((tmc, x.shape[1]), lambda i: (i, 0)),
                  pl.BlockSpec(w.shape, lambda i: (0, 0, 0))],
        out_specs=pl.BlockSpec((tmc, n_out), lambda i: (i, 0)),
        out_shape=jax.ShapeDtypeStruct((mc, n_out), F32),
        compiler_params=_cparams("parallel"),
        name="cmp_ab",
    )(x, w)


CMP_PAGES_PER_STEP = 16


def _cmpab_t_kernel(*refs, npp):
    page_refs, w_ref, o_ref, pk_sc, pv_sc = refs[1:1 + npp], refs[1 + npp], refs[2 + npp], refs[3 + npp], refs[4 + npp]
    w = NSA_KV_W
    p = PAGE_SIZE
    eye = jnp.where(lax.broadcasted_iota(jnp.int32, (p, p), 0) == lax.broadcasted_iota(jnp.int32, (p, p), 1),
                    1.0, 0.0).astype(BF16)
    for k in range(npp):
        page_t = page_refs[k][0]
        pk_sc[k * p:(k + 1) * p, :] = _dot_nt(eye, page_t[:w])
        pv_sc[k * p:(k + 1) * p, :] = _dot_nt(eye, page_t[w:])
    m = npp * (p // NSA_CMP_STRIDE)
    n_half = 4 * NSA_PHI_HIDDEN
    for kv, sc in ((0, pk_sc), (1, pv_sc)):
        acc = None
        for r in range(0, NSA_CMP_STRIDE, 2):
            x = jnp.concatenate([sc[pl.ds(r, m, stride=NSA_CMP_STRIDE), :],
                                 sc[pl.ds(r + 1, m, stride=NSA_CMP_STRIDE), :]], axis=1)
            part = _dot(x, w_ref[kv, r // 2])
            acc = part if acc is None else acc + part
        o_ref[:, kv * n_half:(kv + 1) * n_half] = acc


def _cmp_ab_paged(pool_t, base, page_table, w):
    b, n_pages = page_table.shape
    cpp = PAGE_SIZE // NSA_CMP_STRIDE
    npp = math.gcd(CMP_PAGES_PER_STEP, n_pages)
    steps = n_pages // npp
    n_out = 8 * NSA_PHI_HIDDEN
    w4 = w.reshape(2, NSA_CMP_STRIDE // 2, 2 * NSA_KV_W, w.shape[-1])

    def x_map(k):
        return lambda bi, s, pt: (base + pt[bi * n_pages + s * npp + k], 0, 0)

    grid_spec = pltpu.PrefetchScalarGridSpec(
        num_scalar_prefetch=1,
        grid=(b, steps),
        in_specs=[pl.BlockSpec((1,) + pool_t.shape[1:], x_map(k)) for k in range(npp)]
        + [pl.BlockSpec(w4.shape, lambda bi, s, pt: (0, 0, 0, 0))],
        out_specs=pl.BlockSpec((npp * cpp, n_out), lambda bi, s, pt: (bi * steps + s, 0)),
        scratch_shapes=[pltpu.VMEM((npp * PAGE_SIZE, NSA_KV_W), F32), pltpu.VMEM((npp * PAGE_SIZE, NSA_KV_W), F32)],
    )
    return pl.pallas_call(
        functools.partial(_cmpab_t_kernel, npp=npp),
        grid_spec=grid_spec,
        out_shape=jax.ShapeDtypeStruct((b * n_pages * cpp, n_out), F32),
        compiler_params=_cparams("parallel", "arbitrary"),
        name="cmp_ab_paged",
    )(page_table.reshape(-1), *([pool_t] * npp), w4)


def _cmpfin_kernel(ab_ref, pe_ref, w1_ref, w2_ref, o_ref):
    mc = ab_ref.shape[0]
    hid2 = NSA_KV_HEADS * NSA_PHI_HIDDEN
    for kv in range(2):
        c = _dot(pe_ref[kv], w1_ref[kv])[0:1]
        c = jnp.concatenate([c] * NSA_KV_HEADS, axis=1)
        a = ab_ref[:, kv * 2 * hid2: kv * 2 * hid2 + hid2]
        bnext = pltpu.roll(ab_ref[:, kv * 2 * hid2 + hid2:(kv + 1) * 2 * hid2], mc - 1, 0)
        hpre = a + bnext + c
        o_ref[:, kv * NSA_KV_W:(kv + 1) * NSA_KV_W] = _dot(_gelu(hpre), w2_ref[kv])


def _cmp_finish(ab, nb, pe_k, pe_v, phi1_k, phi1_v, phi2_k, phi2_v):
    mc = ab.shape[0] // nb
    flat = NSA_CMP_LEN * HEAD_DIM
    pe = jnp.stack([jnp.broadcast_to(p.reshape(1, flat), (SUBLANES, flat)) for p in (pe_k, pe_v)])
    w1 = jnp.stack([phi1_k, phi1_v]).astype(BF16)
    eye = np.eye(NSA_KV_HEADS, dtype=np.float32)
    w2 = jnp.stack([(p[None, :, None, :] * eye[:, None, :, None]).reshape(NSA_KV_HEADS * NSA_PHI_HIDDEN, NSA_KV_W)
                    for p in (phi2_k, phi2_v)]).astype(BF16)
    out = pl.pallas_call(
        _cmpfin_kernel,
        grid=(nb,),
        in_specs=[pl.BlockSpec((mc, ab.shape[1]), lambda i: (i, 0)),
                  pl.BlockSpec(pe.shape, lambda i: (0, 0, 0)),
                  pl.BlockSpec(w1.shape, lambda i: (0, 0, 0)),
                  pl.BlockSpec(w2.shape, lambda i: (0, 0, 0))],
        out_specs=pl.BlockSpec((mc, 2 * NSA_KV_W), lambda i: (i, 0)),
        out_shape=jax.ShapeDtypeStruct((nb * mc, 2 * NSA_KV_W), F32),
        compiler_params=_cparams("parallel"),
        name="cmp_finish",
    )(ab, pe, w1, w2)
    return out.reshape(nb, mc, 2 * NSA_KV_W)


def _stack_heads(q, n_heads, w):
    return jnp.concatenate([_bf(q[:, h * w:(h + 1) * w]) for h in range(n_heads)], axis=0)


def _place_heads(o_heads, n_heads, n_groups, o_ref):
    hpg = n_heads // n_groups
    tq = o_heads[0].shape[0]
    lane = lax.broadcasted_iota(jnp.int32, (tq, LANES), 1)
    for pair in range(n_heads // 2):
        pieces = []
        for h in (2 * pair, 2 * pair + 1):
            src = (h // hpg) * HEAD_DIM
            piece = o_heads[h][:, (src // LANES) * LANES:(src // LANES + 1) * LANES]
            if src % LANES != (h % 2) * HEAD_DIM:
                piece = pltpu.roll(piece, HEAD_DIM, 1)
            pieces.append(piece)
        o_ref[0, :, pair * LANES:(pair + 1) * LANES] = jnp.where(lane < HEAD_DIM, pieces[0], pieces[1])


def _top_select(score, lanef, n_pick):
    sel = jnp.zeros(score.shape, F32)
    for _ in range(n_pick):
        m = jnp.max(score, axis=-1, keepdims=True)
        idx = jnp.min(jnp.where(score == m, lanef, 1e9), axis=-1, keepdims=True)
        pick = lanef == idx
        sel = jnp.where(pick & (m > -jnp.inf), 1.0, sel)
        score = jnp.where(pick, -jnp.inf, score)
    return sel


def _cmpattn_kernel(q_ref, kc_ref, ov_ref, o_ref, sel_ref, act_ref, *, tq, qpos0, n_cmp, n_sel_blk, act_blocks):
    w = NSA_KV_W
    hpg = NSA_HEADS // NSA_KV_HEADS
    ncp = kc_ref.shape[1]
    nbp = ov_ref.shape[1]
    qbase = qpos0 + pl.program_id(1) * tq
    qs = _stack_heads(q_ref[0], NSA_HEADS, w)
    kc = kc_ref[0]
    s_all = _dot_nt(qs, kc[:, :w])
    c_idx = lax.broadcasted_iota(jnp.int32, (1, ncp), 1)
    row = lax.broadcasted_iota(jnp.int32, (tq, 1), 0)
    cend_rel = c_idx * NSA_CMP_STRIDE + (NSA_CMP_LEN - 1) - qbase
    valid = (row - cend_rel >= 0) & (c_idx < n_cmp)
    colf = cend_rel.astype(F32)
    ps, psums = [], []
    for h in range(NSA_HEADS):
        s = jnp.where(valid, s_all[h * tq:(h + 1) * tq] + SLOPES_A[h] * colf, NEG)
        m = jnp.max(s, axis=-1, keepdims=True)
        e = jnp.where(valid, jnp.exp2(s - m), 0.0)
        p = e / jnp.maximum(jnp.sum(e, axis=-1, keepdims=True), 1e-30)
        ps.append(_bf(p))
        if h % hpg == 0:
            psums.append(p)
        else:
            psums[-1] = psums[-1] + p
    o_all = jnp.dot(jnp.concatenate(ps, axis=0), _bf(kc[:, w:]), preferred_element_type=F32)
    _place_heads([o_all[h * tq:(h + 1) * tq] for h in range(NSA_HEADS)], NSA_HEADS, NSA_KV_HEADS, o_ref)

    lane = lax.broadcasted_iota(jnp.int32, (tq, nbp), 1)
    lanef = lane.astype(F32)
    cur = (qbase + row) // NSA_SEL_BLOCK
    forced = (lane == 0) | ((lane <= cur) & (lane > cur - NSA_N_LOCAL))
    validb = (lane <= cur) & (lane < n_sel_blk)
    n_top = min(NSA_SEL_TOPK, n_sel_blk)
    scores = [jnp.where(forced, jnp.inf, jnp.where(validb, _dot2_exact_rhs(psums[g], ov_ref[...]), -jnp.inf))
              for g in range(NSA_KV_HEADS)]
    sel_all = _top_select(jnp.concatenate(scores, axis=0), jnp.concatenate([lanef] * NSA_KV_HEADS, axis=0), n_top)
    for g in range(NSA_KV_HEADS):
        sel_ref[0, g] = sel_all[g * tq:(g + 1) * tq]
    any_blk = jnp.max(sel_all, axis=0, keepdims=True)
    pool = jnp.where((lax.broadcasted_iota(jnp.int32, (nbp, LANES), 0) // act_blocks)
                     == lax.broadcasted_iota(jnp.int32, (nbp, LANES), 1), 1.0, 0.0).astype(BF16)
    act_ref[0, 0] = jnp.dot(_bf(jnp.broadcast_to(any_blk, (SUBLANES, nbp))), pool, preferred_element_type=F32)


def _cmp_attention(proj3, kc, qpos0, t_real, tq, act_blocks=SUBLANES):
    b, tp, _ = proj3.shape
    ncp = kc.shape[1]
    seq = qpos0 + t_real
    n_cmp = seq // NSA_CMP_STRIDE - NSA_CMP_LEN // NSA_CMP_STRIDE + 1
    n_sel_blk = -(-seq // NSA_SEL_BLOCK)
    nbp = _round_up(n_sel_blk, LANES)
    cs = np.arange(ncp)[:, None] * NSA_CMP_STRIDE
    ss = np.arange(nbp)[None, :] * NSA_SEL_BLOCK
    ov = ((cs + NSA_CMP_LEN > ss) & (cs < ss + NSA_SEL_BLOCK)
          & (np.arange(ncp)[:, None] < n_cmp) & (np.arange(nbp)[None, :] < n_sel_blk))
    ov = jnp.asarray(ov.astype(np.float32), BF16)
    qw = NSA_HEADS * NSA_KV_W
    return pl.pallas_call(
        functools.partial(_cmpattn_kernel, tq=tq, qpos0=qpos0, n_cmp=n_cmp, n_sel_blk=n_sel_blk,
                          act_blocks=act_blocks),
        grid=(b, tp // tq),
        in_specs=[pl.BlockSpec((1, tq, qw), lambda bi, i: (bi, i, PC_QA // qw)),
                  pl.BlockSpec((1, ncp, 2 * NSA_KV_W), lambda bi, i: (bi, 0, 0)),
                  pl.BlockSpec(ov.shape, lambda bi, i: (0, 0))],
        out_specs=[pl.BlockSpec((1, tq, NSA_Q_W), lambda bi, i: (bi, i, 0)),
                   pl.BlockSpec((1, NSA_KV_HEADS, tq, nbp), lambda bi, i: (bi, 0, i, 0)),
                   pl.BlockSpec((1, 1, SUBLANES, LANES), lambda bi, i: (bi, i, 0, 0))],
        out_shape=[jax.ShapeDtypeStruct((b, tp, NSA_Q_W), F32),
                   jax.ShapeDtypeStruct((b, NSA_KV_HEADS, tp, nbp), F32),
                   jax.ShapeDtypeStruct((b, tp // tq, SUBLANES, LANES), F32)],
        compiler_params=_cparams("parallel", "arbitrary"),
        name="cmp_attn",
    )(proj3, kc, ov)


def _kmean_kernel(*refs, n_x):
    x_refs, o_ref = refs[:n_x], refs[n_x]
    outs = [jnp.sum(r[0], axis=0, keepdims=True) * (1.0 / MOBA_BLOCK) for r in x_refs]
    o_ref[0, 0] = outs[0] if len(outs) == 1 else jnp.concatenate(outs, axis=0)


KMEAN_BLOCKS_PER_STEP = 16


def _moba_kmean(rows3, page_table, base=0):
    w = MOBA_KV_W
    if page_table is None:
        b, t, _ = rows3.shape
        n_blk = t // MOBA_BLOCK
        bps = math.gcd(KMEAN_BLOCKS_PER_STEP, n_blk)
        steps = n_blk // bps
        out = pl.pallas_call(
            functools.partial(_kmean_kernel, n_x=bps),
            grid=(b, steps),
            in_specs=[pl.BlockSpec((1, MOBA_BLOCK, w), (lambda bi, s, k=k: (bi, s * bps + k, PC_KVB // w)))
                      for k in range(bps)],
            out_specs=pl.BlockSpec((1, 1, bps, w), lambda bi, s: (bi, s, 0, 0)),
            out_shape=jax.ShapeDtypeStruct((b, steps, bps, w), F32),
            compiler_params=_cparams("parallel", "arbitrary"),
            name="kmean",
        )(*([rows3] * bps))
        return out.reshape(b, n_blk, w)
    b, n_pages = page_table.shape
    ppb = MOBA_BLOCK // PAGE_SIZE
    n_blk = n_pages // ppb
    bps = math.gcd(KMEAN_BLOCKS_PER_STEP, n_blk)
    steps = n_blk // bps
    npp = bps * ppb

    def x_map(k):
        return lambda bi, s, pt: (base + pt[bi * n_pages + s * npp + k], 0, 0)

    grid_spec = pltpu.PrefetchScalarGridSpec(
        num_scalar_prefetch=1,
        grid=(b, steps),
        in_specs=[pl.BlockSpec((1, w, PAGE_SIZE), x_map(k)) for k in range(npp)],
        out_specs=pl.BlockSpec((1, 1, bps, w), lambda bi, s, pt: (bi, s, 0, 0)),
    )
    out = pl.pallas_call(
        functools.partial(_kmean_t_kernel, npp=npp),
        grid_spec=grid_spec,
        out_shape=jax.ShapeDtypeStruct((b, steps, bps, w), F32),
        compiler_params=_cparams("parallel", "arbitrary"),
        name="kmean_paged",
    )(page_table.reshape(-1), *([rows3] * npp))
    return out.reshape(b, n_blk, w)


def _kmean_t_kernel(*refs, npp):
    page_refs, o_ref = refs[1:1 + npp], refs[1 + npp]
    ppb = MOBA_BLOCK // PAGE_SIZE
    ones = jnp.ones((SUBLANES, PAGE_SIZE), BF16)
    d = functools.partial(lax.dot_general, dimension_numbers=(((1,), (1,)), ((), ())), preferred_element_type=F32)
    outs = []
    for i in range(npp // ppb):
        acc = None
        for k in range(ppb):
            hi, lo = _split(page_refs[i * ppb + k][0])
            part = d(ones, hi) + d(ones, lo)
            acc = part if acc is None else acc + part
        outs.append(acc[0:1] * (1.0 / MOBA_BLOCK))
    o_ref[0, 0] = outs[0] if len(outs) == 1 else jnp.concatenate(outs, axis=0)


def _mobagate_kernel(q_ref, km_ref, m_ref, *, tq, qpos0, k_top, as_bias):
    w = MOBA_KV_W
    nbp = km_ref.shape[1]
    qbase = qpos0 + pl.program_id(1) * tq
    q = q_ref[0]
    qs = jnp.concatenate([q[:, h * w:(h + 1) * w] for h in range(MOBA_HEADS)], axis=0)
    gs = _dot3_nt(qs, km_ref[0])
    rows = MOBA_HEADS * tq
    lane = lax.broadcasted_iota(jnp.int32, (rows, nbp), 1)
    own = (qbase + lax.broadcasted_iota(jnp.int32, (rows, 1), 0) % tq) // MOBA_BLOCK
    score = jnp.where(lane < own, gs, -jnp.inf)
    sel = jnp.where(lane == own, 1.0, _top_select(score, lane.astype(F32), k_top))
    out = jnp.where(sel > 0.5, 0.0, NEG) if as_bias else sel
    for h in range(MOBA_HEADS):
        m_ref[0, h] = out[h * tq:(h + 1) * tq]


def _moba_gate(proj3, kmean, qpos0, t_real, tq, as_bias):
    b, tp, _ = proj3.shape
    n_mb = -(-(qpos0 + t_real) // MOBA_BLOCK)
    nbp = _round_up(n_mb, LANES)
    km = jnp.pad(kmean, ((0, 0), (0, nbp - kmean.shape[1]), (0, 0)))
    qw = MOBA_HEADS * MOBA_KV_W
    return pl.pallas_call(
        functools.partial(_mobagate_kernel, tq=tq, qpos0=qpos0, k_top=min(MOBA_TOPK, n_mb), as_bias=as_bias),
        grid=(b, tp // tq),
        in_specs=[pl.BlockSpec((1, tq, qw), lambda bi, i: (bi, i, PC_QB // qw)),
                  pl.BlockSpec((1, nbp, MOBA_KV_W), lambda bi, i: (bi, 0, 0))],
        out_specs=pl.BlockSpec((1, MOBA_HEADS, tq, nbp), lambda bi, i: (bi, 0, i, 0)),
        out_shape=jax.ShapeDtypeStruct((b, MOBA_HEADS, tp, nbp), F32),
        compiler_params=_cparams("parallel", "arbitrary"),
        name="moba_gate",
    )(proj3, km)


def _attend_tile(qs, k, v, mask, kbase, qbase, m_sc, l_sc, acc_sc, alpha_sc, *, n_heads, n_groups, n_mask, tq,
                 blk_shift, window, slopes, rowbias, causal=True, blk=None):
    tk = k.shape[0]
    s_all = _dot_nt(qs, k)
    delta = qbase - kbase
    col1 = lax.broadcasted_iota(jnp.int32, (1, tk), 1)
    colf = (col1 - delta).astype(F32)
    ok = None
    if causal or window is not None:
        dd = lax.broadcasted_iota(jnp.int32, (tq, tk), 0) - lax.broadcasted_iota(jnp.int32, (tq, tk), 1)
        ok = dd >= -delta
        if window is not None:
            ok = ok & (dd < window - delta)
    expand_mask = n_mask and not rowbias
    if expand_mask:
        nbp = mask.shape[-1]
        blk_col = (kbase + col1) >> blk_shift
        expand = jnp.where(lax.broadcasted_iota(jnp.int32, (nbp, tk), 0) == blk_col, 1.0, 0.0).astype(BF16)
        mexp = jnp.dot(_bf(mask.reshape(n_mask * tq, nbp)), expand, preferred_element_type=F32)
    if rowbias:
        nbp = mask.shape[-1]
        pick = jnp.where(lax.broadcasted_iota(jnp.int32, (nbp, LANES), 0) == blk, 1.0, 0.0).astype(BF16)
        rowb = jnp.dot(_bf(mask.reshape(n_heads * tq, nbp)), pick, preferred_element_type=F32)
    hpm = n_heads // n_mask if n_mask else n_heads
    ps = []
    valid = ok
    for h in range(n_heads):
        rows = slice(h * tq, (h + 1) * tq)
        s = s_all[rows] + slopes[h] * colf
        if rowbias:
            s = s + _lane_rep(rowb[rows], tk)
        elif expand_mask and h % hpm == 0:
            g = h // hpm
            valid = mexp[g * tq:(g + 1) * tq] > 0.5
            if ok is not None:
                valid = ok & valid
        if valid is not None:
            s = jnp.where(valid, s, NEG)
        m_old = m_sc[rows]
        m_new = jnp.maximum(m_old, jnp.max(s, axis=-1, keepdims=True))
        alpha_sc[rows] = jnp.exp2(m_old - m_new)
        m_sc[rows] = m_new
        ps.append(_bf(jnp.exp2(s - _lane_rep(m_new, tk))))
    p_all = jnp.concatenate(ps, axis=0)
    alpha = alpha_sc[...]
    l_sc[...] = alpha * l_sc[...] + jnp.dot(p_all, jnp.ones((tk, LANES), BF16), preferred_element_type=F32)
    acc_sc[...] = (_lane_rep(alpha, acc_sc.shape[1]) * acc_sc[...]
                   + jnp.dot(p_all, _bf(v), preferred_element_type=F32))


def _lane_rep(x, n):
    return x if n == LANES else jnp.concatenate([x] * (n // LANES), axis=1)


def _attn_finish(m_sc, l_sc, acc_sc, o_ref, n_heads, n_groups, tq):
    w = acc_sc.shape[1]
    o = jnp.where(_lane_rep(m_sc[...], w) > 0.5 * NEG, acc_sc[...] / _lane_rep(l_sc[...], w), 0.0)
    _place_heads([o[h * tq:(h + 1) * tq] for h in range(n_heads)], n_heads, n_groups, o_ref)


def _attn_init(q_ref, q_sc, m_sc, l_sc, acc_sc, n_heads, w):
    q_sc[...] = _stack_heads(q_ref[0], n_heads, w)
    m_sc[...] = jnp.full(m_sc.shape, NEG, F32)
    l_sc[...] = jnp.zeros(l_sc.shape, F32)
    acc_sc[...] = jnp.zeros(acc_sc.shape, F32)


def _kv_tile_range(qi, *, tq, tk, qpos0, kpos0, window, n_kt):
    qlo = qpos0 + qi * tq
    last = jnp.minimum((qlo + tq - 1 - kpos0) // tk, n_kt - 1)
    if window is None:
        first = 0
    else:
        first = jnp.maximum(qlo - window + 1 - kpos0, 0) // tk
    return first, last


def _flash_kernel(*refs, cfg, rng):
    if cfg["n_mask"]:
        q_ref, kv_ref, mask_ref, o_ref, q_sc, m_sc, l_sc, acc_sc, alpha_sc = refs
    else:
        q_ref, kv_ref, o_ref, q_sc, m_sc, l_sc, acc_sc, alpha_sc = refs
        mask_ref = None
    w, tq, tk = cfg["n_groups"] * HEAD_DIM, rng["tq"], rng["tk"]
    qi, j = pl.program_id(1), pl.program_id(2)
    first, last = _kv_tile_range(qi, **rng)

    @pl.when(j == 0)
    def _():
        _attn_init(q_ref, q_sc, m_sc, l_sc, acc_sc, cfg["n_heads"], w)

    jt = first + j
    kbase = rng["kpos0"] + jt * tk
    qbase = rng["qpos0"] + qi * tq

    def tile(causal):
        kv = kv_ref[0]
        _attend_tile(q_sc[...], kv[:, :w], kv[:, w:], None if mask_ref is None else mask_ref[0], kbase, qbase,
                     m_sc, l_sc, acc_sc, alpha_sc, tq=tq, causal=causal, blk=kbase >> cfg["blk_shift"], **cfg)

    if cfg["window"] is None:
        fully_past = kbase + tk <= qbase
        pl.when((jt <= last) & fully_past)(lambda: tile(False))
        pl.when((jt <= last) & jnp.logical_not(fully_past))(lambda: tile(True))
    else:
        pl.when(jt <= last)(lambda: tile(True))

    @pl.when(j == pl.num_programs(2) - 1)
    def _():
        _attn_finish(m_sc, l_sc, acc_sc, o_ref, cfg["n_heads"], cfg["n_groups"], tq)


def _flash_linear(q_arr, q_col, kv_arr, kv_col, mask, cfg, *, tq, tk, qpos0, kpos0, n_steps):
    b, tp, _ = q_arr.shape
    w = cfg["n_groups"] * HEAD_DIM
    n_heads = cfg["n_heads"]
    n_kt = kv_arr.shape[1] // tk
    rng = dict(tq=tq, tk=tk, qpos0=qpos0, kpos0=kpos0, window=cfg["window"], n_kt=n_kt)

    def kv_map(bi, i, j):
        first, last = _kv_tile_range(i, **rng)
        return (bi, jnp.minimum(first + j, last), kv_col)

    in_specs = [pl.BlockSpec((1, tq, n_heads * w), lambda bi, i, j: (bi, i, q_col)),
                pl.BlockSpec((1, tk, 2 * w), kv_map)]
    args = [q_arr, kv_arr]
    if cfg["n_mask"]:
        in_specs.append(pl.BlockSpec((1, cfg["n_mask"], tq, mask.shape[-1]), lambda bi, i, j: (bi, 0, i, 0)))
        args.append(mask)
    return pl.pallas_call(
        functools.partial(_flash_kernel, cfg=cfg, rng=rng),
        grid=(b, tp // tq, n_steps),
        in_specs=in_specs,
        out_specs=pl.BlockSpec((1, tq, n_heads * HEAD_DIM), lambda bi, i, j: (bi, i, 0)),
        out_shape=jax.ShapeDtypeStruct((b, tp, n_heads * HEAD_DIM), F32),
        scratch_shapes=[pltpu.VMEM((n_heads * tq, w), BF16), pltpu.VMEM((n_heads * tq, LANES), F32),
                        pltpu.VMEM((n_heads * tq, LANES), F32), pltpu.VMEM((n_heads * tq, w), F32),
                        pltpu.VMEM((n_heads * tq, LANES), F32)],
        compiler_params=_cparams("parallel", "parallel", "arbitrary"),
        name="flash",
    )(*args)


def _flash_t_kernel(q_ref, k_ref, vt_ref, rb_ref, o_ref, q_sc, m_sc, l_sc, acc_sc, *, cfg, rng):
    n_heads, n_groups, slopes = cfg["n_heads"], cfg["n_groups"], cfg["slopes"]
    w, tq, tk = n_groups * HEAD_DIM, rng["tq"], rng["tk"]
    qi, j = pl.program_id(1), pl.program_id(2)
    first, last = _kv_tile_range(qi, **rng)

    @pl.when(j == 0)
    def _():
        q_sc[...] = _stack_heads(q_ref[0], n_heads, w)
        m_sc[...] = jnp.full(m_sc.shape, NEG, F32)
        l_sc[...] = jnp.zeros(l_sc.shape, F32)
        acc_sc[...] = jnp.zeros(acc_sc.shape, F32)

    jt = first + j
    kbase = rng["kpos0"] + jt * tk
    qbase = rng["qpos0"] + qi * tq

    def tile(causal):
        s_t = _dot_nt(k_ref[0], q_sc[...])
        kb = (lax.broadcasted_iota(jnp.int32, (tk, LANES), 0) + (kbase - qbase)).astype(F32)
        rb = rb_ref[0, 0]
        if causal:
            ok = (lax.broadcasted_iota(jnp.int32, (tk, tq), 0) - lax.broadcasted_iota(jnp.int32, (tk, tq), 1)
                  <= qbase - kbase)
        hpg = n_heads // n_groups
        vrows = HEAD_DIM + SUBLANES
        for g in range(n_groups):
            ps, alphas = [], []
            for lt in range(g * hpg * tq // LANES, (g + 1) * hpg * tq // LANES):
                h = lt * LANES // tq
                lanes = slice(lt * LANES, (lt + 1) * LANES)
                s = s_t[:, lanes] + slopes[h] * kb + rb[:, lanes]
                if causal:
                    qq0 = (lt * LANES) % tq
                    s = jnp.where(ok[:, qq0:qq0 + LANES], s, NEG)
                m_old = m_sc[:, lanes]
                m_new = jnp.maximum(m_old, jnp.max(s, axis=0, keepdims=True))
                alphas.append(jnp.exp2(m_old - m_new))
                m_sc[:, lanes] = m_new
                ps.append(_bf(jnp.exp2(s - m_new)))
            glanes = slice(g * hpg * tq, (g + 1) * hpg * tq)
            alpha = jnp.concatenate(alphas, axis=1)
            pv = jnp.dot(_bf(vt_ref[0, g * vrows:(g + 1) * vrows, :]), jnp.concatenate(ps, axis=1),
                         preferred_element_type=F32)
            acc_sc[:, glanes] = alpha * acc_sc[:, glanes] + pv[:HEAD_DIM]
            l_sc[:, glanes] = alpha * l_sc[:, glanes] + pv[HEAD_DIM:HEAD_DIM + 1]

    fully_past = kbase + tk <= qbase
    pl.when((jt <= last) & fully_past)(lambda: tile(False))
    pl.when((jt <= last) & jnp.logical_not(fully_past))(lambda: tile(True))

    @pl.when(j == pl.num_programs(2) - 1)
    def _():
        o = jnp.where(m_sc[...] > 0.5 * NEG, acc_sc[...] / l_sc[...], 0.0)
        for pair in range(n_heads // 2):
            slab = jnp.concatenate([o[:, h * tq:(h + 1) * tq] for h in (2 * pair, 2 * pair + 1)], axis=0)
            o_ref[0, :, pair * LANES:(pair + 1) * LANES] = slab.T


def _flash_transposed(q_arr, q_col, k_arr, k_col, v_t, bias, cfg, *, tq, tk, n_steps):
    b, tp, _ = q_arr.shape
    assert b == 1 and cfg["rowbias"] and (1 << cfg["blk_shift"]) == tk
    w = cfg["n_groups"] * HEAD_DIM
    n_heads = cfg["n_heads"]
    nq, nbp = tp // tq, bias.shape[-1]
    r = n_heads * tq
    bias_t = bias[0].reshape(n_heads, nq, tq, nbp).transpose(1, 3, 0, 2).reshape(nq, nbp, 1, r)
    n_groups = cfg["n_groups"]
    v_t = jnp.concatenate([v_t.reshape(1, n_groups, HEAD_DIM, tp),
                           jnp.ones((1, n_groups, SUBLANES, tp), F32)], axis=2)
    vrows = n_groups * (HEAD_DIM + SUBLANES)
    v_t = v_t.reshape(1, vrows, tp)
    rng = dict(tq=tq, tk=tk, qpos0=0, kpos0=0, window=None, n_kt=tp // tk)

    def tile_of(i, j):
        first, last = _kv_tile_range(i, **rng)
        return jnp.minimum(first + j, last)

    return pl.pallas_call(
        functools.partial(_flash_t_kernel, cfg=cfg, rng=rng),
        grid=(b, nq, n_steps),
        in_specs=[pl.BlockSpec((1, tq, n_heads * w), lambda bi, i, j: (bi, i, q_col)),
                  pl.BlockSpec((1, tk, w), lambda bi, i, j: (bi, tile_of(i, j), k_col)),
                  pl.BlockSpec((1, vrows, tk), lambda bi, i, j: (bi, 0, tile_of(i, j))),
                  pl.BlockSpec((1, 1, 1, r), lambda bi, i, j: (i, tile_of(i, j), 0, 0))],
        out_specs=pl.BlockSpec((1, tq, n_heads * HEAD_DIM), lambda bi, i, j: (bi, i, 0)),
        out_shape=jax.ShapeDtypeStruct((b, tp, n_heads * HEAD_DIM), F32),
        scratch_shapes=[pltpu.VMEM((r, w), BF16), pltpu.VMEM((1, r), F32), pltpu.VMEM((1, r), F32),
                        pltpu.VMEM((HEAD_DIM, r), F32)],
        compiler_params=_cparams("parallel", "parallel", "arbitrary"),
        name="flash_t",
    )(q_arr, k_arr, v_t, bias_t)


def _worklist(act, tq, tk):
    nq, n_kt = act.shape
    i = np.arange(nq)[:, None]
    jt = np.arange(n_kt)[None, :]
    diag = np.minimum((i * tq + tq - 1) // tk, n_kt - 1)
    causal = jt <= diag
    s_max = int(causal.sum())
    need = (act | jnp.asarray(jt == diag)) & jnp.asarray(causal)
    key = jnp.where(need, jnp.asarray(i * n_kt + jt, jnp.int32), nq * n_kt).reshape(-1)
    key = jnp.sort(key)[:s_max]
    n_need = jnp.sum(need.astype(jnp.int32))
    pos = jnp.arange(s_max, dtype=jnp.int32)
    valid = pos < n_need
    key = jnp.where(valid, key, jnp.take(key, n_need - 1))
    qi, kt = key // n_kt, key % n_kt
    first = valid & ((pos == 0) | (qi != jnp.roll(qi, 1)))
    last = valid & ((pos == n_need - 1) | (qi != jnp.roll(qi, -1)))
    return jnp.concatenate([qi, kt, first.astype(jnp.int32), last.astype(jnp.int32),
                            valid.astype(jnp.int32)]).astype(jnp.int32), s_max


def _flash_list_kernel(wl_ref, q_ref, kv_ref, mask_ref, o_ref, q_sc, m_sc, l_sc, acc_sc, alpha_sc, *,
                       cfg, tq, tk, s_max):
    s = pl.program_id(0)
    w = cfg["n_groups"] * HEAD_DIM
    qbase = wl_ref[s] * tq
    kbase = wl_ref[s_max + s] * tk
    valid = wl_ref[4 * s_max + s] == 1

    @pl.when(wl_ref[2 * s_max + s] == 1)
    def _():
        _attn_init(q_ref, q_sc, m_sc, l_sc, acc_sc, cfg["n_heads"], w)

    def tile(causal):
        kv = kv_ref[0]
        _attend_tile(q_sc[...], kv[:, :w], kv[:, w:], mask_ref[0], kbase, qbase, m_sc, l_sc, acc_sc, alpha_sc,
                     tq=tq, causal=causal, **cfg)

    fully_past = kbase + tk <= qbase
    pl.when(valid & fully_past)(lambda: tile(False))
    pl.when(valid & jnp.logical_not(fully_past))(lambda: tile(True))

    @pl.when(wl_ref[3 * s_max + s] == 1)
    def _():
        _attn_finish(m_sc, l_sc, acc_sc, o_ref, cfg["n_heads"], cfg["n_groups"], tq)


def _flash_worklist(q_arr, q_col, kv_arr, kv_col, mask, act, cfg, *, tq, tk):
    b, tp, _ = q_arr.shape
    assert b == 1 and not cfg["rowbias"] and cfg["window"] is None
    w = cfg["n_groups"] * HEAD_DIM
    n_heads = cfg["n_heads"]
    wl, s_max = _worklist(act, tq, tk)
    grid_spec = pltpu.PrefetchScalarGridSpec(
        num_scalar_prefetch=1,
        grid=(s_max,),
        in_specs=[pl.BlockSpec((1, tq, n_heads * w), lambda s, wl: (0, wl[s], q_col)),
                  pl.BlockSpec((1, tk, 2 * w), lambda s, wl: (0, wl[s_max + s], kv_col)),
                  pl.BlockSpec((1, cfg["n_mask"], tq, mask.shape[-1]), lambda s, wl: (0, 0, wl[s], 0))],
        out_specs=pl.BlockSpec((1, tq, n_heads * HEAD_DIM), lambda s, wl: (0, wl[s], 0)),
        scratch_shapes=[pltpu.VMEM((n_heads * tq, w), BF16), pltpu.VMEM((n_heads * tq, LANES), F32),
                        pltpu.VMEM((n_heads * tq, LANES), F32), pltpu.VMEM((n_heads * tq, w), F32),
                        pltpu.VMEM((n_heads * tq, LANES), F32)],
    )
    return pl.pallas_call(
        functools.partial(_flash_list_kernel, cfg=cfg, tq=tq, tk=tk, s_max=s_max),
        grid_spec=grid_spec,
        out_shape=jax.ShapeDtypeStruct((b, tp, n_heads * HEAD_DIM), F32),
        compiler_params=_cparams("arbitrary"),
        name="flash_list",
    )(wl, q_arr, kv_arr, mask)


DECODE_PAGES_PER_STEP = 32


def _decode_update(qs, k_op, v_op, transposed, kbase, qpos_col, slope_col, mrow, m_sc, l_sc, acc_sc, *,
                   blk_shift, window, qpos0):
    s = jnp.dot(qs, _bf(k_op), preferred_element_type=F32) if transposed else _dot_nt(qs, k_op)
    tk = s.shape[1]
    kpos = kbase + lax.broadcasted_iota(jnp.int32, (1, tk), 1)
    d = qpos_col - kpos
    ok = d >= 0
    if window is not None:
        ok = ok & (d < window)
    if mrow is not None:
        nbp = mrow.shape[1]
        expand = jnp.where(lax.broadcasted_iota(jnp.int32, (nbp, tk), 0) == (kpos >> blk_shift), 1.0, 0.0)
        ok = ok & (jnp.dot(mrow, expand.astype(BF16), preferred_element_type=F32) > 0.5)
    s = jnp.where(ok, s + slope_col * (kpos - qpos0).astype(F32), NEG)
    m_old = m_sc[...]
    m_new = jnp.maximum(m_old, jnp.max(s, axis=-1, keepdims=True))
    alpha = jnp.exp2(m_old - m_new)
    p = _bf(jnp.exp2(s - m_new))
    l_sc[...] = alpha * l_sc[...] + jnp.sum(p.astype(F32), axis=-1, keepdims=True)
    m_sc[...] = m_new
    pv = _dot_nt(p, v_op) if transposed else jnp.dot(p, _bf(v_op), preferred_element_type=F32)
    acc_sc[...] = alpha * acc_sc[...] + pv


def _decode_kernel(*refs, cfg, npp, tq, qpos0, kpos0, tile_rows):
    has_mask = bool(cfg["n_mask"])
    page_refs = refs[2:2 + npp]
    q_ref, tail_ref = refs[1], refs[2 + npp]
    rest = refs[3 + npp:]
    if has_mask:
        mask_ref, o_ref, q_sc, m_sc, l_sc, acc_sc, slope_sc, mrow_sc = rest
    else:
        o_ref, q_sc, m_sc, l_sc, acc_sc, slope_sc = rest
    n_heads = cfg["n_heads"]
    w = cfg["n_groups"] * HEAD_DIM
    j = pl.program_id(1)
    n_steps = pl.num_programs(1)
    row = lax.broadcasted_iota(jnp.int32, (n_heads * tq, 1), 0)

    @pl.when(j == 0)
    def _():
        _attn_init(q_ref, q_sc, m_sc, l_sc, acc_sc, n_heads, w)
        slope = jnp.zeros((n_heads * tq, 1), F32)
        for h in range(n_heads):
            slope = jnp.where(row // tq == h, cfg["slopes"][h], slope)
        slope_sc[...] = slope
        if has_mask:
            hpm = n_heads // cfg["n_mask"]
            mrow_sc[...] = _bf(jnp.concatenate([mask_ref[0, h // hpm] for h in range(n_heads)], axis=0))

    qpos_col = qpos0 + row % tq
    kw = dict(blk_shift=cfg["blk_shift"], window=cfg["window"], qpos0=qpos0)
    mrow = mrow_sc[...] if has_mask else None

    @pl.when(j < n_steps - 1)
    def _():
        kt = [r[0, :w, :] for r in page_refs]
        vt = [r[0, w:, :] for r in page_refs]
        kt = kt[0] if npp == 1 else jnp.concatenate(kt, axis=1)
        vt = vt[0] if npp == 1 else jnp.concatenate(vt, axis=1)
        _decode_update(q_sc[...], kt, vt, True, kpos0 + j * (npp * tile_rows), qpos_col, slope_sc[...], mrow,
                       m_sc, l_sc, acc_sc, **kw)

    @pl.when(j == n_steps - 1)
    def _():
        tail = tail_ref[0]
        _decode_update(q_sc[...], tail[:, :w], tail[:, w:], False, qpos0, qpos_col, slope_sc[...], mrow,
                       m_sc, l_sc, acc_sc, **kw)
        o = jnp.where(m_sc[...] > 0.5 * NEG, acc_sc[...] / l_sc[...], 0.0)
        _place_heads([o[h * tq:(h + 1) * tq] for h in range(n_heads)], n_heads, cfg["n_groups"], o_ref)


def _decode_attn(proj3, q_col, tail_col, tiles_t, table, base, mask, cfg, *, qpos0, kpos0):
    b, tq, _ = proj3.shape
    w = cfg["n_groups"] * HEAD_DIM
    n_heads = cfg["n_heads"]
    n_tiles = table.shape[1]
    tile_rows = tiles_t.shape[2]
    npp = math.gcd(DECODE_PAGES_PER_STEP, n_tiles)
    steps = n_tiles // npp
    r = n_heads * tq

    def tile_map(k):
        return lambda bi, j, tb: (base + tb[bi * n_tiles + jnp.minimum(j, steps - 1) * npp + k], 0, 0)

    in_specs = ([pl.BlockSpec((1, tq, n_heads * w), lambda bi, j, tb: (bi, 0, q_col))]
                + [pl.BlockSpec((1, 2 * w, tile_rows), tile_map(k)) for k in range(npp)]
                + [pl.BlockSpec((1, tq, 2 * w), lambda bi, j, tb: (bi, 0, tail_col))])
    args = [table.reshape(-1), proj3] + [tiles_t] * npp + [proj3]
    scratch = [pltpu.VMEM((r, w), BF16), pltpu.VMEM((r, 1), F32), pltpu.VMEM((r, 1), F32), pltpu.VMEM((r, w), F32),
               pltpu.VMEM((r, 1), F32)]
    if cfg["n_mask"]:
        in_specs.append(pl.BlockSpec((1, cfg["n_mask"], tq, mask.shape[-1]), lambda bi, j, tb: (bi, 0, 0, 0)))
        args.append(mask)
        scratch.append(pltpu.VMEM((r, mask.shape[-1]), BF16))
    grid_spec = pltpu.PrefetchScalarGridSpec(
        num_scalar_prefetch=1,
        grid=(b, steps + 1),
        in_specs=in_specs,
        out_specs=pl.BlockSpec((1, tq, n_heads * HEAD_DIM), lambda bi, j, tb: (bi, 0, 0)),
        scratch_shapes=scratch,
    )
    return pl.pallas_call(
        functools.partial(_decode_kernel, cfg=cfg, npp=npp, tq=tq, qpos0=qpos0, kpos0=kpos0, tile_rows=tile_rows),
        grid_spec=grid_spec,
        out_shape=jax.ShapeDtypeStruct((b, tq, n_heads * HEAD_DIM), F32),
        compiler_params=_cparams("parallel", "arbitrary"),
        name="decode_attn",
    )(*args)


def _attn_cfg(kind, rowbias=False):
    if kind == "moba":
        return dict(n_heads=MOBA_HEADS, n_groups=MOBA_KV_HEADS, n_mask=MOBA_HEADS, rowbias=rowbias,
                    blk_shift=int(math.log2(MOBA_BLOCK)), window=None, slopes=SLOPES_B)
    if kind == "sel":
        return dict(n_heads=NSA_HEADS, n_groups=NSA_KV_HEADS, n_mask=NSA_KV_HEADS, rowbias=False,
                    blk_shift=int(math.log2(NSA_SEL_BLOCK)), window=None, slopes=SLOPES_A)
    return dict(n_heads=NSA_HEADS, n_groups=NSA_KV_HEADS, n_mask=0, rowbias=False, blk_shift=0,
                window=NSA_WINDOW, slopes=SLOPES_A)


S5_CHUNK = 4 * LANES
S5_GCHUNK = S5_CHUNK // S5_STATE


def _s5_params(lp):
    lr, li = lp["s5_lambda_re"], lp["s5_lambda_im"]
    dt = jnp.exp(lp["s5_log_dt"])[:, None]
    mag = jnp.exp(lr * dt)
    a_re = mag * jnp.cos(li * dt)
    a_im = mag * jnp.sin(li * dt)
    den = lr * lr + li * li
    f_re = ((a_re - 1.0) * lr + a_im * li) / den
    f_im = (a_im * lr - (a_re - 1.0) * li) / den
    b_re, b_im = lp["s5_b_re"], lp["s5_b_im"]
    bb_re = f_re[..., None] * b_re - f_im[..., None] * b_im
    bb_im = f_re[..., None] * b_im + f_im[..., None] * b_re
    n_ch = S5_GROUPS // S5_GCHUNK
    eye = np.eye(S5_GCHUNK, dtype=np.float32)

    def in_w(bb):
        x = bb.reshape(n_ch, S5_GCHUNK, S5_STATE, S5_GROUP_CH).transpose(0, 1, 3, 2)
        x = x[:, :, :, None, :] * eye[None, :, None, :, None]
        return x.reshape(n_ch, S5_GCHUNK * S5_GROUP_CH, S5_CHUNK)

    def out_w(c):
        x = c.reshape(n_ch, S5_GCHUNK, S5_GROUP_CH, S5_STATE).transpose(0, 1, 3, 2)
        x = x[:, :, :, None, :] * eye[None, :, None, :, None]
        return x.reshape(n_ch, S5_CHUNK, S5_GCHUNK * S5_GROUP_CH).astype(BF16)

    return dict(a_re=a_re.reshape(1, S5_LANES), a_im=a_im.reshape(1, S5_LANES),
                bw_re=in_w(bb_re), bw_im=in_w(bb_im),
                cw_re=out_w(lp["s5_c_re"]), cw_im=out_w(lp["s5_c_im"]),
                d=lp["s5_d"].reshape(1, S5_WIDTH))


def _s5_out_proj(h_re, h_im, cwr_ref, cwi_ref):
    cols = []
    for c in range(S5_LANES // S5_CHUNK):
        sl = slice(c * S5_CHUNK, (c + 1) * S5_CHUNK)
        cols.append(jnp.dot(_bf(h_re[:, sl]), cwr_ref[c], preferred_element_type=F32)
                    - jnp.dot(_bf(h_im[:, sl]), cwi_ref[c], preferred_element_type=F32))
    return jnp.concatenate(cols, axis=1)


def _s5scan_kernel(u_ref, h0r_ref, h0i_ref, ar_ref, ai_ref, bwr_ref, bwi_ref, cwr_ref, cwi_ref, d_ref,
                   y_ref, hr_ref, hi_ref, bur_sc, bui_sc, hr_sc, hi_sc, *, n_par, n_j):
    @pl.when(pl.program_id(0) == 0)
    def _():
        hr_sc[...] = h0r_ref[...]
        hi_sc[...] = h0i_ref[...]

    u = u_ref[...]
    gw = S5_GCHUNK * S5_GROUP_CH
    for c in range(S5_LANES // S5_CHUNK):
        uc = u[:, c * gw:(c + 1) * gw]
        bur_sc[:, c * S5_CHUNK:(c + 1) * S5_CHUNK] = _dot3(uc, bwr_ref[c])
        bui_sc[:, c * S5_CHUNK:(c + 1) * S5_CHUNK] = _dot3(uc, bwi_ref[c])

    for c in range(S5_LANES // S5_CHUNK):
        sl = slice(c * S5_CHUNK, (c + 1) * S5_CHUNK)
        a_re = jnp.broadcast_to(ar_ref[:, sl], (n_par, S5_CHUNK))
        a_im = jnp.broadcast_to(ai_ref[:, sl], (n_par, S5_CHUNK))

        def step(jj, carry):
            h_re, h_im = carry
            rows = pl.ds(pl.multiple_of(jj * n_par, n_par), n_par)
            n_re = a_re * h_re - a_im * h_im + bur_sc[rows, sl]
            n_im = a_re * h_im + a_im * h_re + bui_sc[rows, sl]
            bur_sc[rows, sl] = n_re
            bui_sc[rows, sl] = n_im
            return n_re, n_im

        h_re, h_im = lax.fori_loop(0, n_j, step, (hr_sc[:, sl], hi_sc[:, sl]))
        hr_sc[:, sl] = h_re
        hi_sc[:, sl] = h_im

    y_ref[...] = _s5_out_proj(bur_sc[...], bui_sc[...], cwr_ref, cwi_ref) + d_ref[...] * u
    hr_ref[...] = hr_sc[...]
    hi_ref[...] = hi_sc[...]


def _s5_scan(u_rows, h0_re, h0_im, sp, n_par, jc):
    rows = u_rows.shape[0]
    n_j_total = rows // n_par
    steps = n_j_total // jc
    full = lambda a: pl.BlockSpec(a.shape, lambda i: (0,) * a.ndim)
    ins = [u_rows, h0_re, h0_im, sp["a_re"], sp["a_im"], sp["bw_re"], sp["bw_im"], sp["cw_re"], sp["cw_im"], sp["d"]]
    return pl.pallas_call(
        functools.partial(_s5scan_kernel, n_par=n_par, n_j=jc),
        grid=(steps,),
        in_specs=[pl.BlockSpec((jc * n_par, S5_WIDTH), lambda i: (i, 0))] + [full(a) for a in ins[1:]],
        out_specs=[pl.BlockSpec((jc * n_par, S5_WIDTH), lambda i: (i, 0)),
                   pl.BlockSpec((n_par, S5_LANES), lambda i: (0, 0)),
                   pl.BlockSpec((n_par, S5_LANES), lambda i: (0, 0))],
        out_shape=[jax.ShapeDtypeStruct((rows, S5_WIDTH), F32),
                   jax.ShapeDtypeStruct((n_par, S5_LANES), F32),
                   jax.ShapeDtypeStruct((n_par, S5_LANES), F32)],
        scratch_shapes=[pltpu.VMEM((jc * n_par, S5_LANES), F32), pltpu.VMEM((jc * n_par, S5_LANES), F32),
                        pltpu.VMEM((n_par, S5_LANES), F32), pltpu.VMEM((n_par, S5_LANES), F32)],
        compiler_params=_cparams("arbitrary"),
        name="s5_scan",
    )(*ins)


def _s5fix_kernel(y_ref, er_ref, ei_ref, ar_ref, ai_ref, cwr_ref, cwi_ref, o_ref, fr_ref, fi_ref,
                  cr_sc, ci_sc, pr_sc, pi_sc, xr_sc, xi_sc, *, n_seg, n_j, n_j_total):
    a_re = ar_ref[...]
    a_im = ai_ref[...]

    @pl.when(pl.program_id(0) == 0)
    def _():
        def pw_step(_, carry):
            p_re, p_im = carry
            return a_re * p_re - a_im * p_im, a_re * p_im + a_im * p_re

        al_re, al_im = lax.fori_loop(0, n_j_total - 1, pw_step, (a_re, a_im))
        c_re = jnp.zeros((1, S5_LANES), F32)
        c_im = jnp.zeros((1, S5_LANES), F32)
        for s in range(n_seg):
            cr_sc[s:s + 1, :] = c_re
            ci_sc[s:s + 1, :] = c_im
            e_re, e_im = er_ref[s:s + 1, :], ei_ref[s:s + 1, :]
            c_re, c_im = (e_re + al_re * c_re - al_im * c_im, e_im + al_re * c_im + al_im * c_re)
        fr_ref[...] = c_re
        fi_ref[...] = c_im
        pr_sc[...] = jnp.broadcast_to(a_re, pr_sc.shape)
        pi_sc[...] = jnp.broadcast_to(a_im, pi_sc.shape)

    for c in range(S5_LANES // S5_CHUNK):
        sl = slice(c * S5_CHUNK, (c + 1) * S5_CHUNK)
        ab_re = jnp.broadcast_to(a_re[:, sl], (n_seg, S5_CHUNK))
        ab_im = jnp.broadcast_to(a_im[:, sl], (n_seg, S5_CHUNK))
        c_re, c_im = cr_sc[:, sl], ci_sc[:, sl]

        def step(jj, carry):
            p_re, p_im = carry
            rows = pl.ds(pl.multiple_of(jj * n_seg, n_seg), n_seg)
            xr_sc[rows, sl] = p_re * c_re - p_im * c_im
            xi_sc[rows, sl] = p_re * c_im + p_im * c_re
            return ab_re * p_re - ab_im * p_im, ab_re * p_im + ab_im * p_re

        p_re, p_im = lax.fori_loop(0, n_j, step, (pr_sc[:, sl], pi_sc[:, sl]))
        pr_sc[:, sl] = p_re
        pi_sc[:, sl] = p_im

    o_ref[...] = y_ref[...] + _s5_out_proj(xr_sc[...], xi_sc[...], cwr_ref, cwi_ref)


def _s5_fix(y_rows, end_re, end_im, sp, n_seg, jc):
    rows = y_rows.shape[0]
    n_j_total = rows // n_seg
    steps = n_j_total // jc
    full = lambda a: pl.BlockSpec(a.shape, lambda i: (0,) * a.ndim)
    ins = [y_rows, end_re, end_im, sp["a_re"], sp["a_im"], sp["cw_re"], sp["cw_im"]]
    seg = lambda: pltpu.VMEM((n_seg, S5_LANES), F32)
    return pl.pallas_call(
        functools.partial(_s5fix_kernel, n_seg=n_seg, n_j=jc, n_j_total=n_j_total),
        grid=(steps,),
        in_specs=[pl.BlockSpec((jc * n_seg, S5_WIDTH), lambda i: (i, 0))] + [full(a) for a in ins[1:]],
        out_specs=[pl.BlockSpec((jc * n_seg, S5_WIDTH), lambda i: (i, 0)),
                   pl.BlockSpec((1, S5_LANES), lambda i: (0, 0)),
                   pl.BlockSpec((1, S5_LANES), lambda i: (0, 0))],
        out_shape=[jax.ShapeDtypeStruct((rows, S5_WIDTH), F32),
                   jax.ShapeDtypeStruct((1, S5_LANES), F32),
                   jax.ShapeDtypeStruct((1, S5_LANES), F32)],
        scratch_shapes=[seg(), seg(), seg(), seg(),
                        pltpu.VMEM((jc * n_seg, S5_LANES), F32), pltpu.VMEM((jc * n_seg, S5_LANES), F32)],
        compiler_params=_cparams("arbitrary"),
        name="s5_fix",
    )(*ins)


S5_SEGMENTS = SUBLANES


def _s5_prompt(u, sp):
    t = u.shape[0]
    n_seg = S5_SEGMENTS
    seg_len = t // n_seg
    jc = min(64, seg_len)
    u_rows = u.reshape(n_seg, seg_len, S5_WIDTH).transpose(1, 0, 2).reshape(t, S5_WIDTH)
    zero = jnp.zeros((n_seg, S5_LANES), F32)
    y_loc, end_re, end_im = _s5_scan(u_rows, zero, zero, sp, n_seg, jc)
    y_rows, f_re, f_im = _s5_fix(y_loc, end_re, end_im, sp, n_seg, jc)
    y = y_rows.reshape(seg_len, n_seg, S5_WIDTH).transpose(1, 0, 2).reshape(t, S5_WIDTH)
    return y, f_re, f_im


def _s5_sample(u3, h0_re, h0_im, sp):
    b, t, _ = u3.shape
    u_rows = u3.transpose(1, 0, 2).reshape(t * b, S5_WIDTH)
    y_rows, h_re, h_im = _s5_scan(u_rows, h0_re, h0_im, sp, b, t)
    return y_rows.reshape(t, b, S5_WIDTH).transpose(1, 0, 2), h_re, h_im


def _merge_kernel(x_ref, oc_ref, os_ref, ow_ref, om_ref, y5_ref, ga_ref, gm_ref, wglu_ref, bglu_ref,
                  wn_ref, wm_ref, w5_ref, wo_ref, g_ref, b_ref, o_ref):
    qw = NSA_Q_W
    ga = ga_ref[...]
    o_nsa = (_sigmoid(ga[:, :qw]) * oc_ref[...] + _sigmoid(ga[:, qw:2 * qw]) * os_ref[...]
             + _sigmoid(ga[:, 2 * qw:]) * ow_ref[...])
    z = _gelu(y5_ref[...])
    o_s5 = z * _sigmoid(jnp.dot(_bf(z), wglu_ref[...], preferred_element_type=F32) + bglu_ref[...])
    gm = gm_ref[...]
    d = D_MODEL
    merged = (_sigmoid(gm[:, :d]) * jnp.dot(_bf(o_nsa), wn_ref[...], preferred_element_type=F32)
              + _sigmoid(gm[:, d:2 * d]) * jnp.dot(_bf(om_ref[...]), wm_ref[...], preferred_element_type=F32)
              + _sigmoid(gm[:, 2 * d:]) * jnp.dot(_bf(o_s5), w5_ref[...], preferred_element_type=F32))
    mix = jnp.dot(_bf(merged), wo_ref[...], preferred_element_type=F32)
    o_ref[...] = _layer_norm(DN_ALPHA * x_ref[...] + mix, g_ref[...], b_ref[...])


def _merge(x2, o_c, o_s, o_w, o_m, y5, proj2, lp, tm):
    m = x2.shape[0]
    row = lambda n: pl.BlockSpec((tm, n), lambda i: (i, 0))
    full = lambda a: pl.BlockSpec(a.shape, lambda i: (0,) * a.ndim)
    ws = [lp["s5_w_glu"].astype(BF16), lp["s5_b_glu"].reshape(1, S5_WIDTH),
          lp["w_br_nsa"].astype(BF16), lp["w_br_moba"].astype(BF16), lp["w_br_s5"].astype(BF16),
          lp["w_out"].astype(BF16), lp["ln1_g"].reshape(1, D_MODEL), lp["ln1_b"].reshape(1, D_MODEL)]
    gaw, gmw = 3 * NSA_Q_W, N_BRANCH * D_MODEL
    return pl.pallas_call(
        _merge_kernel,
        grid=(m // tm,),
        in_specs=[row(D_MODEL), row(NSA_Q_W), row(NSA_Q_W), row(NSA_Q_W), row(MOBA_Q_W), row(S5_WIDTH),
                  pl.BlockSpec((tm, gaw), lambda i: (i, PC_GA // gaw)),
                  pl.BlockSpec((tm, gmw), lambda i: (i, PC_GM // gmw))] + [full(a) for a in ws],
        out_specs=row(D_MODEL),
        out_shape=jax.ShapeDtypeStruct((m, D_MODEL), F32),
        compiler_params=_cparams("parallel"),
        name="merge",
    )(x2, o_c, o_s, o_w, o_m, y5, proj2, proj2, *ws)


def _router_kernel(x_ref, w_ref, b_ref, o_ref):
    logits = _dot3(x_ref[...], w_ref[...]) + b_ref[...]
    lane = lax.broadcasted_iota(jnp.int32, logits.shape, 1)
    lanef = lane.astype(F32)
    logits = jnp.where(lane < N_EXPERTS, logits, -jnp.inf)
    sel = _top_select(logits, lanef, MOE_TOP_K)
    m = jnp.max(logits, axis=-1, keepdims=True)
    e = jnp.where(sel > 0.5, jnp.exp(logits - m), 0.0)
    o_ref[...] = e / jnp.sum(e, axis=-1, keepdims=True)


def _router(x2, w_router, b_router, tm):
    m = x2.shape[0]
    w = jnp.pad(w_router, ((0, 0), (0, LANES - N_EXPERTS)))
    bias = jnp.pad(b_router, (0, LANES - N_EXPERTS)).reshape(1, LANES)
    return pl.pallas_call(
        _router_kernel,
        grid=(m // tm,),
        in_specs=[pl.BlockSpec((tm, D_MODEL), lambda i: (i, 0)),
                  pl.BlockSpec(w.shape, lambda i: (0, 0)), pl.BlockSpec(bias.shape, lambda i: (0, 0))],
        out_specs=pl.BlockSpec((tm, LANES), lambda i: (i, 0)),
        out_shape=jax.ShapeDtypeStruct((m, LANES), F32),
        compiler_params=_cparams("parallel"),
        name="router",
    )(x2, w, bias)


def _ffn_kernel(*refs, use_comb):
    if use_comb:
        x_ref, comb_ref, wg_ref, wu_ref, wd_ref, g_ref, b_ref, o_ref, acc_sc = refs
    else:
        x_ref, wg_ref, wu_ref, wd_ref, g_ref, b_ref, o_ref, acc_sc = refs
    f = pl.program_id(1)

    @pl.when(f == 0)
    def _():
        acc_sc[...] = jnp.zeros(acc_sc.shape, F32)

    xb = _bf(x_ref[...])
    wg = wg_ref[0] if use_comb else wg_ref[...]
    wu = wu_ref[0] if use_comb else wu_ref[...]
    wd = wd_ref[0] if use_comb else wd_ref[...]
    gate = jnp.dot(xb, _bf(wg), preferred_element_type=F32)
    up = jnp.dot(xb, _bf(wu), preferred_element_type=F32)
    h = gate * _sigmoid(gate) * up
    if use_comb:
        comb = comb_ref[...]
        lane = lax.broadcasted_iota(jnp.int32, comb.shape, 1)
        h = h * jnp.sum(jnp.where(lane == f, comb, 0.0), axis=-1, keepdims=True)
    acc_sc[...] += jnp.dot(_bf(h), _bf(wd), preferred_element_type=F32)

    @pl.when(f == pl.num_programs(1) - 1)
    def _():
        o_ref[...] = _layer_norm(DN_ALPHA * x_ref[...] + acc_sc[...], g_ref[...], b_ref[...])


def _ffn(x2, comb, wg, wu, wd, ln_g, ln_b, tm, tf):
    m = x2.shape[0]
    use_comb = comb is not None
    if use_comb:
        n_f = wg.shape[0]
        w_specs = [pl.BlockSpec((1,) + wg.shape[1:], lambda i, f: (f, 0, 0)),
                   pl.BlockSpec((1,) + wu.shape[1:], lambda i, f: (f, 0, 0)),
                   pl.BlockSpec((1,) + wd.shape[1:], lambda i, f: (f, 0, 0))]
    else:
        n_f = wg.shape[1] // tf
        w_specs = [pl.BlockSpec((D_MODEL, tf), lambda i, f: (0, f)),
                   pl.BlockSpec((D_MODEL, tf), lambda i, f: (0, f)),
                   pl.BlockSpec((tf, D_MODEL), lambda i, f: (f, 0))]
    vec = pl.BlockSpec((1, D_MODEL), lambda i, f: (0, 0))
    in_specs = [pl.BlockSpec((tm, D_MODEL), lambda i, f: (i, 0))]
    args = [x2]
    if use_comb:
        in_specs.append(pl.BlockSpec((tm, LANES), lambda i, f: (i, 0)))
        args.append(comb)
    return pl.pallas_call(
        functools.partial(_ffn_kernel, use_comb=use_comb),
        grid=(m // tm, n_f),
        in_specs=in_specs + w_specs + [vec, vec],
        out_specs=pl.BlockSpec((tm, D_MODEL), lambda i, f: (i, 0)),
        out_shape=jax.ShapeDtypeStruct((m, D_MODEL), F32),
        scratch_shapes=[pltpu.VMEM((tm, D_MODEL), F32)],
        compiler_params=_cparams("parallel", "arbitrary"),
        name="ffn",
    )(*args, wg.astype(BF16), wu.astype(BF16), wd.astype(BF16), ln_g.reshape(1, D_MODEL), ln_b.reshape(1, D_MODEL))


def _token_mixer(x3, t_real, qpos0, past, lp, sp, h0_re, h0_im):
    b, tp, _ = x3.shape
    rows = b * tp
    x2 = x3.reshape(rows, D_MODEL)
    proj2 = _matmul(x2, lp["w_proj"], min(256, rows), PC_END // 2)
    proj3 = proj2.reshape(b, tp, PC_END)
    kva = proj3[:, :t_real, PC_KVA:PC_END]
    new_cmp, new_sel, new_win = (kva[:, :, i * 2 * NSA_KV_W:(i + 1) * 2 * NSA_KV_W] for i in range(3))
    new_moba = proj3[:, :t_real, PC_KVB:PC_KVA]
    cw = _cmp_weights(lp["phi1_k"], lp["phi1_v"])
    cmp_args = (lp["pe_k"], lp["pe_v"], lp["phi1_k"], lp["phi1_v"], lp["phi2_k"], lp["phi2_v"])
    rw_a, rw_b = 2 * NSA_KV_W, 2 * MOBA_KV_W

    qa_col = PC_QA // (NSA_HEADS * NSA_KV_W)
    qb_col = PC_QB // (MOBA_HEADS * MOBA_KV_W)
    if past is None:
        assert qpos0 == 0 and b == 1 and t_real == tp and t_real % MOBA_BLOCK == 0
        tq_sel = 256 if tp % 256 == 0 else 128
        tq_moba = 256 if tp % 256 == 0 else 128
        tq = 256 if tp % 256 == 0 else 128
        tk = 512 if tp % 512 == 0 else tp
        ab = _cmp_ab_linear(new_cmp.reshape(t_real // NSA_CMP_STRIDE, NSA_CMP_STRIDE * rw_a), cw)
        kc = _cmp_finish(ab, 1, *cmp_args)
        n_kt = tp // tk
        assert n_kt <= LANES
        o_c, sel_mask, act = _cmp_attention(proj3, kc, 0, t_real, tq_sel, act_blocks=tk // NSA_SEL_BLOCK)
        act = (act[0, :, 0, :n_kt] > 0.5).reshape(tp // tq, tq // tq_sel, n_kt).any(axis=1)
        o_s = _flash_worklist(proj3, qa_col, proj3, (PC_KVA + rw_a) // rw_a, sel_mask, act, _attn_cfg("sel"),
                              tq=tq, tk=tk)
        n_win = min(n_kt, (NSA_WINDOW + tq - 2) // tk + 2)
        o_w = _flash_linear(proj3, qa_col, proj3, (PC_KVA + 2 * rw_a) // rw_a, None,
                            _attn_cfg("win"), tq=tq, tk=tk, qpos0=0, kpos0=0, n_steps=n_win)
        kmean = _moba_kmean(proj3, None)
        mb_bias = _moba_gate(proj3, kmean, 0, t_real, tq_sel, True)
        v_t = proj3[:, :, PC_KVB + MOBA_KV_W:PC_KVA].transpose(0, 2, 1)
        o_m = _flash_transposed(proj3, qb_col, proj3, PC_KVB // MOBA_KV_W, v_t, mb_bias,
                                _attn_cfg("moba", rowbias=True), tq=tq_moba, tk=MOBA_BLOCK,
                                n_steps=tp // MOBA_BLOCK)
        y5, h_re, h_im = _s5_prompt(proj2[:, PC_U:PC_U + S5_WIDTH], sp)
        keep = min(NSA_WINDOW, t_real)
        win_state = new_win[:, t_real - keep:]
    else:
        assert t_real < NSA_CMP_STRIDE and qpos0 % MOBA_BLOCK == 0 and tp % SUBLANES == 0
        pt, base = past["page_table"], past["page_base"]
        ab = _cmp_ab_paged(past["cmp_t"], base, pt, cw)
        kc = _cmp_finish(ab, b, *cmp_args)
        o_c, sel_mask, _ = _cmp_attention(proj3, kc, qpos0, t_real, tp)
        o_s = _decode_attn(proj3, qa_col, (PC_KVA + rw_a) // rw_a, past["sel_t"], pt, base, sel_mask,
                           _attn_cfg("sel"), qpos0=qpos0, kpos0=0)
        wb = past["win_t"].shape[2]
        o_w = _decode_attn(proj3, qa_col, (PC_KVA + 2 * rw_a) // rw_a, past["win_t"],
                           jnp.arange(b, dtype=jnp.int32).reshape(b, 1), past["win_base"], None,
                           _attn_cfg("win"), qpos0=qpos0, kpos0=qpos0 - wb)
        kmean = _moba_kmean(past["moba_t"], pt, base)
        mb_mask = _moba_gate(proj3, kmean, qpos0, t_real, tp, False)
        o_m = _decode_attn(proj3, qb_col, PC_KVB // rw_b, past["moba_t"], pt, base, mb_mask,
                           _attn_cfg("moba"), qpos0=qpos0, kpos0=0)
        y5r, h_re, h_im = _s5_sample(proj3[:, :t_real, PC_U:PC_U + S5_WIDTH], h0_re, h0_im, sp)
        y5 = jnp.pad(y5r, ((0, 0), (0, tp - t_real), (0, 0))).reshape(rows, S5_WIDTH)
        win_state = jnp.concatenate([past["win_rows"], new_win], axis=1)[:, -wb:]

    flat = lambda a: a.reshape(rows, a.shape[-1])
    x1 = _merge(x2, flat(o_c), flat(o_s), flat(o_w), flat(o_m), y5, proj2, lp, min(256, rows))
    return x1, (new_cmp, new_sel, win_state, new_moba, h_re, h_im)


def _layer(l, x3, t_real, qpos0, past, h0_re, h0_im, lp, sp, ffn_p):
    b, tp, _ = x3.shape
    x1, st = _token_mixer(x3, t_real, qpos0, past, lp, sp, h0_re, h0_im)
    tm = min(512, x1.shape[0])
    if l % 2 == 0:
        x2 = _ffn(x1, None, ffn_p["wg"], ffn_p["wu"], ffn_p["wd"], lp["ln2_g"], lp["ln2_b"], tm, ffn_p["tf"])
    else:
        comb = _router(x1, ffn_p["w_router"], ffn_p["b_router"], tm)
        x2 = _ffn(x1, comb, ffn_p["wg"], ffn_p["wu"], ffn_p["wd"], lp["ln2_g"], lp["ln2_b"], tm, None)
    return x2.reshape(b, tp, D_MODEL), st


def kernel(x_prompt, x_sample, cache_nsa_cmp, cache_nsa_sel, cache_nsa_win, cache_moba, state_s5_re, state_s5_im,
           page_table, w_in, pe_k, phi1_k, phi2_k, pe_v, phi1_v, phi2_v, s5_lambda_re, s5_lambda_im, s5_log_dt,
           s5_b_re, s5_b_im, s5_c_re, s5_c_im, s5_d, s5_w_glu, s5_b_glu, w_br_nsa, w_br_moba, w_br_s5, w_out,
           ln1_g, ln1_b, ln2_g, ln2_b, ffn_w_gate, ffn_w_up, ffn_w_down, moe_w_router, moe_b_router,
           moe_w_gate, moe_w_up, moe_w_down):
    depth = w_in.shape[0]
    bp, seq, _ = x_prompt.shape
    bs, dec_seq, _ = x_sample.shape
    n_pool = cache_moba.shape[1]
    past_len = page_table.shape[1] * PAGE_SIZE
    tps = _round_up(dec_seq, SUBLANES)
    y_p = x_prompt
    y_s = jnp.pad(x_sample, ((0, 0), (0, tps - dec_seq), (0, 0)))
    rw_a, rw_b = 2 * NSA_KV_W, 2 * MOBA_KV_W

    def tiles_t(cache):
        d, n, r = cache.shape[:3]
        return cache.transpose(0, 1, 3, 4, 5, 2).reshape(d * n, -1, r)

    cmp_t, sel_t, moba_t, win_t = (tiles_t(c) for c in (cache_nsa_cmp, cache_nsa_sel, cache_moba, cache_nsa_win))
    p_st, s_st = [], []
    for l in range(depth):
        lp = dict(w_proj=_proj_weight(w_in[l]), pe_k=pe_k[l], phi1_k=phi1_k[l], phi2_k=phi2_k[l], pe_v=pe_v[l],
                  phi1_v=phi1_v[l], phi2_v=phi2_v[l], s5_lambda_re=s5_lambda_re[l], s5_lambda_im=s5_lambda_im[l],
                  s5_log_dt=s5_log_dt[l], s5_b_re=s5_b_re[l], s5_b_im=s5_b_im[l], s5_c_re=s5_c_re[l],
                  s5_c_im=s5_c_im[l], s5_d=s5_d[l], s5_w_glu=s5_w_glu[l], s5_b_glu=s5_b_glu[l],
                  w_br_nsa=w_br_nsa[l], w_br_moba=w_br_moba[l], w_br_s5=w_br_s5[l], w_out=w_out[l],
                  ln1_g=ln1_g[l], ln1_b=ln1_b[l], ln2_g=ln2_g[l], ln2_b=ln2_b[l])
        sp = _s5_params(lp)
        if l % 2 == 0:
            d_ff = ffn_w_gate.shape[2]
            tf = d_ff // 2 if (d_ff // 2) % LANES == 0 else d_ff
            ffn_p = dict(wg=ffn_w_gate[l // 2], wu=ffn_w_up[l // 2], wd=ffn_w_down[l // 2], tf=tf)
        else:
            ffn_p = dict(wg=moe_w_gate[l // 2], wu=moe_w_up[l // 2], wd=moe_w_down[l // 2],
                         w_router=moe_w_router[l // 2], b_router=moe_b_router[l // 2])
        y_p, st = _layer(l, y_p, seq, 0, None, None, None, lp, sp, ffn_p)
        p_st.append(st)
        past = dict(page_table=page_table, page_base=l * n_pool, cmp_t=cmp_t, sel_t=sel_t, moba_t=moba_t,
                    win_t=win_t, win_base=l * bs, win_rows=cache_nsa_win[l].reshape(bs, -1, rw_a))
        y_s, st = _layer(l, y_s, dec_seq, past_len, past, state_s5_re[l].reshape(bs, S5_LANES),
                         state_s5_im[l].reshape(bs, S5_LANES), lp, sp, ffn_p)
        s_st.append(st)

    def stk(states, j, tail):
        return jnp.stack([st[j].reshape(st[j].shape[:2] + tail) if tail else st[j] for st in states], axis=0)

    row_a = (2, NSA_KV_HEADS, HEAD_DIM)
    row_b = (2, MOBA_KV_HEADS, HEAD_DIM)

    def s5_state(states, j, b):
        return jnp.stack([st[j].reshape(b, S5_GROUPS, S5_STATE) for st in states], axis=0)

    return (y_p, y_s[:, :dec_seq],
            stk(p_st, 0, row_a), stk(p_st, 1, row_a), stk(p_st, 2, row_a), stk(p_st, 3, row_b),
            s5_state(p_st, 4, bp), s5_state(p_st, 5, bp),
            stk(s_st, 0, row_a), stk(s_st, 1, row_a), stk(s_st, 2, row_a), stk(s_st, 3, row_b),
            s5_state(s_st, 4, bs), s5_state(s_st, 5, bs))
```

```python
import functools
import math

import numpy as np
import jax
import jax.numpy as jnp
from jax import lax
from jax.experimental import pallas as pl
from jax.experimental.pallas import tpu as pltpu

F32 = jnp.float32
BF16 = jnp.bfloat16

D_MODEL = 1024
HEAD_DIM = 64
PAGE_SIZE = 128
NSA_HEADS = 8
NSA_KV_HEADS = 2
NSA_CMP_LEN = 32
NSA_CMP_STRIDE = 16
NSA_PHI_HIDDEN = 128
NSA_SEL_BLOCK = 64
NSA_SEL_TOPK = 16
NSA_N_LOCAL = 2
NSA_WINDOW = 512
MOBA_HEADS = 8
MOBA_KV_HEADS = 4
MOBA_BLOCK = 256
MOBA_TOPK = 3
S5_GROUPS = 32
S5_GROUP_CH = 16
S5_WIDTH = S5_GROUPS * S5_GROUP_CH
S5_STATE = 64
S5_LANES = S5_GROUPS * S5_STATE
N_BRANCH = 3
N_EXPERTS = 8
MOE_TOP_K = 2
LN_EPS = 1e-5
DEPTH = 2
DN_ALPHA = (2.0 * DEPTH) ** 0.25

NSA_Q_W = NSA_HEADS * HEAD_DIM
NSA_KV_W = NSA_KV_HEADS * HEAD_DIM
MOBA_Q_W = MOBA_HEADS * HEAD_DIM
MOBA_KV_W = MOBA_KV_HEADS * HEAD_DIM
IN_SPLITS = (NSA_Q_W, 6 * NSA_KV_W, 3 * NSA_HEADS, MOBA_Q_W, 2 * MOBA_KV_W, S5_WIDTH, N_BRANCH * D_MODEL)
IN_OFFS = tuple(int(v) for v in np.cumsum((0,) + IN_SPLITS))

LANES = 128
SUBLANES = 8
VMEM_LIMIT = 56 * 1024 * 1024

PC_QB = 0
PC_QA = PC_QB + MOBA_HEADS * MOBA_KV_W
PC_GM = PC_QA + NSA_HEADS * NSA_KV_W
PC_GA = PC_GM + N_BRANCH * D_MODEL
PC_U = PC_GA + 3 * NSA_Q_W
PC_KVB = PC_U + S5_WIDTH
PC_KVA = PC_KVB + 2 * MOBA_KV_W
PC_END = PC_KVA + 6 * NSA_KV_W

NEG = -1e30
LOG2E = 1.0 / math.log(2.0)
SLOPES_A = tuple(float(2.0 ** (-8.0 * (i + 1) / NSA_HEADS)) * LOG2E for i in range(NSA_HEADS))
SLOPES_B = tuple(float(2.0 ** (-8.0 * (i + 1) / MOBA_HEADS)) * LOG2E for i in range(MOBA_HEADS))


def _cparams(*sem):
    return pltpu.CompilerParams(dimension_semantics=sem, vmem_limit_bytes=VMEM_LIMIT)


def _round_up(n, m):
    return -(-n // m) * m


def _bf(x):
    return x.astype(BF16)


def _dot(a, b):
    return jnp.dot(_bf(a), _bf(b), preferred_element_type=F32)


def _dot_nt(a, b):
    return lax.dot_general(_bf(a), _bf(b), (((1,), (1,)), ((), ())), preferred_element_type=F32)


def _split(a):
    hi = a.astype(BF16)
    lo = (a - hi.astype(F32)).astype(BF16)
    return hi, lo


def _dot3(a, b):
    ah, al = _split(a)
    bh, bl = _split(b)
    d = functools.partial(jnp.dot, preferred_element_type=F32)
    return d(ah, bh) + (d(ah, bl) + d(al, bh))


def _dot3_nt(a, b):
    ah, al = _split(a)
    bh, bl = _split(b)
    d = functools.partial(lax.dot_general, dimension_numbers=(((1,), (1,)), ((), ())),
                          preferred_element_type=F32)
    return d(ah, bh) + (d(ah, bl) + d(al, bh))


def _dot2_exact_rhs(a, b_bf16):
    ah, al = _split(a)
    d = functools.partial(jnp.dot, preferred_element_type=F32)
    return d(ah, b_bf16) + d(al, b_bf16)


def _sigmoid(x):
    return 1.0 / (1.0 + jnp.exp(-x))


def _gelu(x):
    c = math.sqrt(2.0 / math.pi)
    return x * (0.5 * (1.0 + jnp.tanh(c * (x + 0.044715 * (x * x * x)))))


def _layer_norm(v, g, b):
    mu = jnp.mean(v, axis=-1, keepdims=True)
    vc = v - mu
    var = jnp.mean(vc * vc, axis=-1, keepdims=True)
    return vc * lax.rsqrt(var + LN_EPS) * g + b


def _mm_kernel(x_ref, w_ref, o_ref):
    o_ref[...] = jnp.dot(_bf(x_ref[...]), w_ref[...], preferred_element_type=F32)


def _matmul(x, w, tm, tn):
    m, k = x.shape
    n = w.shape[1]
    return pl.pallas_call(
        _mm_kernel,
        grid=(n // tn, m // tm),
        in_specs=[pl.BlockSpec((tm, k), lambda j, i: (i, 0)),
                  pl.BlockSpec((k, tn), lambda j, i: (0, j))],
        out_specs=pl.BlockSpec((tm, tn), lambda j, i: (i, j)),
        out_shape=jax.ShapeDtypeStruct((m, n), F32),
        compiler_params=_cparams("parallel", "arbitrary"),
        name="proj",
    )(x, w)


def _proj_weight(w_in):
    scale = HEAD_DIM ** -0.5 * LOG2E
    qa = w_in[:, IN_OFFS[0]:IN_OFFS[1]].reshape(D_MODEL, NSA_HEADS, 1, HEAD_DIM) * scale
    oh_a = np.zeros((NSA_HEADS, NSA_KV_HEADS, 1), np.float32)
    for h in range(NSA_HEADS):
        oh_a[h, h // (NSA_HEADS // NSA_KV_HEADS)] = 1.0
    qa = (qa * oh_a[None]).reshape(D_MODEL, NSA_HEADS * NSA_KV_W)
    qb = w_in[:, IN_OFFS[3]:IN_OFFS[4]].reshape(D_MODEL, MOBA_HEADS, 1, HEAD_DIM) * scale
    oh_b = np.zeros((MOBA_HEADS, MOBA_KV_HEADS, 1), np.float32)
    for h in range(MOBA_HEADS):
        oh_b[h, h // (MOBA_HEADS // MOBA_KV_HEADS)] = 1.0
    qb = (qb * oh_b[None]).reshape(D_MODEL, MOBA_HEADS * MOBA_KV_W)
    ga = w_in[:, IN_OFFS[2]:IN_OFFS[3]].reshape(D_MODEL, NSA_HEADS, 3).transpose(0, 2, 1)
    ga = jnp.broadcast_to(ga[..., None], (D_MODEL, 3, NSA_HEADS, HEAD_DIM)).reshape(D_MODEL, 3 * NSA_Q_W)
    gm = w_in[:, IN_OFFS[6]:IN_OFFS[7]]
    u = w_in[:, IN_OFFS[5]:IN_OFFS[6]]
    kvb = w_in[:, IN_OFFS[4]:IN_OFFS[5]]
    kva = w_in[:, IN_OFFS[1]:IN_OFFS[2]]
    return jnp.concatenate([qb, qa, gm, ga, u, kvb, kva], axis=1).astype(BF16)


def _cmpab_kernel(x_ref, w_ref, o_ref):
    row_w = 2 * NSA_KV_W
    x = x_ref[...]
    for kv in range(2):
        xk = jnp.concatenate(
            [x[:, r * row_w + kv * NSA_KV_W: r * row_w + (kv + 1) * NSA_KV_W] for r in range(NSA_CMP_STRIDE)],
            axis=1)
        o_ref[:, kv * 4 * NSA_PHI_HIDDEN:(kv + 1) * 4 * NSA_PHI_HIDDEN] = _dot(xk, w_ref[kv])


def _cmp_weights(phi1_k, phi1_v):
    eye = np.eye(NSA_KV_HEADS, dtype=np.float32)
    out = []
    for phi in (phi1_k, phi1_v):
        halves = []
        for half in range(NSA_CMP_LEN // NSA_CMP_STRIDE):
            w = phi[half * NSA_CMP_STRIDE * HEAD_DIM:(half + 1) * NSA_CMP_STRIDE * HEAD_DIM]
            w = w.reshape(NSA_CMP_STRIDE, 1, HEAD_DIM, 1, NSA_PHI_HIDDEN)
            w = w * eye[None, :, None, :, None]
            halves.append(w.reshape(NSA_CMP_STRIDE * NSA_KV_W, NSA_KV_HEADS * NSA_PHI_HIDDEN))
        out.append(jnp.concatenate(halves, axis=1))
    return jnp.stack(out).astype(BF16)


def _cmp_ab_linear(x, w):
    mc = x.shape[0]
    tmc = min(mc, 256)
    n_out = 8 * NSA_PHI_HIDDEN
    return pl.pallas_call(
        _cmpab_kernel,
        grid=(mc // tmc,),
        in_specs=[pl.BlockSpec((tmc, x.shape[1]), lambda i: (i, 0)),
                  pl.BlockSpec(w.shape, lambda i: (0, 0, 0))],
        out_specs=pl.BlockSpec((tmc, n_out), lambda i: (i, 0)),
        out_shape=jax.ShapeDtypeStruct((mc, n_out), F32),
        compiler_params=_cparams("parallel"),
        name="cmp_ab",
    )(x, w)


CMP_PAGES_PER_STEP = 16


def _cmpab_t_kernel(*refs, npp):
    page_refs, w_ref, o_ref, pk_sc, pv_sc = refs[1:1 + npp], refs[1 + npp], refs[2 + npp], refs[3 + npp], refs[4 + npp]
    w = NSA_KV_W
    p = PAGE_SIZE
    eye = jnp.where(lax.broadcasted_iota(jnp.int32, (p, p), 0) == lax.broadcasted_iota(jnp.int32, (p, p), 1),
                    1.0, 0.0).astype(BF16)
    for k in range(npp):
        page_t = page_refs[k][0]
        pk_sc[k * p:(k + 1) * p, :] = _dot_nt(eye, page_t[:w])
        pv_sc[k * p:(k + 1) * p, :] = _dot_nt(eye, page_t[w:])
    m = npp * (p // NSA_CMP_STRIDE)
    n_half = 4 * NSA_PHI_HIDDEN
    for kv, sc in ((0, pk_sc), (1, pv_sc)):
        acc = None
        for r in range(0, NSA_CMP_STRIDE, 2):
            x = jnp.concatenate([sc[pl.ds(r, m, stride=NSA_CMP_STRIDE), :],
                                 sc[pl.ds(r + 1, m, stride=NSA_CMP_STRIDE), :]], axis=1)
            part = _dot(x, w_ref[kv, r // 2])
            acc = part if acc is None else acc + part
        o_ref[:, kv * n_half:(kv + 1) * n_half] = acc


def _cmp_ab_paged(pool_t, base, page_table, w):
    b, n_pages = page_table.shape
    cpp = PAGE_SIZE // NSA_CMP_STRIDE
    npp = math.gcd(CMP_PAGES_PER_STEP, n_pages)
    steps = n_pages // npp
    n_out = 8 * NSA_PHI_HIDDEN
    w4 = w.reshape(2, NSA_CMP_STRIDE // 2, 2 * NSA_KV_W, w.shape[-1])

    def x_map(k):
        return lambda bi, s, pt: (base + pt[bi * n_pages + s * npp + k], 0, 0)

    grid_spec = pltpu.PrefetchScalarGridSpec(
        num_scalar_prefetch=1,
        grid=(b, steps),
        in_specs=[pl.BlockSpec((1,) + pool_t.shape[1:], x_map(k)) for k in range(npp)]
        + [pl.BlockSpec(w4.shape, lambda bi, s, pt: (0, 0, 0, 0))],
        out_specs=pl.BlockSpec((npp * cpp, n_out), lambda bi, s, pt: (bi * steps + s, 0)),
        scratch_shapes=[pltpu.VMEM((npp * PAGE_SIZE, NSA_KV_W), F32), pltpu.VMEM((npp * PAGE_SIZE, NSA_KV_W), F32)],
    )
    return pl.pallas_call(
        functools.partial(_cmpab_t_kernel, npp=npp),
        grid_spec=grid_spec,
        out_shape=jax.ShapeDtypeStruct((b * n_pages * cpp, n_out), F32),
        compiler_params=_cparams("parallel", "arbitrary"),
        name="cmp_ab_paged",
    )(page_table.reshape(-1), *([pool_t] * npp), w4)


def _cmpfin_kernel(ab_ref, pe_ref, w1_ref, w2_ref, o_ref):
    mc = ab_ref.shape[0]
    hid2 = NSA_KV_HEADS * NSA_PHI_HIDDEN
    for kv in range(2):
        c = _dot(pe_ref[kv], w1_ref[kv])[0:1]
        c = jnp.concatenate([c] * NSA_KV_HEADS, axis=1)
        a = ab_ref[:, kv * 2 * hid2: kv * 2 * hid2 + hid2]
        bnext = pltpu.roll(ab_ref[:, kv * 2 * hid2 + hid2:(kv + 1) * 2 * hid2], mc - 1, 0)
        hpre = a + bnext + c
        o_ref[:, kv * NSA_KV_W:(kv + 1) * NSA_KV_W] = _dot(_gelu(hpre), w2_ref[kv])


def _cmp_finish(ab, nb, pe_k, pe_v, phi1_k, phi1_v, phi2_k, phi2_v):
    mc = ab.shape[0] // nb
    flat = NSA_CMP_LEN * HEAD_DIM
    pe = jnp.stack([jnp.broadcast_to(p.reshape(1, flat), (SUBLANES, flat)) for p in (pe_k, pe_v)])
    w1 = jnp.stack([phi1_k, phi1_v]).astype(BF16)
    eye = np.eye(NSA_KV_HEADS, dtype=np.float32)
    w2 = jnp.stack([(p[None, :, None, :] * eye[:, None, :, None]).reshape(NSA_KV_HEADS * NSA_PHI_HIDDEN, NSA_KV_W)
                    for p in (phi2_k, phi2_v)]).astype(BF16)
    out = pl.pallas_call(
        _cmpfin_kernel,
        grid=(nb,),
        in_specs=[pl.BlockSpec((mc, ab.shape[1]), lambda i: (i, 0)),
                  pl.BlockSpec(pe.shape, lambda i: (0, 0, 0)),
                  pl.BlockSpec(w1.shape, lambda i: (0, 0, 0)),
                  pl.BlockSpec(w2.shape, lambda i: (0, 0, 0))],
        out_specs=pl.BlockSpec((mc, 2 * NSA_KV_W), lambda i: (i, 0)),
        out_shape=jax.ShapeDtypeStruct((nb * mc, 2 * NSA_KV_W), F32),
        compiler_params=_cparams("parallel"),
        name="cmp_finish",
    )(ab, pe, w1, w2)
    return out.reshape(nb, mc, 2 * NSA_KV_W)


def _stack_heads(q, n_heads, w):
    return jnp.concatenate([_bf(q[:, h * w:(h + 1) * w]) for h in range(n_heads)], axis=0)


def _place_heads(o_heads, n_heads, n_groups, o_ref):
    hpg = n_heads // n_groups
    tq = o_heads[0].shape[0]
    lane = lax.broadcasted_iota(jnp.int32, (tq, LANES), 1)
    for pair in range(n_heads // 2):
        pieces = []
        for h in (2 * pair, 2 * pair + 1):
            src = (h // hpg) * HEAD_DIM
            piece = o_heads[h][:, (src // LANES) * LANES:(src // LANES + 1) * LANES]
            if src % LANES != (h % 2) * HEAD_DIM:
                piece = pltpu.roll(piece, HEAD_DIM, 1)
            pieces.append(piece)
        o_ref[0, :, pair * LANES:(pair + 1) * LANES] = jnp.where(lane < HEAD_DIM, pieces[0], pieces[1])


def _top_select(score, lanef, n_pick):
    sel = jnp.zeros(score.shape, F32)
    for _ in range(n_pick):
        m = jnp.max(score, axis=-1, keepdims=True)
        idx = jnp.min(jnp.where(score == m, lanef, 1e9), axis=-1, keepdims=True)
        pick = lanef == idx
        sel = jnp.where(pick & (m > -jnp.inf), 1.0, sel)
        score = jnp.where(pick, -jnp.inf, score)
    return sel


def _cmpattn_kernel(q_ref, kc_ref, ov_ref, o_ref, sel_ref, act_ref, *, tq, qpos0, n_cmp, n_sel_blk, act_blocks):
    w = NSA_KV_W
    hpg = NSA_HEADS // NSA_KV_HEADS
    ncp = kc_ref.shape[1]
    nbp = ov_ref.shape[1]
    qbase = qpos0 + pl.program_id(1) * tq
    qs = _stack_heads(q_ref[0], NSA_HEADS, w)
    kc = kc_ref[0]
    s_all = _dot_nt(qs, kc[:, :w])
    c_idx = lax.broadcasted_iota(jnp.int32, (1, ncp), 1)
    row = lax.broadcasted_iota(jnp.int32, (tq, 1), 0)
    cend_rel = c_idx * NSA_CMP_STRIDE + (NSA_CMP_LEN - 1) - qbase
    valid = (row - cend_rel >= 0) & (c_idx < n_cmp)
    colf = cend_rel.astype(F32)
    ps, psums = [], []
    for h in range(NSA_HEADS):
        s = jnp.where(valid, s_all[h * tq:(h + 1) * tq] + SLOPES_A[h] * colf, NEG)
        m = jnp.max(s, axis=-1, keepdims=True)
        e = jnp.where(valid, jnp.exp2(s - m), 0.0)
        p = e / jnp.maximum(jnp.sum(e, axis=-1, keepdims=True), 1e-30)
        ps.append(_bf(p))
        if h % hpg == 0:
            psums.append(p)
        else:
            psums[-1] = psums[-1] + p
    o_all = jnp.dot(jnp.concatenate(ps, axis=0), _bf(kc[:, w:]), preferred_element_type=F32)
    _place_heads([o_all[h * tq:(h + 1) * tq] for h in range(NSA_HEADS)], NSA_HEADS, NSA_KV_HEADS, o_ref)

    lane = lax.broadcasted_iota(jnp.int32, (tq, nbp), 1)
    lanef = lane.astype(F32)
    cur = (qbase + row) // NSA_SEL_BLOCK
    forced = (lane == 0) | ((lane <= cur) & (lane > cur - NSA_N_LOCAL))
    validb = (lane <= cur) & (lane < n_sel_blk)
    n_top = min(NSA_SEL_TOPK, n_sel_blk)
    scores = [jnp.where(forced, jnp.inf, jnp.where(validb, _dot2_exact_rhs(psums[g], ov_ref[...]), -jnp.inf))
              for g in range(NSA_KV_HEADS)]
    sel_all = _top_select(jnp.concatenate(scores, axis=0), jnp.concatenate([lanef] * NSA_KV_HEADS, axis=0), n_top)
    for g in range(NSA_KV_HEADS):
        sel_ref[0, g] = sel_all[g * tq:(g + 1) * tq]
    any_blk = jnp.max(sel_all, axis=0, keepdims=True)
    pool = jnp.where((lax.broadcasted_iota(jnp.int32, (nbp, LANES), 0) // act_blocks)
                     == lax.broadcasted_iota(jnp.int32, (nbp, LANES), 1), 1.0, 0.0).astype(BF16)
    act_ref[0, 0] = jnp.dot(_bf(jnp.broadcast_to(any_blk, (SUBLANES, nbp))), pool, preferred_element_type=F32)


def _cmp_attention(proj3, kc, qpos0, t_real, tq, act_blocks=SUBLANES):
    b, tp, _ = proj3.shape
    ncp = kc.shape[1]
    seq = qpos0 + t_real
    n_cmp = seq // NSA_CMP_STRIDE - NSA_CMP_LEN // NSA_CMP_STRIDE + 1
    n_sel_blk = -(-seq // NSA_SEL_BLOCK)
    nbp = _round_up(n_sel_blk, LANES)
    cs = np.arange(ncp)[:, None] * NSA_CMP_STRIDE
    ss = np.arange(nbp)[None, :] * NSA_SEL_BLOCK
    ov = ((cs + NSA_CMP_LEN > ss) & (cs < ss + NSA_SEL_BLOCK)
          & (np.arange(ncp)[:, None] < n_cmp) & (np.arange(nbp)[None, :] < n_sel_blk))
    ov = jnp.asarray(ov.astype(np.float32), BF16)
    qw = NSA_HEADS * NSA_KV_W
    return pl.pallas_call(
        functools.partial(_cmpattn_kernel, tq=tq, qpos0=qpos0, n_cmp=n_cmp, n_sel_blk=n_sel_blk,
                          act_blocks=act_blocks),
        grid=(b, tp // tq),
        in_specs=[pl.BlockSpec((1, tq, qw), lambda bi, i: (bi, i, PC_QA // qw)),
                  pl.BlockSpec((1, ncp, 2 * NSA_KV_W), lambda bi, i: (bi, 0, 0)),
                  pl.BlockSpec(ov.shape, lambda bi, i: (0, 0))],
        out_specs=[pl.BlockSpec((1, tq, NSA_Q_W), lambda bi, i: (bi, i, 0)),
                   pl.BlockSpec((1, NSA_KV_HEADS, tq, nbp), lambda bi, i: (bi, 0, i, 0)),
                   pl.BlockSpec((1, 1, SUBLANES, LANES), lambda bi, i: (bi, i, 0, 0))],
        out_shape=[jax.ShapeDtypeStruct((b, tp, NSA_Q_W), F32),
                   jax.ShapeDtypeStruct((b, NSA_KV_HEADS, tp, nbp), F32),
                   jax.ShapeDtypeStruct((b, tp // tq, SUBLANES, LANES), F32)],
        compiler_params=_cparams("parallel", "arbitrary"),
        name="cmp_attn",
    )(proj3, kc, ov)


def _kmean_kernel(*refs, n_x):
    x_refs, o_ref = refs[:n_x], refs[n_x]
    outs = [jnp.sum(r[0], axis=0, keepdims=True) * (1.0 / MOBA_BLOCK) for r in x_refs]
    o_ref[0, 0] = outs[0] if len(outs) == 1 else jnp.concatenate(outs, axis=0)


KMEAN_BLOCKS_PER_STEP = 16


def _moba_kmean(rows3, page_table, base=0):
    w = MOBA_KV_W
    if page_table is None:
        b, t, _ = rows3.shape
        n_blk = t // MOBA_BLOCK
        bps = math.gcd(KMEAN_BLOCKS_PER_STEP, n_blk)
        steps = n_blk // bps
        out = pl.pallas_call(
            functools.partial(_kmean_kernel, n_x=bps),
            grid=(b, steps),
            in_specs=[pl.BlockSpec((1, MOBA_BLOCK, w), (lambda bi, s, k=k: (bi, s * bps + k, PC_KVB // w)))
                      for k in range(bps)],
            out_specs=pl.BlockSpec((1, 1, bps, w), lambda bi, s: (bi, s, 0, 0)),
            out_shape=jax.ShapeDtypeStruct((b, steps, bps, w), F32),
            compiler_params=_cparams("parallel", "arbitrary"),
            name="kmean",
        )(*([rows3] * bps))
        return out.reshape(b, n_blk, w)
    b, n_pages = page_table.shape
    ppb = MOBA_BLOCK // PAGE_SIZE
    n_blk = n_pages // ppb
    bps = math.gcd(KMEAN_BLOCKS_PER_STEP, n_blk)
    steps = n_blk // bps
    npp = bps * ppb

    def x_map(k):
        return lambda bi, s, pt: (base + pt[bi * n_pages + s * npp + k], 0, 0)

    grid_spec = pltpu.PrefetchScalarGridSpec(
        num_scalar_prefetch=1,
        grid=(b, steps),
        in_specs=[pl.BlockSpec((1, w, PAGE_SIZE), x_map(k)) for k in range(npp)],
        out_specs=pl.BlockSpec((1, 1, bps, w), lambda bi, s, pt: (bi, s, 0, 0)),
    )
    out = pl.pallas_call(
        functools.partial(_kmean_t_kernel, npp=npp),
        grid_spec=grid_spec,
        out_shape=jax.ShapeDtypeStruct((b, steps, bps, w), F32),
        compiler_params=_cparams("parallel", "arbitrary"),
        name="kmean_paged",
    )(page_table.reshape(-1), *([rows3] * npp))
    return out.reshape(b, n_blk, w)


def _kmean_t_kernel(*refs, npp):
    page_refs, o_ref = refs[1:1 + npp], refs[1 + npp]
    ppb = MOBA_BLOCK // PAGE_SIZE
    ones = jnp.ones((SUBLANES, PAGE_SIZE), BF16)
    d = functools.partial(lax.dot_general, dimension_numbers=(((1,), (1,)), ((), ())), preferred_element_type=F32)
    outs = []
    for i in range(npp // ppb):
        acc = None
        for k in range(ppb):
            hi, lo = _split(page_refs[i * ppb + k][0])
            part = d(ones, hi) + d(ones, lo)
            acc = part if acc is None else acc + part
        outs.append(acc[0:1] * (1.0 / MOBA_BLOCK))
    o_ref[0, 0] = outs[0] if len(outs) == 1 else jnp.concatenate(outs, axis=0)


def _mobagate_kernel(q_ref, km_ref, m_ref, *, tq, qpos0, k_top, as_bias):
    w = MOBA_KV_W
    nbp = km_ref.shape[1]
    qbase = qpos0 + pl.program_id(1) * tq
    q = q_ref[0]
    qs = jnp.concatenate([q[:, h * w:(h + 1) * w] for h in range(MOBA_HEADS)], axis=0)
    gs = _dot3_nt(qs, km_ref[0])
    rows = MOBA_HEADS * tq
    lane = lax.broadcasted_iota(jnp.int32, (rows, nbp), 1)
    own = (qbase + lax.broadcasted_iota(jnp.int32, (rows, 1), 0) % tq) // MOBA_BLOCK
    score = jnp.where(lane < own, gs, -jnp.inf)
    sel = jnp.where(lane == own, 1.0, _top_select(score, lane.astype(F32), k_top))
    if as_bias:
        m_ref[0] = jnp.where(sel > 0.5, 0.0, NEG).T
    else:
        for h in range(MOBA_HEADS):
            m_ref[0, h] = sel[h * tq:(h + 1) * tq]


def _moba_gate(proj3, kmean, qpos0, t_real, tq, as_bias):
    b, tp, _ = proj3.shape
    n_mb = -(-(qpos0 + t_real) // MOBA_BLOCK)
    nbp = _round_up(n_mb, LANES)
    km = jnp.pad(kmean, ((0, 0), (0, nbp - kmean.shape[1]), (0, 0)))
    qw = MOBA_HEADS * MOBA_KV_W
    if as_bias:
        assert b == 1
        out_spec = pl.BlockSpec((1, nbp, MOBA_HEADS * tq), lambda bi, i: (i, 0, 0))
        out_shape = jax.ShapeDtypeStruct((tp // tq, nbp, MOBA_HEADS * tq), F32)
    else:
        out_spec = pl.BlockSpec((1, MOBA_HEADS, tq, nbp), lambda bi, i: (bi, 0, i, 0))
        out_shape = jax.ShapeDtypeStruct((b, MOBA_HEADS, tp, nbp), F32)
    return pl.pallas_call(
        functools.partial(_mobagate_kernel, tq=tq, qpos0=qpos0, k_top=min(MOBA_TOPK, n_mb), as_bias=as_bias),
        grid=(b, tp // tq),
        in_specs=[pl.BlockSpec((1, tq, qw), lambda bi, i: (bi, i, PC_QB // qw)),
                  pl.BlockSpec((1, nbp, MOBA_KV_W), lambda bi, i: (bi, 0, 0))],
        out_specs=out_spec,
        out_shape=out_shape,
        compiler_params=_cparams("parallel", "arbitrary"),
        name="moba_gate",
    )(proj3, km)


def _attend_tile(qs, k, v, mask, kbase, qbase, m_sc, l_sc, acc_sc, alpha_sc, *, n_heads, n_groups, n_mask, tq,
                 blk_shift, window, slopes, causal=True):
    tk = k.shape[0]
    s_all = _dot_nt(qs, k)
    delta = qbase - kbase
    col1 = lax.broadcasted_iota(jnp.int32, (1, tk), 1)
    colf = (col1 - delta).astype(F32)
    ok = None
    if causal or window is not None:
        dd = lax.broadcasted_iota(jnp.int32, (tq, tk), 0) - lax.broadcasted_iota(jnp.int32, (tq, tk), 1)
        ok = dd >= -delta
        if window is not None:
            ok = ok & (dd < window - delta)
    if n_mask:
        nbp = mask.shape[-1]
        blk_col = (kbase + col1) >> blk_shift
        expand = jnp.where(lax.broadcasted_iota(jnp.int32, (nbp, tk), 0) == blk_col, 1.0, 0.0).astype(BF16)
        mexp = jnp.dot(_bf(mask.reshape(n_mask * tq, nbp)), expand, preferred_element_type=F32)
    hpm = n_heads // n_mask if n_mask else n_heads
    ps = []
    valid = ok
    for h in range(n_heads):
        rows = slice(h * tq, (h + 1) * tq)
        s = s_all[rows] + slopes[h] * colf
        if n_mask and h % hpm == 0:
            g = h // hpm
            valid = mexp[g * tq:(g + 1) * tq] > 0.5
            if ok is not None:
                valid = ok & valid
        if valid is not None:
            s = jnp.where(valid, s, NEG)
        m_old = m_sc[rows]
        m_new = jnp.maximum(m_old, jnp.max(s, axis=-1, keepdims=True))
        alpha_sc[rows] = jnp.exp2(m_old - m_new)
        m_sc[rows] = m_new
        ps.append(_bf(jnp.exp2(s - _lane_rep(m_new, tk))))
    p_all = jnp.concatenate(ps, axis=0)
    alpha = alpha_sc[...]
    l_sc[...] = alpha * l_sc[...] + jnp.dot(p_all, jnp.ones((tk, LANES), BF16), preferred_element_type=F32)
    acc_sc[...] = (_lane_rep(alpha, acc_sc.shape[1]) * acc_sc[...]
                   + jnp.dot(p_all, _bf(v), preferred_element_type=F32))


def _lane_rep(x, n):
    return x if n == LANES else jnp.concatenate([x] * (n // LANES), axis=1)


def _attn_finish(m_sc, l_sc, acc_sc, o_ref, n_heads, n_groups, tq):
    w = acc_sc.shape[1]
    o = jnp.where(_lane_rep(m_sc[...], w) > 0.5 * NEG, acc_sc[...] / _lane_rep(l_sc[...], w), 0.0)
    _place_heads([o[h * tq:(h + 1) * tq] for h in range(n_heads)], n_heads, n_groups, o_ref)


def _attn_init(q_ref, q_sc, m_sc, l_sc, acc_sc, n_heads, w):
    q_sc[...] = _stack_heads(q_ref[0], n_heads, w)
    m_sc[...] = jnp.full(m_sc.shape, NEG, F32)
    l_sc[...] = jnp.zeros(l_sc.shape, F32)
    acc_sc[...] = jnp.zeros(acc_sc.shape, F32)


def _kv_tile_range(qi, *, tq, tk, qpos0, kpos0, window, n_kt):
    qlo = qpos0 + qi * tq
    last = jnp.minimum((qlo + tq - 1 - kpos0) // tk, n_kt - 1)
    if window is None:
        first = 0
    else:
        first = jnp.maximum(qlo - window + 1 - kpos0, 0) // tk
    return first, last


def _flash_kernel(*refs, cfg, rng):
    if cfg["n_mask"]:
        q_ref, kv_ref, mask_ref, o_ref, q_sc, m_sc, l_sc, acc_sc, alpha_sc = refs
    else:
        q_ref, kv_ref, o_ref, q_sc, m_sc, l_sc, acc_sc, alpha_sc = refs
        mask_ref = None
    w, tq, tk = cfg["n_groups"] * HEAD_DIM, rng["tq"], rng["tk"]
    qi, j = pl.program_id(1), pl.program_id(2)
    first, last = _kv_tile_range(qi, **rng)

    @pl.when(j == 0)
    def _():
        _attn_init(q_ref, q_sc, m_sc, l_sc, acc_sc, cfg["n_heads"], w)

    jt = first + j
    kbase = rng["kpos0"] + jt * tk
    qbase = rng["qpos0"] + qi * tq

    def tile(causal):
        kv = kv_ref[0]
        _attend_tile(q_sc[...], kv[:, :w], kv[:, w:], None if mask_ref is None else mask_ref[0], kbase, qbase,
                     m_sc, l_sc, acc_sc, alpha_sc, tq=tq, causal=causal, **cfg)

    if cfg["window"] is None:
        fully_past = kbase + tk <= qbase
        pl.when((jt <= last) & fully_past)(lambda: tile(False))
        pl.when((jt <= last) & jnp.logical_not(fully_past))(lambda: tile(True))
    else:
        pl.when(jt <= last)(lambda: tile(True))

    @pl.when(j == pl.num_programs(2) - 1)
    def _():
        _attn_finish(m_sc, l_sc, acc_sc, o_ref, cfg["n_heads"], cfg["n_groups"], tq)


def _flash_linear(q_arr, q_col, kv_arr, kv_col, mask, cfg, *, tq, tk, qpos0, kpos0, n_steps):
    b, tp, _ = q_arr.shape
    w = cfg["n_groups"] * HEAD_DIM
    n_heads = cfg["n_heads"]
    n_kt = kv_arr.shape[1] // tk
    rng = dict(tq=tq, tk=tk, qpos0=qpos0, kpos0=kpos0, window=cfg["window"], n_kt=n_kt)

    def kv_map(bi, i, j):
        first, last = _kv_tile_range(i, **rng)
        return (bi, jnp.minimum(first + j, last), kv_col)

    in_specs = [pl.BlockSpec((1, tq, n_heads * w), lambda bi, i, j: (bi, i, q_col)),
                pl.BlockSpec((1, tk, 2 * w), kv_map)]
    args = [q_arr, kv_arr]
    if cfg["n_mask"]:
        in_specs.append(pl.BlockSpec((1, cfg["n_mask"], tq, mask.shape[-1]), lambda bi, i, j: (bi, 0, i, 0)))
        args.append(mask)
    return pl.pallas_call(
        functools.partial(_flash_kernel, cfg=cfg, rng=rng),
        grid=(b, tp // tq, n_steps),
        in_specs=in_specs,
        out_specs=pl.BlockSpec((1, tq, n_heads * HEAD_DIM), lambda bi, i, j: (bi, i, 0)),
        out_shape=jax.ShapeDtypeStruct((b, tp, n_heads * HEAD_DIM), F32),
        scratch_shapes=[pltpu.VMEM((n_heads * tq, w), BF16), pltpu.VMEM((n_heads * tq, LANES), F32),
                        pltpu.VMEM((n_heads * tq, LANES), F32), pltpu.VMEM((n_heads * tq, w), F32),
                        pltpu.VMEM((n_heads * tq, LANES), F32)],
        compiler_params=_cparams("parallel", "parallel", "arbitrary"),
        name="flash",
    )(*args)


def _flash_t_kernel(q_ref, k_ref, vt_ref, rb_ref, o_ref, q_sc, m_sc, l_sc, acc_sc, *, cfg, rng):
    n_heads, n_groups, slopes = cfg["n_heads"], cfg["n_groups"], cfg["slopes"]
    w, tq, tk = n_groups * HEAD_DIM, rng["tq"], rng["tk"]
    qi, j = pl.program_id(1), pl.program_id(2)
    first, last = _kv_tile_range(qi, **rng)

    @pl.when(j == 0)
    def _():
        q_sc[...] = _stack_heads(q_ref[0], n_heads, w)
        m_sc[...] = jnp.full(m_sc.shape, NEG, F32)
        l_sc[...] = jnp.zeros(l_sc.shape, F32)
        acc_sc[...] = jnp.zeros(acc_sc.shape, F32)

    jt = first + j
    kbase = rng["kpos0"] + jt * tk
    qbase = rng["qpos0"] + qi * tq

    def tile(causal):
        s_t = _dot_nt(k_ref[0], q_sc[...])
        kb = (lax.broadcasted_iota(jnp.int32, (tk, LANES), 0) + (kbase - qbase)).astype(F32)
        rb = rb_ref[0, pl.ds(jt % SUBLANES, 1), :]
        if causal:
            ok = (lax.broadcasted_iota(jnp.int32, (tk, tq), 0) - lax.broadcasted_iota(jnp.int32, (tk, tq), 1)
                  <= qbase - kbase)
        hpg = n_heads // n_groups
        vrows = HEAD_DIM + SUBLANES
        for g in range(n_groups):
            ps, alphas = [], []
            for lt in range(g * hpg * tq // LANES, (g + 1) * hpg * tq // LANES):
                h = lt * LANES // tq
                lanes = slice(lt * LANES, (lt + 1) * LANES)
                s = s_t[:, lanes] + slopes[h] * kb
                if causal:
                    qq0 = (lt * LANES) % tq
                    s = jnp.where(ok[:, qq0:qq0 + LANES], s, NEG)
                rbl = rb[:, lanes]
                m_old = m_sc[:, lanes]
                m_new = jnp.maximum(m_old, jnp.max(s, axis=0, keepdims=True) + rbl)
                alphas.append(jnp.exp2(m_old - m_new))
                m_sc[:, lanes] = m_new
                shift = jnp.where(m_new > 0.5 * NEG, m_new - rbl, -NEG)
                ps.append(_bf(jnp.exp2(s - shift)))
            glanes = slice(g * hpg * tq, (g + 1) * hpg * tq)
            alpha = jnp.concatenate(alphas, axis=1)
            pv = jnp.dot(_bf(vt_ref[0, g * vrows:(g + 1) * vrows, :]), jnp.concatenate(ps, axis=1),
                         preferred_element_type=F32)
            acc_sc[:, glanes] = alpha * acc_sc[:, glanes] + pv[:HEAD_DIM]
            l_sc[:, glanes] = alpha * l_sc[:, glanes] + pv[HEAD_DIM:HEAD_DIM + 1]

    fully_past = kbase + tk <= qbase
    pl.when((jt <= last) & fully_past)(lambda: tile(False))
    pl.when((jt <= last) & jnp.logical_not(fully_past))(lambda: tile(True))

    @pl.when(j == pl.num_programs(2) - 1)
    def _():
        o = jnp.where(m_sc[...] > 0.5 * NEG, acc_sc[...] / l_sc[...], 0.0)
        for pair in range(n_heads // 2):
            slab = jnp.concatenate([o[:, h * tq:(h + 1) * tq] for h in (2 * pair, 2 * pair + 1)], axis=0)
            o_ref[0, :, pair * LANES:(pair + 1) * LANES] = slab.T


def _flash_transposed(q_arr, q_col, k_arr, k_col, v_t, bias_t, cfg, *, tq, tk, n_steps):
    b, tp, _ = q_arr.shape
    assert b == 1 and (1 << cfg["blk_shift"]) == tk
    w = cfg["n_groups"] * HEAD_DIM
    n_heads = cfg["n_heads"]
    nq = tp // tq
    r = n_heads * tq
    assert bias_t.shape[0] == nq and bias_t.shape[2] == r
    n_groups = cfg["n_groups"]
    v_t = jnp.concatenate([v_t.reshape(1, n_groups, HEAD_DIM, tp),
                           jnp.ones((1, n_groups, SUBLANES, tp), F32)], axis=2)
    vrows = n_groups * (HEAD_DIM + SUBLANES)
    v_t = v_t.reshape(1, vrows, tp)
    rng = dict(tq=tq, tk=tk, qpos0=0, kpos0=0, window=None, n_kt=tp // tk)

    def tile_of(i, j):
        first, last = _kv_tile_range(i, **rng)
        return jnp.minimum(first + j, last)

    return pl.pallas_call(
        functools.partial(_flash_t_kernel, cfg=cfg, rng=rng),
        grid=(b, nq, n_steps),
        in_specs=[pl.BlockSpec((1, tq, n_heads * w), lambda bi, i, j: (bi, i, q_col)),
                  pl.BlockSpec((1, tk, w), lambda bi, i, j: (bi, tile_of(i, j), k_col)),
                  pl.BlockSpec((1, vrows, tk), lambda bi, i, j: (bi, 0, tile_of(i, j))),
                  pl.BlockSpec((1, SUBLANES, r), lambda bi, i, j: (i, tile_of(i, j) // SUBLANES, 0))],
        out_specs=pl.BlockSpec((1, tq, n_heads * HEAD_DIM), lambda bi, i, j: (bi, i, 0)),
        out_shape=jax.ShapeDtypeStruct((b, tp, n_heads * HEAD_DIM), F32),
        scratch_shapes=[pltpu.VMEM((r, w), BF16), pltpu.VMEM((1, r), F32), pltpu.VMEM((1, r), F32),
                        pltpu.VMEM((HEAD_DIM, r), F32)],
        compiler_params=_cparams("parallel", "parallel", "arbitrary"),
        name="flash_t",
    )(q_arr, k_arr, v_t, bias_t)


def _worklist(act, tq, tk):
    nq, n_kt = act.shape
    i = np.arange(nq)[:, None]
    jt = np.arange(n_kt)[None, :]
    diag = np.minimum((i * tq + tq - 1) // tk, n_kt - 1)
    causal = jt <= diag
    s_max = int(causal.sum())
    need = (act | jnp.asarray(jt == diag)) & jnp.asarray(causal)
    key = jnp.where(need, jnp.asarray(i * n_kt + jt, jnp.int32), nq * n_kt).reshape(-1)
    key = jnp.sort(key)[:s_max]
    n_need = jnp.sum(need.astype(jnp.int32))
    pos = jnp.arange(s_max, dtype=jnp.int32)
    valid = pos < n_need
    key = jnp.where(valid, key, jnp.take(key, n_need - 1))
    qi, kt = key // n_kt, key % n_kt
    first = valid & ((pos == 0) | (qi != jnp.roll(qi, 1)))
    last = valid & ((pos == n_need - 1) | (qi != jnp.roll(qi, -1)))
    return jnp.concatenate([qi, kt, first.astype(jnp.int32), last.astype(jnp.int32),
                            valid.astype(jnp.int32)]).astype(jnp.int32), s_max


def _flash_list_kernel(wl_ref, q_ref, kv_ref, mask_ref, o_ref, q_sc, m_sc, l_sc, acc_sc, alpha_sc, *,
                       cfg, tq, tk, s_max):
    s = pl.program_id(0)
    w = cfg["n_groups"] * HEAD_DIM
    qbase = wl_ref[s] * tq
    kbase = wl_ref[s_max + s] * tk
    valid = wl_ref[4 * s_max + s] == 1

    @pl.when(wl_ref[2 * s_max + s] == 1)
    def _():
        _attn_init(q_ref, q_sc, m_sc, l_sc, acc_sc, cfg["n_heads"], w)

    def tile(causal):
        kv = kv_ref[0]
        _attend_tile(q_sc[...], kv[:, :w], kv[:, w:], mask_ref[0], kbase, qbase, m_sc, l_sc, acc_sc, alpha_sc,
                     tq=tq, causal=causal, **cfg)

    fully_past = kbase + tk <= qbase
    pl.when(valid & fully_past)(lambda: tile(False))
    pl.when(valid & jnp.logical_not(fully_past))(lambda: tile(True))

    @pl.when(wl_ref[3 * s_max + s] == 1)
    def _():
        _attn_finish(m_sc, l_sc, acc_sc, o_ref, cfg["n_heads"], cfg["n_groups"], tq)


def _flash_worklist(q_arr, q_col, kv_arr, kv_col, mask, act, cfg, *, tq, tk):
    b, tp, _ = q_arr.shape
    assert b == 1 and cfg["n_mask"] and cfg["window"] is None
    w = cfg["n_groups"] * HEAD_DIM
    n_heads = cfg["n_heads"]
    wl, s_max = _worklist(act, tq, tk)
    grid_spec = pltpu.PrefetchScalarGridSpec(
        num_scalar_prefetch=1,
        grid=(s_max,),
        in_specs=[pl.BlockSpec((1, tq, n_heads * w), lambda s, wl: (0, wl[s], q_col)),
                  pl.BlockSpec((1, tk, 2 * w), lambda s, wl: (0, wl[s_max + s], kv_col)),
                  pl.BlockSpec((1, cfg["n_mask"], tq, mask.shape[-1]), lambda s, wl: (0, 0, wl[s], 0))],
        out_specs=pl.BlockSpec((1, tq, n_heads * HEAD_DIM), lambda s, wl: (0, wl[s], 0)),
        scratch_shapes=[pltpu.VMEM((n_heads * tq, w), BF16), pltpu.VMEM((n_heads * tq, LANES), F32),
                        pltpu.VMEM((n_heads * tq, LANES), F32), pltpu.VMEM((n_heads * tq, w), F32),
                        pltpu.VMEM((n_heads * tq, LANES), F32)],
    )
    return pl.pallas_call(
        functools.partial(_flash_list_kernel, cfg=cfg, tq=tq, tk=tk, s_max=s_max),
        grid_spec=grid_spec,
        out_shape=jax.ShapeDtypeStruct((b, tp, n_heads * HEAD_DIM), F32),
        compiler_params=_cparams("arbitrary"),
        name="flash_list",
    )(wl, q_arr, kv_arr, mask)


DECODE_PAGES_PER_STEP = 32


def _decode_update(qs, k_op, v_op, transposed, kbase, qpos_col, slope_col, mrow, m_sc, l_sc, acc_sc, *,
                   blk_shift, window, qpos0):
    s = jnp.dot(qs, _bf(k_op), preferred_element_type=F32) if transposed else _dot_nt(qs, k_op)
    tk = s.shape[1]
    kpos = kbase + lax.broadcasted_iota(jnp.int32, (1, tk), 1)
    d = qpos_col - kpos
    ok = d >= 0
    if window is not None:
        ok = ok & (d < window)
    if mrow is not None:
        nbp = mrow.shape[1]
        expand = jnp.where(lax.broadcasted_iota(jnp.int32, (nbp, tk), 0) == (kpos >> blk_shift), 1.0, 0.0)
        ok = ok & (jnp.dot(mrow, expand.astype(BF16), preferred_element_type=F32) > 0.5)
    s = jnp.where(ok, s + slope_col * (kpos - qpos0).astype(F32), NEG)
    m_old = m_sc[...]
    m_new = jnp.maximum(m_old, jnp.max(s, axis=-1, keepdims=True))
    alpha = jnp.exp2(m_old - m_new)
    p = _bf(jnp.exp2(s - m_new))
    l_sc[...] = alpha * l_sc[...] + jnp.sum(p.astype(F32), axis=-1, keepdims=True)
    m_sc[...] = m_new
    pv = _dot_nt(p, v_op) if transposed else jnp.dot(p, _bf(v_op), preferred_element_type=F32)
    acc_sc[...] = alpha * acc_sc[...] + pv


def _decode_kernel(*refs, cfg, npp, tq, qpos0, kpos0, tile_rows):
    has_mask = bool(cfg["n_mask"])
    page_refs = refs[2:2 + npp]
    q_ref, tail_ref = refs[1], refs[2 + npp]
    rest = refs[3 + npp:]
    if has_mask:
        mask_ref, o_ref, q_sc, m_sc, l_sc, acc_sc, slope_sc, mrow_sc = rest
    else:
        o_ref, q_sc, m_sc, l_sc, acc_sc, slope_sc = rest
    n_heads = cfg["n_heads"]
    w = cfg["n_groups"] * HEAD_DIM
    j = pl.program_id(1)
    n_steps = pl.num_programs(1)
    row = lax.broadcasted_iota(jnp.int32, (n_heads * tq, 1), 0)

    @pl.when(j == 0)
    def _():
        _attn_init(q_ref, q_sc, m_sc, l_sc, acc_sc, n_heads, w)
        slope = jnp.zeros((n_heads * tq, 1), F32)
        for h in range(n_heads):
            slope = jnp.where(row // tq == h, cfg["slopes"][h], slope)
        slope_sc[...] = slope
        if has_mask:
            hpm = n_heads // cfg["n_mask"]
            mrow_sc[...] = _bf(jnp.concatenate([mask_ref[0, h // hpm] for h in range(n_heads)], axis=0))

    qpos_col = qpos0 + row % tq
    kw = dict(blk_shift=cfg["blk_shift"], window=cfg["window"], qpos0=qpos0)
    mrow = mrow_sc[...] if has_mask else None

    @pl.when(j < n_steps - 1)
    def _():
        kt = [r[0, :w, :] for r in page_refs]
        vt = [r[0, w:, :] for r in page_refs]
        kt = kt[0] if npp == 1 else jnp.concatenate(kt, axis=1)
        vt = vt[0] if npp == 1 else jnp.concatenate(vt, axis=1)
        _decode_update(q_sc[...], kt, vt, True, kpos0 + j * (npp * tile_rows), qpos_col, slope_sc[...], mrow,
                       m_sc, l_sc, acc_sc, **kw)

    @pl.when(j == n_steps - 1)
    def _():
        tail = tail_ref[0]
        _decode_update(q_sc[...], tail[:, :w], tail[:, w:], False, qpos0, qpos_col, slope_sc[...], mrow,
                       m_sc, l_sc, acc_sc, **kw)
        o = jnp.where(m_sc[...] > 0.5 * NEG, acc_sc[...] / l_sc[...], 0.0)
        _place_heads([o[h * tq:(h + 1) * tq] for h in range(n_heads)], n_heads, cfg["n_groups"], o_ref)


def _decode_attn(proj3, q_col, tail_col, tiles_t, table, base, mask, cfg, *, qpos0, kpos0):
    b, tq, _ = proj3.shape
    w = cfg["n_groups"] * HEAD_DIM
    n_heads = cfg["n_heads"]
    n_tiles = table.shape[1]
    tile_rows = tiles_t.shape[2]
    npp = math.gcd(DECODE_PAGES_PER_STEP, n_tiles)
    steps = n_tiles // npp
    r = n_heads * tq

    def tile_map(k):
        return lambda bi, j, tb: (base + tb[bi * n_tiles + jnp.minimum(j, steps - 1) * npp + k], 0, 0)

    in_specs = ([pl.BlockSpec((1, tq, n_heads * w), lambda bi, j, tb: (bi, 0, q_col))]
                + [pl.BlockSpec((1, 2 * w, tile_rows), tile_map(k)) for k in range(npp)]
                + [pl.BlockSpec((1, tq, 2 * w), lambda bi, j, tb: (bi, 0, tail_col))])
    args = [table.reshape(-1), proj3] + [tiles_t] * npp + [proj3]
    scratch = [pltpu.VMEM((r, w), BF16), pltpu.VMEM((r, 1), F32), pltpu.VMEM((r, 1), F32), pltpu.VMEM((r, w), F32),
               pltpu.VMEM((r, 1), F32)]
    if cfg["n_mask"]:
        in_specs.append(pl.BlockSpec((1, cfg["n_mask"], tq, mask.shape[-1]), lambda bi, j, tb: (bi, 0, 0, 0)))
        args.append(mask)
        scratch.append(pltpu.VMEM((r, mask.shape[-1]), BF16))
    grid_spec = pltpu.PrefetchScalarGridSpec(
        num_scalar_prefetch=1,
        grid=(b, steps + 1),
        in_specs=in_specs,
        out_specs=pl.BlockSpec((1, tq, n_heads * HEAD_DIM), lambda bi, j, tb: (bi, 0, 0)),
        scratch_shapes=scratch,
    )
    return pl.pallas_call(
        functools.partial(_decode_kernel, cfg=cfg, npp=npp, tq=tq, qpos0=qpos0, kpos0=kpos0, tile_rows=tile_rows),
        grid_spec=grid_spec,
        out_shape=jax.ShapeDtypeStruct((b, tq, n_heads * HEAD_DIM), F32),
        compiler_params=_cparams("parallel", "arbitrary"),
        name="decode_attn",
    )(*args)


def _attn_cfg(kind):
    if kind == "moba":
        return dict(n_heads=MOBA_HEADS, n_groups=MOBA_KV_HEADS, n_mask=MOBA_HEADS,
                    blk_shift=int(math.log2(MOBA_BLOCK)), window=None, slopes=SLOPES_B)
    if kind == "sel":
        return dict(n_heads=NSA_HEADS, n_groups=NSA_KV_HEADS, n_mask=NSA_KV_HEADS,
                    blk_shift=int(math.log2(NSA_SEL_BLOCK)), window=None, slopes=SLOPES_A)
    return dict(n_heads=NSA_HEADS, n_groups=NSA_KV_HEADS, n_mask=0, blk_shift=0, window=NSA_WINDOW, slopes=SLOPES_A)


S5_CHUNK = 4 * LANES
S5_GCHUNK = S5_CHUNK // S5_STATE


def _s5_params(lp):
    lr, li = lp["s5_lambda_re"], lp["s5_lambda_im"]
    dt = jnp.exp(lp["s5_log_dt"])[:, None]
    mag = jnp.exp(lr * dt)
    a_re = mag * jnp.cos(li * dt)
    a_im = mag * jnp.sin(li * dt)
    den = lr * lr + li * li
    f_re = ((a_re - 1.0) * lr + a_im * li) / den
    f_im = (a_im * lr - (a_re - 1.0) * li) / den
    b_re, b_im = lp["s5_b_re"], lp["s5_b_im"]
    bb_re = f_re[..., None] * b_re - f_im[..., None] * b_im
    bb_im = f_re[..., None] * b_im + f_im[..., None] * b_re
    n_ch = S5_GROUPS // S5_GCHUNK
    eye = np.eye(S5_GCHUNK, dtype=np.float32)

    def in_w(bb):
        x = bb.reshape(n_ch, S5_GCHUNK, S5_STATE, S5_GROUP_CH).transpose(0, 1, 3, 2)
        x = x[:, :, :, None, :] * eye[None, :, None, :, None]
        return x.reshape(n_ch, S5_GCHUNK * S5_GROUP_CH, S5_CHUNK)

    def out_w(c):
        x = c.reshape(n_ch, S5_GCHUNK, S5_GROUP_CH, S5_STATE).transpose(0, 1, 3, 2)
        x = x[:, :, :, None, :] * eye[None, :, None, :, None]
        return x.reshape(n_ch, S5_CHUNK, S5_GCHUNK * S5_GROUP_CH).astype(BF16)

    return dict(a_re=a_re.reshape(1, S5_LANES), a_im=a_im.reshape(1, S5_LANES),
                bw_re=in_w(bb_re), bw_im=in_w(bb_im),
                cw_re=out_w(lp["s5_c_re"]), cw_im=out_w(lp["s5_c_im"]),
                d=lp["s5_d"].reshape(1, S5_WIDTH))


def _s5_out_proj(h_re, h_im, cwr_ref, cwi_ref):
    cols = []
    for c in range(S5_LANES // S5_CHUNK):
        sl = slice(c * S5_CHUNK, (c + 1) * S5_CHUNK)
        cols.append(jnp.dot(_bf(h_re[:, sl]), cwr_ref[c], preferred_element_type=F32)
                    - jnp.dot(_bf(h_im[:, sl]), cwi_ref[c], preferred_element_type=F32))
    return jnp.concatenate(cols, axis=1)


def _s5scan_kernel(u_ref, h0r_ref, h0i_ref, ar_ref, ai_ref, bwr_ref, bwi_ref, cwr_ref, cwi_ref, d_ref,
                   y_ref, hr_ref, hi_ref, bur_sc, bui_sc, hr_sc, hi_sc, *, n_par, n_j):
    @pl.when(pl.program_id(0) == 0)
    def _():
        hr_sc[...] = h0r_ref[...]
        hi_sc[...] = h0i_ref[...]

    u = u_ref[...]
    gw = S5_GCHUNK * S5_GROUP_CH
    for c in range(S5_LANES // S5_CHUNK):
        uc = u[:, c * gw:(c + 1) * gw]
        bur_sc[:, c * S5_CHUNK:(c + 1) * S5_CHUNK] = _dot3(uc, bwr_ref[c])
        bui_sc[:, c * S5_CHUNK:(c + 1) * S5_CHUNK] = _dot3(uc, bwi_ref[c])

    for c in range(S5_LANES // S5_CHUNK):
        sl = slice(c * S5_CHUNK, (c + 1) * S5_CHUNK)
        a_re = jnp.broadcast_to(ar_ref[:, sl], (n_par, S5_CHUNK))
        a_im = jnp.broadcast_to(ai_ref[:, sl], (n_par, S5_CHUNK))

        def step(jj, carry):
            h_re, h_im = carry
            rows = pl.ds(pl.multiple_of(jj * n_par, n_par), n_par)
            n_re = a_re * h_re - a_im * h_im + bur_sc[rows, sl]
            n_im = a_re * h_im + a_im * h_re + bui_sc[rows, sl]
            bur_sc[rows, sl] = n_re
            bui_sc[rows, sl] = n_im
            return n_re, n_im

        h_re, h_im = lax.fori_loop(0, n_j, step, (hr_sc[:, sl], hi_sc[:, sl]))
        hr_sc[:, sl] = h_re
        hi_sc[:, sl] = h_im

    y_ref[...] = _s5_out_proj(bur_sc[...], bui_sc[...], cwr_ref, cwi_ref) + d_ref[...] * u
    hr_ref[...] = hr_sc[...]
    hi_ref[...] = hi_sc[...]


def _s5_scan(u_rows, h0_re, h0_im, sp, n_par, jc):
    rows = u_rows.shape[0]
    n_j_total = rows // n_par
    steps = n_j_total // jc
    full = lambda a: pl.BlockSpec(a.shape, lambda i: (0,) * a.ndim)
    ins = [u_rows, h0_re, h0_im, sp["a_re"], sp["a_im"], sp["bw_re"], sp["bw_im"], sp["cw_re"], sp["cw_im"], sp["d"]]
    return pl.pallas_call(
        functools.partial(_s5scan_kernel, n_par=n_par, n_j=jc),
        grid=(steps,),
        in_specs=[pl.BlockSpec((jc * n_par, S5_WIDTH), lambda i: (i, 0))] + [full(a) for a in ins[1:]],
        out_specs=[pl.BlockSpec((jc * n_par, S5_WIDTH), lambda i: (i, 0)),
                   pl.BlockSpec((n_par, S5_LANES), lambda i: (0, 0)),
                   pl.BlockSpec((n_par, S5_LANES), lambda i: (0, 0))],
        out_shape=[jax.ShapeDtypeStruct((rows, S5_WIDTH), F32),
                   jax.ShapeDtypeStruct((n_par, S5_LANES), F32),
                   jax.ShapeDtypeStruct((n_par, S5_LANES), F32)],
        scratch_shapes=[pltpu.VMEM((jc * n_par, S5_LANES), F32), pltpu.VMEM((jc * n_par, S5_LANES), F32),
                        pltpu.VMEM((n_par, S5_LANES), F32), pltpu.VMEM((n_par, S5_LANES), F32)],
        compiler_params=_cparams("arbitrary"),
        name="s5_scan",
    )(*ins)


def _s5fix_kernel(y_ref, er_ref, ei_ref, ar_ref, ai_ref, cwr_ref, cwi_ref, o_ref, fr_ref, fi_ref,
                  cr_sc, ci_sc, pr_sc, pi_sc, xr_sc, xi_sc, *, n_seg, n_j, n_j_total):
    a_re = ar_ref[...]
    a_im = ai_ref[...]

    @pl.when(pl.program_id(0) == 0)
    def _():
        def pw_step(_, carry):
            p_re, p_im = carry
            return a_re * p_re - a_im * p_im, a_re * p_im + a_im * p_re

        al_re, al_im = lax.fori_loop(0, n_j_total - 1, pw_step, (a_re, a_im))
        c_re = jnp.zeros((1, S5_LANES), F32)
        c_im = jnp.zeros((1, S5_LANES), F32)
        for s in range(n_seg):
            cr_sc[s:s + 1, :] = c_re
            ci_sc[s:s + 1, :] = c_im
            e_re, e_im = er_ref[s:s + 1, :], ei_ref[s:s + 1, :]
            c_re, c_im = (e_re + al_re * c_re - al_im * c_im, e_im + al_re * c_im + al_im * c_re)
        fr_ref[...] = c_re
        fi_ref[...] = c_im
        pr_sc[...] = jnp.broadcast_to(a_re, pr_sc.shape)
        pi_sc[...] = jnp.broadcast_to(a_im, pi_sc.shape)

    for c in range(S5_LANES // S5_CHUNK):
        sl = slice(c * S5_CHUNK, (c + 1) * S5_CHUNK)
        ab_re = jnp.broadcast_to(a_re[:, sl], (n_seg, S5_CHUNK))
        ab_im = jnp.broadcast_to(a_im[:, sl], (n_seg, S5_CHUNK))
        c_re, c_im = cr_sc[:, sl], ci_sc[:, sl]

        def step(jj, carry):
            p_re, p_im = carry
            rows = pl.ds(pl.multiple_of(jj * n_seg, n_seg), n_seg)
            xr_sc[rows, sl] = p_re * c_re - p_im * c_im
            xi_sc[rows, sl] = p_re * c_im + p_im * c_re
            return ab_re * p_re - ab_im * p_im, ab_re * p_im + ab_im * p_re

        p_re, p_im = lax.fori_loop(0, n_j, step, (pr_sc[:, sl], pi_sc[:, sl]))
        pr_sc[:, sl] = p_re
        pi_sc[:, sl] = p_im

    o_ref[...] = y_ref[...] + _s5_out_proj(xr_sc[...], xi_sc[...], cwr_ref, cwi_ref)


def _s5_fix(y_rows, end_re, end_im, sp, n_seg, jc):
    rows = y_rows.shape[0]
    n_j_total = rows // n_seg
    steps = n_j_total // jc
    full = lambda a: pl.BlockSpec(a.shape, lambda i: (0,) * a.ndim)
    ins = [y_rows, end_re, end_im, sp["a_re"], sp["a_im"], sp["cw_re"], sp["cw_im"]]
    seg = lambda: pltpu.VMEM((n_seg, S5_LANES), F32)
    return pl.pallas_call(
        functools.partial(_s5fix_kernel, n_seg=n_seg, n_j=jc, n_j_total=n_j_total),
        grid=(steps,),
        in_specs=[pl.BlockSpec((jc * n_seg, S5_WIDTH), lambda i: (i, 0))] + [full(a) for a in ins[1:]],
        out_specs=[pl.BlockSpec((jc * n_seg, S5_WIDTH), lambda i: (i, 0)),
                   pl.BlockSpec((1, S5_LANES), lambda i: (0, 0)),
                   pl.BlockSpec((1, S5_LANES), lambda i: (0, 0))],
        out_shape=[jax.ShapeDtypeStruct((rows, S5_WIDTH), F32),
                   jax.ShapeDtypeStruct((1, S5_LANES), F32),
                   jax.ShapeDtypeStruct((1, S5_LANES), F32)],
        scratch_shapes=[seg(), seg(), seg(), seg(),
                        pltpu.VMEM((jc * n_seg, S5_LANES), F32), pltpu.VMEM((jc * n_seg, S5_LANES), F32)],
        compiler_params=_cparams("arbitrary"),
        name="s5_fix",
    )(*ins)


S5_SEGMENTS = SUBLANES


def _s5_prompt(u, sp):
    t = u.shape[0]
    n_seg = S5_SEGMENTS
    seg_len = t // n_seg
    jc = min(64, seg_len)
    u_rows = u.reshape(n_seg, seg_len, S5_WIDTH).transpose(1, 0, 2).reshape(t, S5_WIDTH)
    zero = jnp.zeros((n_seg, S5_LANES), F32)
    y_loc, end_re, end_im = _s5_scan(u_rows, zero, zero, sp, n_seg, jc)
    y_rows, f_re, f_im = _s5_fix(y_loc, end_re, end_im, sp, n_seg, jc)
    y = y_rows.reshape(seg_len, n_seg, S5_WIDTH).transpose(1, 0, 2).reshape(t, S5_WIDTH)
    return y, f_re, f_im


def _s5_sample(u3, h0_re, h0_im, sp):
    b, t, _ = u3.shape
    u_rows = u3.transpose(1, 0, 2).reshape(t * b, S5_WIDTH)
    y_rows, h_re, h_im = _s5_scan(u_rows, h0_re, h0_im, sp, b, t)
    return y_rows.reshape(t, b, S5_WIDTH).transpose(1, 0, 2), h_re, h_im


def _merge_kernel(x_ref, oc_ref, os_ref, ow_ref, om_ref, y5_ref, ga_ref, gm_ref, wglu_ref, bglu_ref,
                  wn_ref, wm_ref, w5_ref, wo_ref, g_ref, b_ref, o_ref):
    qw = NSA_Q_W
    ga = ga_ref[...]
    o_nsa = (_sigmoid(ga[:, :qw]) * oc_ref[...] + _sigmoid(ga[:, qw:2 * qw]) * os_ref[...]
             + _sigmoid(ga[:, 2 * qw:]) * ow_ref[...])
    z = _gelu(y5_ref[...])
    o_s5 = z * _sigmoid(jnp.dot(_bf(z), wglu_ref[...], preferred_element_type=F32) + bglu_ref[...])
    gm = gm_ref[...]
    d = D_MODEL
    merged = (_sigmoid(gm[:, :d]) * jnp.dot(_bf(o_nsa), wn_ref[...], preferred_element_type=F32)
              + _sigmoid(gm[:, d:2 * d]) * jnp.dot(_bf(om_ref[...]), wm_ref[...], preferred_element_type=F32)
              + _sigmoid(gm[:, 2 * d:]) * jnp.dot(_bf(o_s5), w5_ref[...], preferred_element_type=F32))
    mix = jnp.dot(_bf(merged), wo_ref[...], preferred_element_type=F32)
    o_ref[...] = _layer_norm(DN_ALPHA * x_ref[...] + mix, g_ref[...], b_ref[...])


def _merge(x2, o_c, o_s, o_w, o_m, y5, proj2, lp, tm):
    m = x2.shape[0]
    row = lambda n: pl.BlockSpec((tm, n), lambda i: (i, 0))
    full = lambda a: pl.BlockSpec(a.shape, lambda i: (0,) * a.ndim)
    ws = [lp["s5_w_glu"].astype(BF16), lp["s5_b_glu"].reshape(1, S5_WIDTH),
          lp["w_br_nsa"].astype(BF16), lp["w_br_moba"].astype(BF16), lp["w_br_s5"].astype(BF16),
          lp["w_out"].astype(BF16), lp["ln1_g"].reshape(1, D_MODEL), lp["ln1_b"].reshape(1, D_MODEL)]
    gaw, gmw = 3 * NSA_Q_W, N_BRANCH * D_MODEL
    return pl.pallas_call(
        _merge_kernel,
        grid=(m // tm,),
        in_specs=[row(D_MODEL), row(NSA_Q_W), row(NSA_Q_W), row(NSA_Q_W), row(MOBA_Q_W), row(S5_WIDTH),
                  pl.BlockSpec((tm, gaw), lambda i: (i, PC_GA // gaw)),
                  pl.BlockSpec((tm, gmw), lambda i: (i, PC_GM // gmw))] + [full(a) for a in ws],
        out_specs=row(D_MODEL),
        out_shape=jax.ShapeDtypeStruct((m, D_MODEL), F32),
        compiler_params=_cparams("parallel"),
        name="merge",
    )(x2, o_c, o_s, o_w, o_m, y5, proj2, proj2, *ws)


def _router_kernel(x_ref, w_ref, b_ref, o_ref):
    logits = _dot3(x_ref[...], w_ref[...]) + b_ref[...]
    lane = lax.broadcasted_iota(jnp.int32, logits.shape, 1)
    lanef = lane.astype(F32)
    logits = jnp.where(lane < N_EXPERTS, logits, -jnp.inf)
    sel = _top_select(logits, lanef, MOE_TOP_K)
    m = jnp.max(logits, axis=-1, keepdims=True)
    e = jnp.where(sel > 0.5, jnp.exp(logits - m), 0.0)
    o_ref[...] = e / jnp.sum(e, axis=-1, keepdims=True)


def _router(x2, w_router, b_router, tm):
    m = x2.shape[0]
    w = jnp.pad(w_router, ((0, 0), (0, LANES - N_EXPERTS)))
    bias = jnp.pad(b_router, (0, LANES - N_EXPERTS)).reshape(1, LANES)
    return pl.pallas_call(
        _router_kernel,
        grid=(m // tm,),
        in_specs=[pl.BlockSpec((tm, D_MODEL), lambda i: (i, 0)),
                  pl.BlockSpec(w.shape, lambda i: (0, 0)), pl.BlockSpec(bias.shape, lambda i: (0, 0))],
        out_specs=pl.BlockSpec((tm, LANES), lambda i: (i, 0)),
        out_shape=jax.ShapeDtypeStruct((m, LANES), F32),
        compiler_params=_cparams("parallel"),
        name="router",
    )(x2, w, bias)


def _ffn_kernel(*refs, use_comb):
    if use_comb:
        x_ref, comb_ref, wg_ref, wu_ref, wd_ref, g_ref, b_ref, o_ref, acc_sc = refs
    else:
        x_ref, wg_ref, wu_ref, wd_ref, g_ref, b_ref, o_ref, acc_sc = refs
    f = pl.program_id(1)

    @pl.when(f == 0)
    def _():
        acc_sc[...] = jnp.zeros(acc_sc.shape, F32)

    xb = _bf(x_ref[...])
    wg = wg_ref[0] if use_comb else wg_ref[...]
    wu = wu_ref[0] if use_comb else wu_ref[...]
    wd = wd_ref[0] if use_comb else wd_ref[...]
    gate = jnp.dot(xb, _bf(wg), preferred_element_type=F32)
    up = jnp.dot(xb, _bf(wu), preferred_element_type=F32)
    h = gate * _sigmoid(gate) * up
    if use_comb:
        comb = comb_ref[...]
        lane = lax.broadcasted_iota(jnp.int32, comb.shape, 1)
        h = h * jnp.sum(jnp.where(lane == f, comb, 0.0), axis=-1, keepdims=True)
    acc_sc[...] += jnp.dot(_bf(h), _bf(wd), preferred_element_type=F32)

    @pl.when(f == pl.num_programs(1) - 1)
    def _():
        o_ref[...] = _layer_norm(DN_ALPHA * x_ref[...] + acc_sc[...], g_ref[...], b_ref[...])


def _ffn(x2, comb, wg, wu, wd, ln_g, ln_b, tm, tf):
    m = x2.shape[0]
    use_comb = comb is not None
    if use_comb:
        n_f = wg.shape[0]
        w_specs = [pl.BlockSpec((1,) + wg.shape[1:], lambda i, f: (f, 0, 0)),
                   pl.BlockSpec((1,) + wu.shape[1:], lambda i, f: (f, 0, 0)),
                   pl.BlockSpec((1,) + wd.shape[1:], lambda i, f: (f, 0, 0))]
    else:
        n_f = wg.shape[1] // tf
        w_specs = [pl.BlockSpec((D_MODEL, tf), lambda i, f: (0, f)),
                   pl.BlockSpec((D_MODEL, tf), lambda i, f: (0, f)),
                   pl.BlockSpec((tf, D_MODEL), lambda i, f: (f, 0))]
    vec = pl.BlockSpec((1, D_MODEL), lambda i, f: (0, 0))
    in_specs = [pl.BlockSpec((tm, D_MODEL), lambda i, f: (i, 0))]
    args = [x2]
    if use_comb:
        in_specs.append(pl.BlockSpec((tm, LANES), lambda i, f: (i, 0)))
        args.append(comb)
    return pl.pallas_call(
        functools.partial(_ffn_kernel, use_comb=use_comb),
        grid=(m // tm, n_f),
        in_specs=in_specs + w_specs + [vec, vec],
        out_specs=pl.BlockSpec((tm, D_MODEL), lambda i, f: (i, 0)),
        out_shape=jax.ShapeDtypeStruct((m, D_MODEL), F32),
        scratch_shapes=[pltpu.VMEM((tm, D_MODEL), F32)],
        compiler_params=_cparams("parallel", "arbitrary"),
        name="ffn",
    )(*args, wg.astype(BF16), wu.astype(BF16), wd.astype(BF16), ln_g.reshape(1, D_MODEL), ln_b.reshape(1, D_MODEL))


def _token_mixer(x3, t_real, qpos0, past, lp, sp, h0_re, h0_im):
    b, tp, _ = x3.shape
    rows = b * tp
    x2 = x3.reshape(rows, D_MODEL)
    proj2 = _matmul(x2, lp["w_proj"], min(256, rows), PC_END // 2)
    proj3 = proj2.reshape(b, tp, PC_END)
    kva = proj3[:, :t_real, PC_KVA:PC_END]
    new_cmp, new_sel, new_win = (kva[:, :, i * 2 * NSA_KV_W:(i + 1) * 2 * NSA_KV_W] for i in range(3))
    new_moba = proj3[:, :t_real, PC_KVB:PC_KVA]
    cw = _cmp_weights(lp["phi1_k"], lp["phi1_v"])
    cmp_args = (lp["pe_k"], lp["pe_v"], lp["phi1_k"], lp["phi1_v"], lp["phi2_k"], lp["phi2_v"])
    rw_a, rw_b = 2 * NSA_KV_W, 2 * MOBA_KV_W

    qa_col = PC_QA // (NSA_HEADS * NSA_KV_W)
    qb_col = PC_QB // (MOBA_HEADS * MOBA_KV_W)
    if past is None:
        assert qpos0 == 0 and b == 1 and t_real == tp and t_real % MOBA_BLOCK == 0
        tq_sel = 256 if tp % 256 == 0 else 128
        tq_moba = 512 if tp % 512 == 0 else tq_sel
        tq = 256 if tp % 256 == 0 else 128
        tk = 512 if tp % 512 == 0 else tp
        ab = _cmp_ab_linear(new_cmp.reshape(t_real // NSA_CMP_STRIDE, NSA_CMP_STRIDE * rw_a), cw)
        kc = _cmp_finish(ab, 1, *cmp_args)
        n_kt = tp // tk
        assert n_kt <= LANES
        o_c, sel_mask, act = _cmp_attention(proj3, kc, 0, t_real, tq_sel, act_blocks=tk // NSA_SEL_BLOCK)
        act = (act[0, :, 0, :n_kt] > 0.5).reshape(tp // tq, tq // tq_sel, n_kt).any(axis=1)
        o_s = _flash_worklist(proj3, qa_col, proj3, (PC_KVA + rw_a) // rw_a, sel_mask, act, _attn_cfg("sel"),
                              tq=tq, tk=tk)
        n_win = min(n_kt, (NSA_WINDOW + tq - 2) // tk + 2)
        o_w = _flash_linear(proj3, qa_col, proj3, (PC_KVA + 2 * rw_a) // rw_a, None,
                            _attn_cfg("win"), tq=tq, tk=tk, qpos0=0, kpos0=0, n_steps=n_win)
        kmean = _moba_kmean(proj3, None)
        mb_bias = _moba_gate(proj3, kmean, 0, t_real, tq_moba, True)
        v_t = proj3[:, :, PC_KVB + MOBA_KV_W:PC_KVA].transpose(0, 2, 1)
        o_m = _flash_transposed(proj3, qb_col, proj3, PC_KVB // MOBA_KV_W, v_t, mb_bias, _attn_cfg("moba"),
                                tq=tq_moba, tk=MOBA_BLOCK, n_steps=tp // MOBA_BLOCK)
        y5, h_re, h_im = _s5_prompt(proj2[:, PC_U:PC_U + S5_WIDTH], sp)
        keep = min(NSA_WINDOW, t_real)
        win_state = new_win[:, t_real - keep:]
    else:
        assert t_real < NSA_CMP_STRIDE and qpos0 % MOBA_BLOCK == 0 and tp % SUBLANES == 0
        pt, base = past["page_table"], past["page_base"]
        ab = _cmp_ab_paged(past["cmp_t"], base, pt, cw)
        kc = _cmp_finish(ab, b, *cmp_args)
        o_c, sel_mask, _ = _cmp_attention(proj3, kc, qpos0, t_real, tp)
        o_s = _decode_attn(proj3, qa_col, (PC_KVA + rw_a) // rw_a, past["sel_t"], pt, base, sel_mask,
                           _attn_cfg("sel"), qpos0=qpos0, kpos0=0)
        wb = past["win_t"].shape[2]
        o_w = _decode_attn(proj3, qa_col, (PC_KVA + 2 * rw_a) // rw_a, past["win_t"],
                           jnp.arange(b, dtype=jnp.int32).reshape(b, 1), past["win_base"], None,
                           _attn_cfg("win"), qpos0=qpos0, kpos0=qpos0 - wb)
        kmean = _moba_kmean(past["moba_t"], pt, base)
        mb_mask = _moba_gate(proj3, kmean, qpos0, t_real, tp, False)
        o_m = _decode_attn(proj3, qb_col, PC_KVB // rw_b, past["moba_t"], pt, base, mb_mask,
                           _attn_cfg("moba"), qpos0=qpos0, kpos0=0)
        y5r, h_re, h_im = _s5_sample(proj3[:, :t_real, PC_U:PC_U + S5_WIDTH], h0_re, h0_im, sp)
        y5 = jnp.pad(y5r, ((0, 0), (0, tp - t_real), (0, 0))).reshape(rows, S5_WIDTH)
        win_state = jnp.concatenate([past["win_rows"], new_win], axis=1)[:, -wb:]

    flat = lambda a: a.reshape(rows, a.shape[-1])
    x1 = _merge(x2, flat(o_c), flat(o_s), flat(o_w), flat(o_m), y5, proj2, lp, min(256, rows))
    return x1, (new_cmp, new_sel, win_state, new_moba, h_re, h_im)


def _layer(l, x3, t_real, qpos0, past, h0_re, h0_im, lp, sp, ffn_p):
    b, tp, _ = x3.shape
    x1, st = _token_mixer(x3, t_real, qpos0, past, lp, sp, h0_re, h0_im)
    tm = min(512, x1.shape[0])
    if l % 2 == 0:
        x2 = _ffn(x1, None, ffn_p["wg"], ffn_p["wu"], ffn_p["wd"], lp["ln2_g"], lp["ln2_b"], tm, ffn_p["tf"])
    else:
        comb = _router(x1, ffn_p["w_router"], ffn_p["b_router"], tm)
        x2 = _ffn(x1, comb, ffn_p["wg"], ffn_p["wu"], ffn_p["wd"], lp["ln2_g"], lp["ln2_b"], tm, None)
    return x2.reshape(b, tp, D_MODEL), st


def kernel(x_prompt, x_sample, cache_nsa_cmp, cache_nsa_sel, cache_nsa_win, cache_moba, state_s5_re, state_s5_im,
           page_table, w_in, pe_k, phi1_k, phi2_k, pe_v, phi1_v, phi2_v, s5_lambda_re, s5_lambda_im, s5_log_dt,
           s5_b_re, s5_b_im, s5_c_re, s5_c_im, s5_d, s5_w_glu, s5_b_glu, w_br_nsa, w_br_moba, w_br_s5, w_out,
           ln1_g, ln1_b, ln2_g, ln2_b, ffn_w_gate, ffn_w_up, ffn_w_down, moe_w_router, moe_b_router,
           moe_w_gate, moe_w_up, moe_w_down):
    depth = w_in.shape[0]
    bp, seq, _ = x_prompt.shape
    bs, dec_seq, _ = x_sample.shape
    n_pool = cache_moba.shape[1]
    past_len = page_table.shape[1] * PAGE_SIZE
    tps = _round_up(dec_seq, SUBLANES)
    y_p = x_prompt
    y_s = jnp.pad(x_sample, ((0, 0), (0, tps - dec_seq), (0, 0)))
    rw_a, rw_b = 2 * NSA_KV_W, 2 * MOBA_KV_W

    def tiles_t(cache):
        d, n, r = cache.shape[:3]
        return cache.transpose(0, 1, 3, 4, 5, 2).reshape(d * n, -1, r)

    cmp_t, sel_t, moba_t, win_t = (tiles_t(c) for c in (cache_nsa_cmp, cache_nsa_sel, cache_moba, cache_nsa_win))
    p_st, s_st = [], []
    for l in range(depth):
        lp = dict(w_proj=_proj_weight(w_in[l]), pe_k=pe_k[l], phi1_k=phi1_k[l], phi2_k=phi2_k[l], pe_v=pe_v[l],
                  phi1_v=phi1_v[l], phi2_v=phi2_v[l], s5_lambda_re=s5_lambda_re[l], s5_lambda_im=s5_lambda_im[l],
                  s5_log_dt=s5_log_dt[l], s5_b_re=s5_b_re[l], s5_b_im=s5_b_im[l], s5_c_re=s5_c_re[l],
                  s5_c_im=s5_c_im[l], s5_d=s5_d[l], s5_w_glu=s5_w_glu[l], s5_b_glu=s5_b_glu[l],
                  w_br_nsa=w_br_nsa[l], w_br_moba=w_br_moba[l], w_br_s5=w_br_s5[l], w_out=w_out[l],
                  ln1_g=ln1_g[l], ln1_b=ln1_b[l], ln2_g=ln2_g[l], ln2_b=ln2_b[l])
        sp = _s5_params(lp)
        if l % 2 == 0:
            d_ff = ffn_w_gate.shape[2]
            tf = d_ff // 2 if (d_ff // 2) % LANES == 0 else d_ff
            ffn_p = dict(wg=ffn_w_gate[l // 2], wu=ffn_w_up[l // 2], wd=ffn_w_down[l // 2], tf=tf)
        else:
            ffn_p = dict(wg=moe_w_gate[l // 2], wu=moe_w_up[l // 2], wd=moe_w_down[l // 2],
                         w_router=moe_w_router[l // 2], b_router=moe_b_router[l // 2])
        y_p, st = _layer(l, y_p, seq, 0, None, None, None, lp, sp, ffn_p)
        p_st.append(st)
        past = dict(page_table=page_table, page_base=l * n_pool, cmp_t=cmp_t, sel_t=sel_t, moba_t=moba_t,
                    win_t=win_t, win_base=l * bs, win_rows=cache_nsa_win[l].reshape(bs, -1, rw_a))
        y_s, st = _layer(l, y_s, dec_seq, past_len, past, state_s5_re[l].reshape(bs, S5_LANES),
                         state_s5_im[l].reshape(bs, S5_LANES), lp, sp, ffn_p)
        s_st.append(st)

    def stk(states, j, tail):
        return jnp.stack([st[j].reshape(st[j].shape[:2] + tail) if tail else st[j] for st in states], axis=0)

    row_a = (2, NSA_KV_HEADS, HEAD_DIM)
    row_b = (2, MOBA_KV_HEADS, HEAD_DIM)

    def s5_state(states, j, b):
        return jnp.stack([st[j].reshape(b, S5_GROUPS, S5_STATE) for st in states], axis=0)

    return (y_p, y_s[:, :dec_seq],
            stk(p_st, 0, row_a), stk(p_st, 1, row_a), stk(p_st, 2, row_a), stk(p_st, 3, row_b),
            s5_state(p_st, 4, bp), s5_state(p_st, 5, bp),
            stk(s_st, 0, row_a), stk(s_st, 1, row_a), stk(s_st, 2, row_a), stk(s_st, 3, row_b),
            s5_state(s_st, 4, bs), s5_state(s_st, 5, bs))
```

```python
import functools
import math

import numpy as np
import jax
import jax.numpy as jnp
from jax import lax
from jax.experimental import pallas as pl
from jax.experimental.pallas import tpu as pltpu

F32 = jnp.float32
BF16 = jnp.bfloat16

D_MODEL = 1024
HEAD_DIM = 64
PAGE_SIZE = 128
NSA_HEADS = 8
NSA_KV_HEADS = 2
NSA_CMP_LEN = 32
NSA_CMP_STRIDE = 16
NSA_PHI_HIDDEN = 128
NSA_SEL_BLOCK = 64
NSA_SEL_TOPK = 16
NSA_N_LOCAL = 2
NSA_WINDOW = 512
MOBA_HEADS = 8
MOBA_KV_HEADS = 4
MOBA_BLOCK = 256
MOBA_TOPK = 3
S5_GROUPS = 32
S5_GROUP_CH = 16
S5_WIDTH = S5_GROUPS * S5_GROUP_CH
S5_STATE = 64
S5_LANES = S5_GROUPS * S5_STATE
N_BRANCH = 3
N_EXPERTS = 8
MOE_TOP_K = 2
LN_EPS = 1e-5
DEPTH = 2
DN_ALPHA = (2.0 * DEPTH) ** 0.25

NSA_Q_W = NSA_HEADS * HEAD_DIM
NSA_KV_W = NSA_KV_HEADS * HEAD_DIM
MOBA_Q_W = MOBA_HEADS * HEAD_DIM
MOBA_KV_W = MOBA_KV_HEADS * HEAD_DIM
IN_SPLITS = (NSA_Q_W, 6 * NSA_KV_W, 3 * NSA_HEADS, MOBA_Q_W, 2 * MOBA_KV_W, S5_WIDTH, N_BRANCH * D_MODEL)
IN_OFFS = tuple(int(v) for v in np.cumsum((0,) + IN_SPLITS))

LANES = 128
SUBLANES = 8
VMEM_LIMIT = 56 * 1024 * 1024

PC_QB = 0
PC_QA = PC_QB + MOBA_HEADS * MOBA_KV_W
PC_GM = PC_QA + NSA_HEADS * NSA_KV_W
PC_GA = PC_GM + N_BRANCH * D_MODEL
PC_U = PC_GA + 3 * NSA_Q_W
PC_KVB = PC_U + S5_WIDTH
PC_KVA = PC_KVB + 2 * MOBA_KV_W
PC_END = PC_KVA + 6 * NSA_KV_W

NEG = -1e30
LOG2E = 1.0 / math.log(2.0)
SLOPES_A = tuple(float(2.0 ** (-8.0 * (i + 1) / NSA_HEADS)) * LOG2E for i in range(NSA_HEADS))
SLOPES_B = tuple(float(2.0 ** (-8.0 * (i + 1) / MOBA_HEADS)) * LOG2E for i in range(MOBA_HEADS))


def _cparams(*sem):
    return pltpu.CompilerParams(dimension_semantics=sem, vmem_limit_bytes=VMEM_LIMIT)


def _round_up(n, m):
    return -(-n // m) * m


def _bf(x):
    return x.astype(BF16)


def _dot(a, b):
    return jnp.dot(_bf(a), _bf(b), preferred_element_type=F32)


def _dot_nt(a, b):
    return lax.dot_general(_bf(a), _bf(b), (((1,), (1,)), ((), ())), preferred_element_type=F32)


def _split(a):
    hi = a.astype(BF16)
    lo = (a - hi.astype(F32)).astype(BF16)
    return hi, lo


def _dot3(a, b):
    ah, al = _split(a)
    bh, bl = _split(b)
    d = functools.partial(jnp.dot, preferred_element_type=F32)
    return d(ah, bh) + (d(ah, bl) + d(al, bh))


def _dot3_nt(a, b):
    ah, al = _split(a)
    bh, bl = _split(b)
    d = functools.partial(lax.dot_general, dimension_numbers=(((1,), (1,)), ((), ())),
                          preferred_element_type=F32)
    return d(ah, bh) + (d(ah, bl) + d(al, bh))


def _dot2_exact_rhs(a, b_bf16):
    ah, al = _split(a)
    d = functools.partial(jnp.dot, preferred_element_type=F32)
    return d(ah, b_bf16) + d(al, b_bf16)


def _sigmoid(x):
    return 1.0 / (1.0 + jnp.exp(-x))


def _gelu(x):
    c = math.sqrt(2.0 / math.pi)
    return x * (0.5 * (1.0 + jnp.tanh(c * (x + 0.044715 * (x * x * x)))))


def _layer_norm(v, g, b):
    mu = jnp.mean(v, axis=-1, keepdims=True)
    vc = v - mu
    var = jnp.mean(vc * vc, axis=-1, keepdims=True)
    return vc * lax.rsqrt(var + LN_EPS) * g + b


def _mm_kernel(x_ref, w_ref, o_ref):
    o_ref[...] = jnp.dot(_bf(x_ref[...]), w_ref[...], preferred_element_type=F32)


def _matmul(x, w, tm, tn):
    m, k = x.shape
    n = w.shape[1]
    return pl.pallas_call(
        _mm_kernel,
        grid=(n // tn, m // tm),
        in_specs=[pl.BlockSpec((tm, k), lambda j, i: (i, 0)),
                  pl.BlockSpec((k, tn), lambda j, i: (0, j))],
        out_specs=pl.BlockSpec((tm, tn), lambda j, i: (i, j)),
        out_shape=jax.ShapeDtypeStruct((m, n), F32),
        compiler_params=_cparams("parallel", "arbitrary"),
        name="proj",
    )(x, w)


def _proj_weight(w_in):
    scale = HEAD_DIM ** -0.5 * LOG2E
    qa = w_in[:, IN_OFFS[0]:IN_OFFS[1]].reshape(D_MODEL, NSA_HEADS, 1, HEAD_DIM) * scale
    oh_a = np.zeros((NSA_HEADS, NSA_KV_HEADS, 1), np.float32)
    for h in range(NSA_HEADS):
        oh_a[h, h // (NSA_HEADS // NSA_KV_HEADS)] = 1.0
    qa = (qa * oh_a[None]).reshape(D_MODEL, NSA_HEADS * NSA_KV_W)
    qb = w_in[:, IN_OFFS[3]:IN_OFFS[4]].reshape(D_MODEL, MOBA_HEADS, 1, HEAD_DIM) * scale
    oh_b = np.zeros((MOBA_HEADS, MOBA_KV_HEADS, 1), np.float32)
    for h in range(MOBA_HEADS):
        oh_b[h, h // (MOBA_HEADS // MOBA_KV_HEADS)] = 1.0
    qb = (qb * oh_b[None]).reshape(D_MODEL, MOBA_HEADS * MOBA_KV_W)
    ga = w_in[:, IN_OFFS[2]:IN_OFFS[3]].reshape(D_MODEL, NSA_HEADS, 3).transpose(0, 2, 1)
    ga = jnp.broadcast_to(ga[..., None], (D_MODEL, 3, NSA_HEADS, HEAD_DIM)).reshape(D_MODEL, 3 * NSA_Q_W)
    gm = w_in[:, IN_OFFS[6]:IN_OFFS[7]]
    u = w_in[:, IN_OFFS[5]:IN_OFFS[6]]
    kvb = w_in[:, IN_OFFS[4]:IN_OFFS[5]]
    kva = w_in[:, IN_OFFS[1]:IN_OFFS[2]]
    return jnp.concatenate([qb, qa, gm, ga, u, kvb, kva], axis=1).astype(BF16)


def _cmpab_kernel(x_ref, w_ref, o_ref):
    row_w = 2 * NSA_KV_W
    x = x_ref[...]
    for kv in range(2):
        xk = jnp.concatenate(
            [x[:, r * row_w + kv * NSA_KV_W: r * row_w + (kv + 1) * NSA_KV_W] for r in range(NSA_CMP_STRIDE)],
            axis=1)
        o_ref[:, kv * 4 * NSA_PHI_HIDDEN:(kv + 1) * 4 * NSA_PHI_HIDDEN] = _dot(xk, w_ref[kv])


def _cmp_weights(phi1_k, phi1_v):
    eye = np.eye(NSA_KV_HEADS, dtype=np.float32)
    out = []
    for phi in (phi1_k, phi1_v):
        halves = []
        for half in range(NSA_CMP_LEN // NSA_CMP_STRIDE):
            w = phi[half * NSA_CMP_STRIDE * HEAD_DIM:(half + 1) * NSA_CMP_STRIDE * HEAD_DIM]
            w = w.reshape(NSA_CMP_STRIDE, 1, HEAD_DIM, 1, NSA_PHI_HIDDEN)
            w = w * eye[None, :, None, :, None]
            halves.append(w.reshape(NSA_CMP_STRIDE * NSA_KV_W, NSA_KV_HEADS * NSA_PHI_HIDDEN))
        out.append(jnp.concatenate(halves, axis=1))
    return jnp.stack(out).astype(BF16)


def _cmp_ab_linear(x, w):
    mc = x.shape[0]
    tmc = min(mc, 256)
    n_out = 8 * NSA_PHI_HIDDEN
    return pl.pallas_call(
        _cmpab_kernel,
        grid=(mc // tmc,),
        in_specs=[pl.BlockSpec((tmc, x.shape[1]), lambda i: (i, 0)),
                  pl.BlockSpec(w.shape, lambda i: (0, 0, 0))],
        out_specs=pl.BlockSpec((tmc, n_out), lambda i: (i, 0)),
        out_shape=jax.ShapeDtypeStruct((mc, n_out), F32),
        compiler_params=_cparams("parallel"),
        name="cmp_ab",
    )(x, w)


CMP_PAGES_PER_STEP = 16


def _cmpab_t_kernel(*refs, npp):
    page_refs, w_ref, o_ref, pk_sc, pv_sc = refs[1:1 + npp], refs[1 + npp], refs[2 + npp], refs[3 + npp], refs[4 + npp]
    w = NSA_KV_W
    p = PAGE_SIZE
    eye = jnp.where(lax.broadcasted_iota(jnp.int32, (p, p), 0) == lax.broadcasted_iota(jnp.int32, (p, p), 1),
                    1.0, 0.0).astype(BF16)
    for k in range(npp):
        page_t = page_refs[k][0]
        pk_sc[k * p:(k + 1) * p, :] = _dot_nt(eye, page_t[:w])
        pv_sc[k * p:(k + 1) * p, :] = _dot_nt(eye, page_t[w:])
    m = npp * (p // NSA_CMP_STRIDE)
    n_half = 4 * NSA_PHI_HIDDEN
    for kv, sc in ((0, pk_sc), (1, pv_sc)):
        acc = None
        for r in range(0, NSA_CMP_STRIDE, 2):
            x = jnp.concatenate([sc[pl.ds(r, m, stride=NSA_CMP_STRIDE), :],
                                 sc[pl.ds(r + 1, m, stride=NSA_CMP_STRIDE), :]], axis=1)
            part = _dot(x, w_ref[kv, r // 2])
            acc = part if acc is None else acc + part
        o_ref[:, kv * n_half:(kv + 1) * n_half] = acc


def _cmp_ab_paged(pool_t, base, page_table, w):
    b, n_pages = page_table.shape
    cpp = PAGE_SIZE // NSA_CMP_STRIDE
    npp = math.gcd(CMP_PAGES_PER_STEP, n_pages)
    steps = n_pages // npp
    n_out = 8 * NSA_PHI_HIDDEN
    w4 = w.reshape(2, NSA_CMP_STRIDE // 2, 2 * NSA_KV_W, w.shape[-1])

    def x_map(k):
        return lambda bi, s, pt: (base + pt[bi * n_pages + s * npp + k], 0, 0)

    grid_spec = pltpu.PrefetchScalarGridSpec(
        num_scalar_prefetch=1,
        grid=(b, steps),
        in_specs=[pl.BlockSpec((1,) + pool_t.shape[1:], x_map(k)) for k in range(npp)]
        + [pl.BlockSpec(w4.shape, lambda bi, s, pt: (0, 0, 0, 0))],
        out_specs=pl.BlockSpec((npp * cpp, n_out), lambda bi, s, pt: (bi * steps + s, 0)),
        scratch_shapes=[pltpu.VMEM((npp * PAGE_SIZE, NSA_KV_W), F32), pltpu.VMEM((npp * PAGE_SIZE, NSA_KV_W), F32)],
    )
    return pl.pallas_call(
        functools.partial(_cmpab_t_kernel, npp=npp),
        grid_spec=grid_spec,
        out_shape=jax.ShapeDtypeStruct((b * n_pages * cpp, n_out), F32),
        compiler_params=_cparams("parallel", "arbitrary"),
        name="cmp_ab_paged",
    )(page_table.reshape(-1), *([pool_t] * npp), w4)


def _cmpfin_kernel(ab_ref, pe_ref, w1_ref, w2_ref, o_ref):
    mc = ab_ref.shape[0]
    hid2 = NSA_KV_HEADS * NSA_PHI_HIDDEN
    for kv in range(2):
        c = _dot(pe_ref[kv], w1_ref[kv])[0:1]
        c = jnp.concatenate([c] * NSA_KV_HEADS, axis=1)
        a = ab_ref[:, kv * 2 * hid2: kv * 2 * hid2 + hid2]
        bnext = pltpu.roll(ab_ref[:, kv * 2 * hid2 + hid2:(kv + 1) * 2 * hid2], mc - 1, 0)
        hpre = a + bnext + c
        o_ref[:, kv * NSA_KV_W:(kv + 1) * NSA_KV_W] = _dot(_gelu(hpre), w2_ref[kv])


def _cmp_finish(ab, nb, pe_k, pe_v, phi1_k, phi1_v, phi2_k, phi2_v):
    mc = ab.shape[0] // nb
    flat = NSA_CMP_LEN * HEAD_DIM
    pe = jnp.stack([jnp.broadcast_to(p.reshape(1, flat), (SUBLANES, flat)) for p in (pe_k, pe_v)])
    w1 = jnp.stack([phi1_k, phi1_v]).astype(BF16)
    eye = np.eye(NSA_KV_HEADS, dtype=np.float32)
    w2 = jnp.stack([(p[None, :, None, :] * eye[:, None, :, None]).reshape(NSA_KV_HEADS * NSA_PHI_HIDDEN, NSA_KV_W)
                    for p in (phi2_k, phi2_v)]).astype(BF16)
    out = pl.pallas_call(
        _cmpfin_kernel,
        grid=(nb,),
        in_specs=[pl.BlockSpec((mc, ab.shape[1]), lambda i: (i, 0)),
                  pl.BlockSpec(pe.shape, lambda i: (0, 0, 0)),
                  pl.BlockSpec(w1.shape, lambda i: (0, 0, 0)),
                  pl.BlockSpec(w2.shape, lambda i: (0, 0, 0))],
        out_specs=pl.BlockSpec((mc, 2 * NSA_KV_W), lambda i: (i, 0)),
        out_shape=jax.ShapeDtypeStruct((nb * mc, 2 * NSA_KV_W), F32),
        compiler_params=_cparams("parallel"),
        name="cmp_finish",
    )(ab, pe, w1, w2)
    return out.reshape(nb, mc, 2 * NSA_KV_W)


def _stack_heads(q, n_heads, w):
    return jnp.concatenate([_bf(q[:, h * w:(h + 1) * w]) for h in range(n_heads)], axis=0)


def _place_heads(o_heads, n_heads, n_groups, o_ref):
    hpg = n_heads // n_groups
    tq = o_heads[0].shape[0]
    lane = lax.broadcasted_iota(jnp.int32, (tq, LANES), 1)
    for pair in range(n_heads // 2):
        pieces = []
        for h in (2 * pair, 2 * pair + 1):
            src = (h // hpg) * HEAD_DIM
            piece = o_heads[h][:, (src // LANES) * LANES:(src // LANES + 1) * LANES]
            if src % LANES != (h % 2) * HEAD_DIM:
                piece = pltpu.roll(piece, HEAD_DIM, 1)
            pieces.append(piece)
        o_ref[0, :, pair * LANES:(pair + 1) * LANES] = jnp.where(lane < HEAD_DIM, pieces[0], pieces[1])


def _top_select(score, lanef, n_pick):
    sel = jnp.zeros(score.shape, F32)
    for _ in range(n_pick):
        m = jnp.max(score, axis=-1, keepdims=True)
        idx = jnp.min(jnp.where(score == m, lanef, 1e9), axis=-1, keepdims=True)
        pick = lanef == idx
        sel = jnp.where(pick & (m > -jnp.inf), 1.0, sel)
        score = jnp.where(pick, -jnp.inf, score)
    return sel


def _cmpattn_kernel(q_ref, kc_ref, ov_ref, o_ref, sel_ref, act_ref, *, tq, qpos0, n_cmp, n_sel_blk, act_blocks,
                    bias_t):
    w = NSA_KV_W
    hpg = NSA_HEADS // NSA_KV_HEADS
    ncp = kc_ref.shape[1]
    nbp = ov_ref.shape[1]
    qbase = qpos0 + pl.program_id(1) * tq
    qs = _stack_heads(q_ref[0], NSA_HEADS, w)
    kc = kc_ref[0]
    s_all = _dot_nt(qs, kc[:, :w])
    c_idx = lax.broadcasted_iota(jnp.int32, (1, ncp), 1)
    row = lax.broadcasted_iota(jnp.int32, (tq, 1), 0)
    cend_rel = c_idx * NSA_CMP_STRIDE + (NSA_CMP_LEN - 1) - qbase
    valid = (row - cend_rel >= 0) & (c_idx < n_cmp)
    colf = cend_rel.astype(F32)
    ps, psums = [], []
    for h in range(NSA_HEADS):
        s = jnp.where(valid, s_all[h * tq:(h + 1) * tq] + SLOPES_A[h] * colf, NEG)
        m = jnp.max(s, axis=-1, keepdims=True)
        e = jnp.where(valid, jnp.exp2(s - m), 0.0)
        p = e / jnp.maximum(jnp.sum(e, axis=-1, keepdims=True), 1e-30)
        ps.append(_bf(p))
        if h % hpg == 0:
            psums.append(p)
        else:
            psums[-1] = psums[-1] + p
    o_all = jnp.dot(jnp.concatenate(ps, axis=0), _bf(kc[:, w:]), preferred_element_type=F32)
    _place_heads([o_all[h * tq:(h + 1) * tq] for h in range(NSA_HEADS)], NSA_HEADS, NSA_KV_HEADS, o_ref)

    lane = lax.broadcasted_iota(jnp.int32, (tq, nbp), 1)
    lanef = lane.astype(F32)
    cur = (qbase + row) // NSA_SEL_BLOCK
    forced = (lane == 0) | ((lane <= cur) & (lane > cur - NSA_N_LOCAL))
    validb = (lane <= cur) & (lane < n_sel_blk)
    n_top = min(NSA_SEL_TOPK, n_sel_blk)
    scores = [jnp.where(forced, jnp.inf, jnp.where(validb, _dot2_exact_rhs(psums[g], ov_ref[...]), -jnp.inf))
              for g in range(NSA_KV_HEADS)]
    sel_all = _top_select(jnp.concatenate(scores, axis=0), jnp.concatenate([lanef] * NSA_KV_HEADS, axis=0), n_top)
    if bias_t:
        sel_ref[0] = jnp.where(sel_all > 0.5, 0.0, NEG).T
    else:
        for g in range(NSA_KV_HEADS):
            sel_ref[0, g] = sel_all[g * tq:(g + 1) * tq]
    any_blk = jnp.max(sel_all, axis=0, keepdims=True)
    pool = jnp.where((lax.broadcasted_iota(jnp.int32, (nbp, LANES), 0) // act_blocks)
                     == lax.broadcasted_iota(jnp.int32, (nbp, LANES), 1), 1.0, 0.0).astype(BF16)
    act_ref[0, 0] = jnp.dot(_bf(jnp.broadcast_to(any_blk, (SUBLANES, nbp))), pool, preferred_element_type=F32)


def _cmp_attention(proj3, kc, qpos0, t_real, tq, act_blocks=SUBLANES, bias_t=False):
    b, tp, _ = proj3.shape
    ncp = kc.shape[1]
    seq = qpos0 + t_real
    n_cmp = seq // NSA_CMP_STRIDE - NSA_CMP_LEN // NSA_CMP_STRIDE + 1
    n_sel_blk = -(-seq // NSA_SEL_BLOCK)
    nbp = _round_up(n_sel_blk, LANES)
    cs = np.arange(ncp)[:, None] * NSA_CMP_STRIDE
    ss = np.arange(nbp)[None, :] * NSA_SEL_BLOCK
    ov = ((cs + NSA_CMP_LEN > ss) & (cs < ss + NSA_SEL_BLOCK)
          & (np.arange(ncp)[:, None] < n_cmp) & (np.arange(nbp)[None, :] < n_sel_blk))
    ov = jnp.asarray(ov.astype(np.float32), BF16)
    qw = NSA_HEADS * NSA_KV_W
    if bias_t:
        assert b == 1
        sel_spec = pl.BlockSpec((1, nbp, NSA_KV_HEADS * tq), lambda bi, i: (i, 0, 0))
        sel_shape = jax.ShapeDtypeStruct((tp // tq, nbp, NSA_KV_HEADS * tq), F32)
    else:
        sel_spec = pl.BlockSpec((1, NSA_KV_HEADS, tq, nbp), lambda bi, i: (bi, 0, i, 0))
        sel_shape = jax.ShapeDtypeStruct((b, NSA_KV_HEADS, tp, nbp), F32)
    return pl.pallas_call(
        functools.partial(_cmpattn_kernel, tq=tq, qpos0=qpos0, n_cmp=n_cmp, n_sel_blk=n_sel_blk,
                          act_blocks=act_blocks, bias_t=bias_t),
        grid=(b, tp // tq),
        in_specs=[pl.BlockSpec((1, tq, qw), lambda bi, i: (bi, i, PC_QA // qw)),
                  pl.BlockSpec((1, ncp, 2 * NSA_KV_W), lambda bi, i: (bi, 0, 0)),
                  pl.BlockSpec(ov.shape, lambda bi, i: (0, 0))],
        out_specs=[pl.BlockSpec((1, tq, NSA_Q_W), lambda bi, i: (bi, i, 0)),
                   sel_spec,
                   pl.BlockSpec((1, 1, SUBLANES, LANES), lambda bi, i: (bi, i, 0, 0))],
        out_shape=[jax.ShapeDtypeStruct((b, tp, NSA_Q_W), F32),
                   sel_shape,
                   jax.ShapeDtypeStruct((b, tp // tq, SUBLANES, LANES), F32)],
        compiler_params=_cparams("parallel", "arbitrary"),
        name="cmp_attn",
    )(proj3, kc, ov)


def _kmean_kernel(*refs, n_x):
    x_refs, o_ref = refs[:n_x], refs[n_x]
    outs = [jnp.sum(r[0], axis=0, keepdims=True) * (1.0 / MOBA_BLOCK) for r in x_refs]
    o_ref[0, 0] = outs[0] if len(outs) == 1 else jnp.concatenate(outs, axis=0)


KMEAN_BLOCKS_PER_STEP = 16


def _moba_kmean(rows3, page_table, base=0):
    w = MOBA_KV_W
    if page_table is None:
        b, t, _ = rows3.shape
        n_blk = t // MOBA_BLOCK
        bps = math.gcd(KMEAN_BLOCKS_PER_STEP, n_blk)
        steps = n_blk // bps
        out = pl.pallas_call(
            functools.partial(_kmean_kernel, n_x=bps),
            grid=(b, steps),
            in_specs=[pl.BlockSpec((1, MOBA_BLOCK, w), (lambda bi, s, k=k: (bi, s * bps + k, PC_KVB // w)))
                      for k in range(bps)],
            out_specs=pl.BlockSpec((1, 1, bps, w), lambda bi, s: (bi, s, 0, 0)),
            out_shape=jax.ShapeDtypeStruct((b, steps, bps, w), F32),
            compiler_params=_cparams("parallel", "arbitrary"),
            name="kmean",
        )(*([rows3] * bps))
        return out.reshape(b, n_blk, w)
    b, n_pages = page_table.shape
    ppb = MOBA_BLOCK // PAGE_SIZE
    n_blk = n_pages // ppb
    bps = math.gcd(KMEAN_BLOCKS_PER_STEP, n_blk)
    steps = n_blk // bps
    npp = bps * ppb

    def x_map(k):
        return lambda bi, s, pt: (base + pt[bi * n_pages + s * npp + k], 0, 0)

    grid_spec = pltpu.PrefetchScalarGridSpec(
        num_scalar_prefetch=1,
        grid=(b, steps),
        in_specs=[pl.BlockSpec((1, w, PAGE_SIZE), x_map(k)) for k in range(npp)],
        out_specs=pl.BlockSpec((1, 1, bps, w), lambda bi, s, pt: (bi, s, 0, 0)),
    )
    out = pl.pallas_call(
        functools.partial(_kmean_t_kernel, npp=npp),
        grid_spec=grid_spec,
        out_shape=jax.ShapeDtypeStruct((b, steps, bps, w), F32),
        compiler_params=_cparams("parallel", "arbitrary"),
        name="kmean_paged",
    )(page_table.reshape(-1), *([rows3] * npp))
    return out.reshape(b, n_blk, w)


def _kmean_t_kernel(*refs, npp):
    page_refs, o_ref = refs[1:1 + npp], refs[1 + npp]
    ppb = MOBA_BLOCK // PAGE_SIZE
    ones = jnp.ones((SUBLANES, PAGE_SIZE), BF16)
    d = functools.partial(lax.dot_general, dimension_numbers=(((1,), (1,)), ((), ())), preferred_element_type=F32)
    outs = []
    for i in range(npp // ppb):
        acc = None
        for k in range(ppb):
            hi, lo = _split(page_refs[i * ppb + k][0])
            part = d(ones, hi) + d(ones, lo)
            acc = part if acc is None else acc + part
        outs.append(acc[0:1] * (1.0 / MOBA_BLOCK))
    o_ref[0, 0] = outs[0] if len(outs) == 1 else jnp.concatenate(outs, axis=0)


def _mobagate_kernel(q_ref, km_ref, m_ref, *, tq, qpos0, k_top, as_bias):
    w = MOBA_KV_W
    nbp = km_ref.shape[1]
    qbase = qpos0 + pl.program_id(1) * tq
    q = q_ref[0]
    qs = jnp.concatenate([q[:, h * w:(h + 1) * w] for h in range(MOBA_HEADS)], axis=0)
    gs = _dot3_nt(qs, km_ref[0])
    rows = MOBA_HEADS * tq
    lane = lax.broadcasted_iota(jnp.int32, (rows, nbp), 1)
    own = (qbase + lax.broadcasted_iota(jnp.int32, (rows, 1), 0) % tq) // MOBA_BLOCK
    score = jnp.where(lane < own, gs, -jnp.inf)
    sel = jnp.where(lane == own, 1.0, _top_select(score, lane.astype(F32), k_top))
    if as_bias:
        m_ref[0] = jnp.where(sel > 0.5, 0.0, NEG).T
    else:
        for h in range(MOBA_HEADS):
            m_ref[0, h] = sel[h * tq:(h + 1) * tq]


def _moba_gate(proj3, kmean, qpos0, t_real, tq, as_bias):
    b, tp, _ = proj3.shape
    n_mb = -(-(qpos0 + t_real) // MOBA_BLOCK)
    nbp = _round_up(n_mb, LANES)
    km = jnp.pad(kmean, ((0, 0), (0, nbp - kmean.shape[1]), (0, 0)))
    qw = MOBA_HEADS * MOBA_KV_W
    if as_bias:
        assert b == 1
        out_spec = pl.BlockSpec((1, nbp, MOBA_HEADS * tq), lambda bi, i: (i, 0, 0))
        out_shape = jax.ShapeDtypeStruct((tp // tq, nbp, MOBA_HEADS * tq), F32)
    else:
        out_spec = pl.BlockSpec((1, MOBA_HEADS, tq, nbp), lambda bi, i: (bi, 0, i, 0))
        out_shape = jax.ShapeDtypeStruct((b, MOBA_HEADS, tp, nbp), F32)
    return pl.pallas_call(
        functools.partial(_mobagate_kernel, tq=tq, qpos0=qpos0, k_top=min(MOBA_TOPK, n_mb), as_bias=as_bias),
        grid=(b, tp // tq),
        in_specs=[pl.BlockSpec((1, tq, qw), lambda bi, i: (bi, i, PC_QB // qw)),
                  pl.BlockSpec((1, nbp, MOBA_KV_W), lambda bi, i: (bi, 0, 0))],
        out_specs=out_spec,
        out_shape=out_shape,
        compiler_params=_cparams("parallel", "arbitrary"),
        name="moba_gate",
    )(proj3, km)


def _attend_tile(qs, k, v, mask, kbase, qbase, m_sc, l_sc, acc_sc, alpha_sc, *, n_heads, n_groups, n_mask, tq,
                 blk_shift, window, slopes, causal=True):
    tk = k.shape[0]
    s_all = _dot_nt(qs, k)
    delta = qbase - kbase
    col1 = lax.broadcasted_iota(jnp.int32, (1, tk), 1)
    colf = (col1 - delta).astype(F32)
    ok = None
    if causal or window is not None:
        dd = lax.broadcasted_iota(jnp.int32, (tq, tk), 0) - lax.broadcasted_iota(jnp.int32, (tq, tk), 1)
        ok = dd >= -delta
        if window is not None:
            ok = ok & (dd < window - delta)
    if n_mask:
        nbp = mask.shape[-1]
        blk_col = (kbase + col1) >> blk_shift
        expand = jnp.where(lax.broadcasted_iota(jnp.int32, (nbp, tk), 0) == blk_col, 1.0, 0.0).astype(BF16)
        mexp = jnp.dot(_bf(mask.reshape(n_mask * tq, nbp)), expand, preferred_element_type=F32)
    hpm = n_heads // n_mask if n_mask else n_heads
    ps = []
    valid = ok
    for h in range(n_heads):
        rows = slice(h * tq, (h + 1) * tq)
        s = s_all[rows] + slopes[h] * colf
        if n_mask and h % hpm == 0:
            g = h // hpm
            valid = mexp[g * tq:(g + 1) * tq] > 0.5
            if ok is not None:
                valid = ok & valid
        if valid is not None:
            s = jnp.where(valid, s, NEG)
        m_old = m_sc[rows]
        m_new = jnp.maximum(m_old, jnp.max(s, axis=-1, keepdims=True))
        alpha_sc[rows] = jnp.exp2(m_old - m_new)
        m_sc[rows] = m_new
        ps.append(_bf(jnp.exp2(s - _lane_rep(m_new, tk))))
    p_all = jnp.concatenate(ps, axis=0)
    alpha = alpha_sc[...]
    l_sc[...] = alpha * l_sc[...] + jnp.dot(p_all, jnp.ones((tk, LANES), BF16), preferred_element_type=F32)
    acc_sc[...] = (_lane_rep(alpha, acc_sc.shape[1]) * acc_sc[...]
                   + jnp.dot(p_all, _bf(v), preferred_element_type=F32))


def _lane_rep(x, n):
    return x if n == LANES else jnp.concatenate([x] * (n // LANES), axis=1)


def _attn_finish(m_sc, l_sc, acc_sc, o_ref, n_heads, n_groups, tq):
    w = acc_sc.shape[1]
    o = jnp.where(_lane_rep(m_sc[...], w) > 0.5 * NEG, acc_sc[...] / _lane_rep(l_sc[...], w), 0.0)
    _place_heads([o[h * tq:(h + 1) * tq] for h in range(n_heads)], n_heads, n_groups, o_ref)


def _attn_init(q_ref, q_sc, m_sc, l_sc, acc_sc, n_heads, w):
    q_sc[...] = _stack_heads(q_ref[0], n_heads, w)
    m_sc[...] = jnp.full(m_sc.shape, NEG, F32)
    l_sc[...] = jnp.zeros(l_sc.shape, F32)
    acc_sc[...] = jnp.zeros(acc_sc.shape, F32)


def _kv_tile_range(qi, *, tq, tk, qpos0, kpos0, window, n_kt):
    qlo = qpos0 + qi * tq
    last = jnp.minimum((qlo + tq - 1 - kpos0) // tk, n_kt - 1)
    if window is None:
        first = 0
    else:
        first = jnp.maximum(qlo - window + 1 - kpos0, 0) // tk
    return first, last


def _flash_kernel(*refs, cfg, rng):
    if cfg["n_mask"]:
        q_ref, kv_ref, mask_ref, o_ref, q_sc, m_sc, l_sc, acc_sc, alpha_sc = refs
    else:
        q_ref, kv_ref, o_ref, q_sc, m_sc, l_sc, acc_sc, alpha_sc = refs
        mask_ref = None
    w, tq, tk = cfg["n_groups"] * HEAD_DIM, rng["tq"], rng["tk"]
    qi, j = pl.program_id(1), pl.program_id(2)
    first, last = _kv_tile_range(qi, **rng)

    @pl.when(j == 0)
    def _():
        _attn_init(q_ref, q_sc, m_sc, l_sc, acc_sc, cfg["n_heads"], w)

    jt = first + j
    kbase = rng["kpos0"] + jt * tk
    qbase = rng["qpos0"] + qi * tq

    def tile(causal):
        kv = kv_ref[0]
        _attend_tile(q_sc[...], kv[:, :w], kv[:, w:], None if mask_ref is None else mask_ref[0], kbase, qbase,
                     m_sc, l_sc, acc_sc, alpha_sc, tq=tq, causal=causal, **cfg)

    if cfg["window"] is None:
        fully_past = kbase + tk <= qbase
        pl.when((jt <= last) & fully_past)(lambda: tile(False))
        pl.when((jt <= last) & jnp.logical_not(fully_past))(lambda: tile(True))
    else:
        pl.when(jt <= last)(lambda: tile(True))

    @pl.when(j == pl.num_programs(2) - 1)
    def _():
        _attn_finish(m_sc, l_sc, acc_sc, o_ref, cfg["n_heads"], cfg["n_groups"], tq)


def _flash_linear(q_arr, q_col, kv_arr, kv_col, mask, cfg, *, tq, tk, qpos0, kpos0, n_steps):
    b, tp, _ = q_arr.shape
    w = cfg["n_groups"] * HEAD_DIM
    n_heads = cfg["n_heads"]
    n_kt = kv_arr.shape[1] // tk
    rng = dict(tq=tq, tk=tk, qpos0=qpos0, kpos0=kpos0, window=cfg["window"], n_kt=n_kt)

    def kv_map(bi, i, j):
        first, last = _kv_tile_range(i, **rng)
        return (bi, jnp.minimum(first + j, last), kv_col)

    in_specs = [pl.BlockSpec((1, tq, n_heads * w), lambda bi, i, j: (bi, i, q_col)),
                pl.BlockSpec((1, tk, 2 * w), kv_map)]
    args = [q_arr, kv_arr]
    if cfg["n_mask"]:
        in_specs.append(pl.BlockSpec((1, cfg["n_mask"], tq, mask.shape[-1]), lambda bi, i, j: (bi, 0, i, 0)))
        args.append(mask)
    return pl.pallas_call(
        functools.partial(_flash_kernel, cfg=cfg, rng=rng),
        grid=(b, tp // tq, n_steps),
        in_specs=in_specs,
        out_specs=pl.BlockSpec((1, tq, n_heads * HEAD_DIM), lambda bi, i, j: (bi, i, 0)),
        out_shape=jax.ShapeDtypeStruct((b, tp, n_heads * HEAD_DIM), F32),
        scratch_shapes=[pltpu.VMEM((n_heads * tq, w), BF16), pltpu.VMEM((n_heads * tq, LANES), F32),
                        pltpu.VMEM((n_heads * tq, LANES), F32), pltpu.VMEM((n_heads * tq, w), F32),
                        pltpu.VMEM((n_heads * tq, LANES), F32)],
        compiler_params=_cparams("parallel", "parallel", "arbitrary"),
        name="flash",
    )(*args)


def _flash_t_kernel(q_ref, k_ref, vt_ref, rb_ref, o_ref, q_sc, m_sc, l_sc, acc_sc, *, cfg, rng):
    n_heads, n_groups, slopes = cfg["n_heads"], cfg["n_groups"], cfg["slopes"]
    w, tq, tk = n_groups * HEAD_DIM, rng["tq"], rng["tk"]
    qi, j = pl.program_id(1), pl.program_id(2)
    first, last = _kv_tile_range(qi, **rng)

    @pl.when(j == 0)
    def _():
        q_sc[...] = _stack_heads(q_ref[0], n_heads, w)
        m_sc[...] = jnp.full(m_sc.shape, NEG, F32)
        l_sc[...] = jnp.zeros(l_sc.shape, F32)
        acc_sc[...] = jnp.zeros(acc_sc.shape, F32)

    jt = first + j
    kbase = rng["kpos0"] + jt * tk
    qbase = rng["qpos0"] + qi * tq

    def tile(causal):
        s_t = _dot_nt(k_ref[0], q_sc[...])
        kb = (lax.broadcasted_iota(jnp.int32, (tk, LANES), 0) + (kbase - qbase)).astype(F32)
        rb = rb_ref[0, pl.ds(jt % SUBLANES, 1), :]
        ok = _causal_t(tk, tq, qbase - kbase) if causal else None
        _t_update(s_t, kb, ok, None, rb, vt_ref, m_sc, l_sc, acc_sc,
                  n_heads=n_heads, n_groups=n_groups, slopes=slopes, tq=tq)

    fully_past = kbase + tk <= qbase
    pl.when((jt <= last) & fully_past)(lambda: tile(False))
    pl.when((jt <= last) & jnp.logical_not(fully_past))(lambda: tile(True))

    @pl.when(j == pl.num_programs(2) - 1)
    def _():
        _t_finish(m_sc, l_sc, acc_sc, o_ref, n_heads, tq)


def _causal_t(tk, tq, delta):
    return lax.broadcasted_iota(jnp.int32, (tk, tq), 0) - lax.broadcasted_iota(jnp.int32, (tk, tq), 1) <= delta


def _t_update(s_t, kb, ok, elem_bias, rb, vt_ref, m_sc, l_sc, acc_sc, *, n_heads, n_groups, slopes, tq):
    hpg = n_heads // n_groups
    vrows = HEAD_DIM + SUBLANES
    for g in range(n_groups):
        ps, alphas = [], []
        for lt in range(g * hpg * tq // LANES, (g + 1) * hpg * tq // LANES):
            h = lt * LANES // tq
            qq0 = (lt * LANES) % tq
            lanes = slice(lt * LANES, (lt + 1) * LANES)
            s = s_t[:, lanes] + slopes[h] * kb
            if elem_bias is not None:
                s = s + elem_bias[:, g * tq + qq0:g * tq + qq0 + LANES]
            if ok is not None:
                s = jnp.where(ok[:, qq0:qq0 + LANES], s, NEG)
            m_old = m_sc[:, lanes]
            m_cur = jnp.max(s, axis=0, keepdims=True)
            if rb is not None:
                m_cur = m_cur + rb[:, lanes]
            m_new = jnp.maximum(m_old, m_cur)
            alphas.append(jnp.exp2(m_old - m_new))
            m_sc[:, lanes] = m_new
            shift = jnp.where(m_new > 0.5 * NEG, m_new if rb is None else m_new - rb[:, lanes], -NEG)
            ps.append(_bf(jnp.exp2(s - shift)))
        glanes = slice(g * hpg * tq, (g + 1) * hpg * tq)
        alpha = jnp.concatenate(alphas, axis=1)
        pv = jnp.dot(_bf(vt_ref[0, g * vrows:(g + 1) * vrows, :]), jnp.concatenate(ps, axis=1),
                     preferred_element_type=F32)
        acc_sc[:, glanes] = alpha * acc_sc[:, glanes] + pv[:HEAD_DIM]
        l_sc[:, glanes] = alpha * l_sc[:, glanes] + pv[HEAD_DIM:HEAD_DIM + 1]


def _t_finish(m_sc, l_sc, acc_sc, o_ref, n_heads, tq):
    o = jnp.where(m_sc[...] > 0.5 * NEG, acc_sc[...] / l_sc[...], 0.0)
    for pair in range(n_heads // 2):
        slab = jnp.concatenate([o[:, h * tq:(h + 1) * tq] for h in (2 * pair, 2 * pair + 1)], axis=0)
        o_ref[0, :, pair * LANES:(pair + 1) * LANES] = slab.T


def _vt_with_ones(v_t, n_groups):
    tp = v_t.shape[2]
    v = jnp.concatenate([v_t.reshape(1, n_groups, HEAD_DIM, tp), jnp.ones((1, n_groups, SUBLANES, tp), F32)], axis=2)
    return v.reshape(1, n_groups * (HEAD_DIM + SUBLANES), tp)


def _flash_transposed(q_arr, q_col, k_arr, k_col, v_t, bias_t, cfg, *, tq, tk, n_steps):
    b, tp, _ = q_arr.shape
    assert b == 1 and (1 << cfg["blk_shift"]) == tk
    w = cfg["n_groups"] * HEAD_DIM
    n_heads = cfg["n_heads"]
    nq = tp // tq
    r = n_heads * tq
    assert bias_t.shape[0] == nq and bias_t.shape[2] == r
    v_t = _vt_with_ones(v_t, cfg["n_groups"])
    vrows = v_t.shape[1]
    rng = dict(tq=tq, tk=tk, qpos0=0, kpos0=0, window=None, n_kt=tp // tk)

    def tile_of(i, j):
        first, last = _kv_tile_range(i, **rng)
        return jnp.minimum(first + j, last)

    return pl.pallas_call(
        functools.partial(_flash_t_kernel, cfg=cfg, rng=rng),
        grid=(b, nq, n_steps),
        in_specs=[pl.BlockSpec((1, tq, n_heads * w), lambda bi, i, j: (bi, i, q_col)),
                  pl.BlockSpec((1, tk, w), lambda bi, i, j: (bi, tile_of(i, j), k_col)),
                  pl.BlockSpec((1, vrows, tk), lambda bi, i, j: (bi, 0, tile_of(i, j))),
                  pl.BlockSpec((1, SUBLANES, r), lambda bi, i, j: (i, tile_of(i, j) // SUBLANES, 0))],
        out_specs=pl.BlockSpec((1, tq, n_heads * HEAD_DIM), lambda bi, i, j: (bi, i, 0)),
        out_shape=jax.ShapeDtypeStruct((b, tp, n_heads * HEAD_DIM), F32),
        scratch_shapes=[pltpu.VMEM((r, w), BF16), pltpu.VMEM((1, r), F32), pltpu.VMEM((1, r), F32),
                        pltpu.VMEM((HEAD_DIM, r), F32)],
        compiler_params=_cparams("parallel", "parallel", "arbitrary"),
        name="flash_t",
    )(q_arr, k_arr, v_t, bias_t)


def _worklist(act, tq, tk):
    nq, n_kt = act.shape
    i = np.arange(nq)[:, None]
    jt = np.arange(n_kt)[None, :]
    diag = np.minimum((i * tq + tq - 1) // tk, n_kt - 1)
    causal = jt <= diag
    s_max = int(causal.sum())
    need = (act | jnp.asarray(jt == diag)) & jnp.asarray(causal)
    key = jnp.where(need, jnp.asarray(i * n_kt + jt, jnp.int32), nq * n_kt).reshape(-1)
    key = jnp.sort(key)[:s_max]
    n_need = jnp.sum(need.astype(jnp.int32))
    pos = jnp.arange(s_max, dtype=jnp.int32)
    valid = pos < n_need
    key = jnp.where(valid, key, jnp.take(key, n_need - 1))
    qi, kt = key // n_kt, key % n_kt
    first = valid & ((pos == 0) | (qi != jnp.roll(qi, 1)))
    last = valid & ((pos == n_need - 1) | (qi != jnp.roll(qi, -1)))
    return jnp.concatenate([qi, kt, first.astype(jnp.int32), last.astype(jnp.int32),
                            valid.astype(jnp.int32)]).astype(jnp.int32), s_max


def _flash_list_kernel(wl_ref, q_ref, k_ref, vt_ref, mt_ref, o_ref, q_sc, m_sc, l_sc, acc_sc, *,
                       cfg, tq, tk, s_max):
    s = pl.program_id(0)
    n_heads, n_groups = cfg["n_heads"], cfg["n_groups"]
    w = n_groups * HEAD_DIM
    qbase = wl_ref[s] * tq
    kbase = wl_ref[s_max + s] * tk
    valid = wl_ref[4 * s_max + s] == 1

    @pl.when(wl_ref[2 * s_max + s] == 1)
    def _():
        q_sc[...] = _stack_heads(q_ref[0], n_heads, w)
        m_sc[...] = jnp.full(m_sc.shape, NEG, F32)
        l_sc[...] = jnp.zeros(l_sc.shape, F32)
        acc_sc[...] = jnp.zeros(acc_sc.shape, F32)

    def tile(causal):
        s_t = _dot_nt(k_ref[0], q_sc[...])
        kb = (lax.broadcasted_iota(jnp.int32, (tk, LANES), 0) + (kbase - qbase)).astype(F32)
        nbp = mt_ref.shape[1]
        key_blk = (kbase + lax.broadcasted_iota(jnp.int32, (tk, nbp), 0)) >> cfg["blk_shift"]
        expand = jnp.where(key_blk == lax.broadcasted_iota(jnp.int32, (tk, nbp), 1), 1.0, 0.0).astype(BF16)
        elem_bias = jnp.dot(expand, _bf(mt_ref[0]), preferred_element_type=F32)
        ok = _causal_t(tk, tq, qbase - kbase) if causal else None
        _t_update(s_t, kb, ok, elem_bias, None, vt_ref, m_sc, l_sc, acc_sc,
                  n_heads=n_heads, n_groups=n_groups, slopes=cfg["slopes"], tq=tq)

    fully_past = kbase + tk <= qbase
    pl.when(valid & fully_past)(lambda: tile(False))
    pl.when(valid & jnp.logical_not(fully_past))(lambda: tile(True))

    @pl.when(wl_ref[3 * s_max + s] == 1)
    def _():
        _t_finish(m_sc, l_sc, acc_sc, o_ref, n_heads, tq)


def _flash_worklist(q_arr, q_col, k_arr, k_col, v_t, bias_t, act, cfg, *, tq, tk):
    b, tp, _ = q_arr.shape
    assert b == 1 and cfg["n_mask"] == cfg["n_groups"] and cfg["window"] is None
    w = cfg["n_groups"] * HEAD_DIM
    n_heads = cfg["n_heads"]
    r = n_heads * tq
    v_t = _vt_with_ones(v_t, cfg["n_groups"])
    wl, s_max = _worklist(act, tq, tk)
    grid_spec = pltpu.PrefetchScalarGridSpec(
        num_scalar_prefetch=1,
        grid=(s_max,),
        in_specs=[pl.BlockSpec((1, tq, n_heads * w), lambda s, wl: (0, wl[s], q_col)),
                  pl.BlockSpec((1, tk, w), lambda s, wl: (0, wl[s_max + s], k_col)),
                  pl.BlockSpec((1, v_t.shape[1], tk), lambda s, wl: (0, 0, wl[s_max + s])),
                  pl.BlockSpec((1,) + bias_t.shape[1:], lambda s, wl: (wl[s], 0, 0))],
        out_specs=pl.BlockSpec((1, tq, n_heads * HEAD_DIM), lambda s, wl: (0, wl[s], 0)),
        scratch_shapes=[pltpu.VMEM((r, w), BF16), pltpu.VMEM((1, r), F32), pltpu.VMEM((1, r), F32),
                        pltpu.VMEM((HEAD_DIM, r), F32)],
    )
    return pl.pallas_call(
        functools.partial(_flash_list_kernel, cfg=cfg, tq=tq, tk=tk, s_max=s_max),
        grid_spec=grid_spec,
        out_shape=jax.ShapeDtypeStruct((b, tp, n_heads * HEAD_DIM), F32),
        compiler_params=_cparams("arbitrary"),
        name="flash_list",
    )(wl, q_arr, k_arr, v_t, bias_t)


DECODE_PAGES_PER_STEP = 32


def _decode_update(qs, k_op, v_op, transposed, kbase, qpos_col, slope_col, mrow, m_sc, l_sc, acc_sc, *,
                   blk_shift, window, qpos0):
    s = jnp.dot(qs, _bf(k_op), preferred_element_type=F32) if transposed else _dot_nt(qs, k_op)
    tk = s.shape[1]
    kpos = kbase + lax.broadcasted_iota(jnp.int32, (1, tk), 1)
    d = qpos_col - kpos
    ok = d >= 0
    if window is not None:
        ok = ok & (d < window)
    if mrow is not None:
        nbp = mrow.shape[1]
        expand = jnp.where(lax.broadcasted_iota(jnp.int32, (nbp, tk), 0) == (kpos >> blk_shift), 1.0, 0.0)
        ok = ok & (jnp.dot(mrow, expand.astype(BF16), preferred_element_type=F32) > 0.5)
    s = jnp.where(ok, s + slope_col * (kpos - qpos0).astype(F32), NEG)
    m_old = m_sc[...]
    m_new = jnp.maximum(m_old, jnp.max(s, axis=-1, keepdims=True))
    alpha = jnp.exp2(m_old - m_new)
    p = _bf(jnp.exp2(s - m_new))
    l_sc[...] = alpha * l_sc[...] + jnp.sum(p.astype(F32), axis=-1, keepdims=True)
    m_sc[...] = m_new
    pv = _dot_nt(p, v_op) if transposed else jnp.dot(p, _bf(v_op), preferred_element_type=F32)
    acc_sc[...] = alpha * acc_sc[...] + pv


def _decode_kernel(*refs, cfg, npp, tq, qpos0, kpos0, tile_rows):
    has_mask = bool(cfg["n_mask"])
    page_refs = refs[2:2 + npp]
    q_ref, tail_ref = refs[1], refs[2 + npp]
    rest = refs[3 + npp:]
    if has_mask:
        mask_ref, o_ref, q_sc, m_sc, l_sc, acc_sc, slope_sc, mrow_sc = rest
    else:
        o_ref, q_sc, m_sc, l_sc, acc_sc, slope_sc = rest
    n_heads = cfg["n_heads"]
    w = cfg["n_groups"] * HEAD_DIM
    j = pl.program_id(1)
    n_steps = pl.num_programs(1)
    row = lax.broadcasted_iota(jnp.int32, (n_heads * tq, 1), 0)

    @pl.when(j == 0)
    def _():
        _attn_init(q_ref, q_sc, m_sc, l_sc, acc_sc, n_heads, w)
        slope = jnp.zeros((n_heads * tq, 1), F32)
        for h in range(n_heads):
            slope = jnp.where(row // tq == h, cfg["slopes"][h], slope)
        slope_sc[...] = slope
        if has_mask:
            hpm = n_heads // cfg["n_mask"]
            mrow_sc[...] = _bf(jnp.concatenate([mask_ref[0, h // hpm] for h in range(n_heads)], axis=0))

    qpos_col = qpos0 + row % tq
    kw = dict(blk_shift=cfg["blk_shift"], window=cfg["window"], qpos0=qpos0)
    mrow = mrow_sc[...] if has_mask else None

    @pl.when(j < n_steps - 1)
    def _():
        kt = [r[0, :w, :] for r in page_refs]
        vt = [r[0, w:, :] for r in page_refs]
        kt = kt[0] if npp == 1 else jnp.concatenate(kt, axis=1)
        vt = vt[0] if npp == 1 else jnp.concatenate(vt, axis=1)
        _decode_update(q_sc[...], kt, vt, True, kpos0 + j * (npp * tile_rows), qpos_col, slope_sc[...], mrow,
                       m_sc, l_sc, acc_sc, **kw)

    @pl.when(j == n_steps - 1)
    def _():
        tail = tail_ref[0]
        _decode_update(q_sc[...], tail[:, :w], tail[:, w:], False, qpos0, qpos_col, slope_sc[...], mrow,
                       m_sc, l_sc, acc_sc, **kw)
        o = jnp.where(m_sc[...] > 0.5 * NEG, acc_sc[...] / l_sc[...], 0.0)
        _place_heads([o[h * tq:(h + 1) * tq] for h in range(n_heads)], n_heads, cfg["n_groups"], o_ref)


def _decode_attn(proj3, q_col, tail_col, tiles_t, table, base, mask, cfg, *, qpos0, kpos0):
    b, tq, _ = proj3.shape
    w = cfg["n_groups"] * HEAD_DIM
    n_heads = cfg["n_heads"]
    n_tiles = table.shape[1]
    tile_rows = tiles_t.shape[2]
    npp = math.gcd(DECODE_PAGES_PER_STEP, n_tiles)
    steps = n_tiles // npp
    r = n_heads * tq

    def tile_map(k):
        return lambda bi, j, tb: (base + tb[bi * n_tiles + jnp.minimum(j, steps - 1) * npp + k], 0, 0)

    in_specs = ([pl.BlockSpec((1, tq, n_heads * w), lambda bi, j, tb: (bi, 0, q_col))]
                + [pl.BlockSpec((1, 2 * w, tile_rows), tile_map(k)) for k in range(npp)]
                + [pl.BlockSpec((1, tq, 2 * w), lambda bi, j, tb: (bi, 0, tail_col))])
    args = [table.reshape(-1), proj3] + [tiles_t] * npp + [proj3]
    scratch = [pltpu.VMEM((r, w), BF16), pltpu.VMEM((r, 1), F32), pltpu.VMEM((r, 1), F32), pltpu.VMEM((r, w), F32),
               pltpu.VMEM((r, 1), F32)]
    if cfg["n_mask"]:
        in_specs.append(pl.BlockSpec((1, cfg["n_mask"], tq, mask.shape[-1]), lambda bi, j, tb: (bi, 0, 0, 0)))
        args.append(mask)
        scratch.append(pltpu.VMEM((r, mask.shape[-1]), BF16))
    grid_spec = pltpu.PrefetchScalarGridSpec(
        num_scalar_prefetch=1,
        grid=(b, steps + 1),
        in_specs=in_specs,
        out_specs=pl.BlockSpec((1, tq, n_heads * HEAD_DIM), lambda bi, j, tb: (bi, 0, 0)),
        scratch_shapes=scratch,
    )
    return pl.pallas_call(
        functools.partial(_decode_kernel, cfg=cfg, npp=npp, tq=tq, qpos0=qpos0, kpos0=kpos0, tile_rows=tile_rows),
        grid_spec=grid_spec,
        out_shape=jax.ShapeDtypeStruct((b, tq, n_heads * HEAD_DIM), F32),
        compiler_params=_cparams("parallel", "arbitrary"),
        name="decode_attn",
    )(*args)


def _attn_cfg(kind):
    if kind == "moba":
        return dict(n_heads=MOBA_HEADS, n_groups=MOBA_KV_HEADS, n_mask=MOBA_HEADS,
                    blk_shift=int(math.log2(MOBA_BLOCK)), window=None, slopes=SLOPES_B)
    if kind == "sel":
        return dict(n_heads=NSA_HEADS, n_groups=NSA_KV_HEADS, n_mask=NSA_KV_HEADS,
                    blk_shift=int(math.log2(NSA_SEL_BLOCK)), window=None, slopes=SLOPES_A)
    return dict(n_heads=NSA_HEADS, n_groups=NSA_KV_HEADS, n_mask=0, blk_shift=0, window=NSA_WINDOW, slopes=SLOPES_A)


S5_CHUNK = 4 * LANES
S5_GCHUNK = S5_CHUNK // S5_STATE


def _s5_params(lp):
    lr, li = lp["s5_lambda_re"], lp["s5_lambda_im"]
    dt = jnp.exp(lp["s5_log_dt"])[:, None]
    mag = jnp.exp(lr * dt)
    a_re = mag * jnp.cos(li * dt)
    a_im = mag * jnp.sin(li * dt)
    den = lr * lr + li * li
    f_re = ((a_re - 1.0) * lr + a_im * li) / den
    f_im = (a_im * lr - (a_re - 1.0) * li) / den
    b_re, b_im = lp["s5_b_re"], lp["s5_b_im"]
    bb_re = f_re[..., None] * b_re - f_im[..., None] * b_im
    bb_im = f_re[..., None] * b_im + f_im[..., None] * b_re
    n_ch = S5_GROUPS // S5_GCHUNK
    eye = np.eye(S5_GCHUNK, dtype=np.float32)

    def in_w(bb):
        x = bb.reshape(n_ch, S5_GCHUNK, S5_STATE, S5_GROUP_CH).transpose(0, 1, 3, 2)
        x = x[:, :, :, None, :] * eye[None, :, None, :, None]
        return x.reshape(n_ch, S5_GCHUNK * S5_GROUP_CH, S5_CHUNK)

    def out_w(c):
        x = c.reshape(n_ch, S5_GCHUNK, S5_GROUP_CH, S5_STATE).transpose(0, 1, 3, 2)
        x = x[:, :, :, None, :] * eye[None, :, None, :, None]
        return x.reshape(n_ch, S5_CHUNK, S5_GCHUNK * S5_GROUP_CH).astype(BF16)

    return dict(a_re=a_re.reshape(1, S5_LANES), a_im=a_im.reshape(1, S5_LANES),
                bw_re=in_w(bb_re), bw_im=in_w(bb_im),
                cw_re=out_w(lp["s5_c_re"]), cw_im=out_w(lp["s5_c_im"]),
                d=lp["s5_d"].reshape(1, S5_WIDTH))


def _s5_out_proj(h_re, h_im, cwr_ref, cwi_ref):
    cols = []
    for c in range(S5_LANES // S5_CHUNK):
        sl = slice(c * S5_CHUNK, (c + 1) * S5_CHUNK)
        cols.append(jnp.dot(_bf(h_re[:, sl]), cwr_ref[c], preferred_element_type=F32)
                    - jnp.dot(_bf(h_im[:, sl]), cwi_ref[c], preferred_element_type=F32))
    return jnp.concatenate(cols, axis=1)


def _s5scan_kernel(u_ref, h0r_ref, h0i_ref, ar_ref, ai_ref, bwr_ref, bwi_ref, cwr_ref, cwi_ref, d_ref,
                   y_ref, hr_ref, hi_ref, bur_sc, bui_sc, hr_sc, hi_sc, *, n_par, n_j):
    @pl.when(pl.program_id(0) == 0)
    def _():
        hr_sc[...] = h0r_ref[...]
        hi_sc[...] = h0i_ref[...]

    u = u_ref[...]
    gw = S5_GCHUNK * S5_GROUP_CH
    for c in range(S5_LANES // S5_CHUNK):
        uc = u[:, c * gw:(c + 1) * gw]
        bur_sc[:, c * S5_CHUNK:(c + 1) * S5_CHUNK] = _dot3(uc, bwr_ref[c])
        bui_sc[:, c * S5_CHUNK:(c + 1) * S5_CHUNK] = _dot3(uc, bwi_ref[c])

    for c in range(S5_LANES // S5_CHUNK):
        sl = slice(c * S5_CHUNK, (c + 1) * S5_CHUNK)
        a_re = jnp.broadcast_to(ar_ref[:, sl], (n_par, S5_CHUNK))
        a_im = jnp.broadcast_to(ai_ref[:, sl], (n_par, S5_CHUNK))

        def step(jj, carry):
            h_re, h_im = carry
            rows = pl.ds(pl.multiple_of(jj * n_par, n_par), n_par)
            n_re = a_re * h_re - a_im * h_im + bur_sc[rows, sl]
            n_im = a_re * h_im + a_im * h_re + bui_sc[rows, sl]
            bur_sc[rows, sl] = n_re
            bui_sc[rows, sl] = n_im
            return n_re, n_im

        h_re, h_im = lax.fori_loop(0, n_j, step, (hr_sc[:, sl], hi_sc[:, sl]))
        hr_sc[:, sl] = h_re
        hi_sc[:, sl] = h_im

    y_ref[...] = _s5_out_proj(bur_sc[...], bui_sc[...], cwr_ref, cwi_ref) + d_ref[...] * u
    hr_ref[...] = hr_sc[...]
    hi_ref[...] = hi_sc[...]


def _s5_scan(u_rows, h0_re, h0_im, sp, n_par, jc):
    rows = u_rows.shape[0]
    n_j_total = rows // n_par
    steps = n_j_total // jc
    full = lambda a: pl.BlockSpec(a.shape, lambda i: (0,) * a.ndim)
    ins = [u_rows, h0_re, h0_im, sp["a_re"], sp["a_im"], sp["bw_re"], sp["bw_im"], sp["cw_re"], sp["cw_im"], sp["d"]]
    return pl.pallas_call(
        functools.partial(_s5scan_kernel, n_par=n_par, n_j=jc),
        grid=(steps,),
        in_specs=[pl.BlockSpec((jc * n_par, S5_WIDTH), lambda i: (i, 0))] + [full(a) for a in ins[1:]],
        out_specs=[pl.BlockSpec((jc * n_par, S5_WIDTH), lambda i: (i, 0)),
                   pl.BlockSpec((n_par, S5_LANES), lambda i: (0, 0)),
                   pl.BlockSpec((n_par, S5_LANES), lambda i: (0, 0))],
        out_shape=[jax.ShapeDtypeStruct((rows, S5_WIDTH), F32),
                   jax.ShapeDtypeStruct((n_par, S5_LANES), F32),
                   jax.ShapeDtypeStruct((n_par, S5_LANES), F32)],
        scratch_shapes=[pltpu.VMEM((jc * n_par, S5_LANES), F32), pltpu.VMEM((jc * n_par, S5_LANES), F32),
                        pltpu.VMEM((n_par, S5_LANES), F32), pltpu.VMEM((n_par, S5_LANES), F32)],
        compiler_params=_cparams("arbitrary"),
        name="s5_scan",
    )(*ins)


def _s5fix_kernel(y_ref, er_ref, ei_ref, ar_ref, ai_ref, cwr_ref, cwi_ref, o_ref, fr_ref, fi_ref,
                  cr_sc, ci_sc, pr_sc, pi_sc, xr_sc, xi_sc, *, n_seg, n_j, n_j_total):
    a_re = ar_ref[...]
    a_im = ai_ref[...]

    @pl.when(pl.program_id(0) == 0)
    def _():
        def pw_step(_, carry):
            p_re, p_im = carry
            return a_re * p_re - a_im * p_im, a_re * p_im + a_im * p_re

        al_re, al_im = lax.fori_loop(0, n_j_total - 1, pw_step, (a_re, a_im))
        c_re = jnp.zeros((1, S5_LANES), F32)
        c_im = jnp.zeros((1, S5_LANES), F32)
        for s in range(n_seg):
            cr_sc[s:s + 1, :] = c_re
            ci_sc[s:s + 1, :] = c_im
            e_re, e_im = er_ref[s:s + 1, :], ei_ref[s:s + 1, :]
            c_re, c_im = (e_re + al_re * c_re - al_im * c_im, e_im + al_re * c_im + al_im * c_re)
        fr_ref[...] = c_re
        fi_ref[...] = c_im
        pr_sc[...] = jnp.broadcast_to(a_re, pr_sc.shape)
        pi_sc[...] = jnp.broadcast_to(a_im, pi_sc.shape)

    for c in range(S5_LANES // S5_CHUNK):
        sl = slice(c * S5_CHUNK, (c + 1) * S5_CHUNK)
        ab_re = jnp.broadcast_to(a_re[:, sl], (n_seg, S5_CHUNK))
        ab_im = jnp.broadcast_to(a_im[:, sl], (n_seg, S5_CHUNK))
        c_re, c_im = cr_sc[:, sl], ci_sc[:, sl]

        def step(jj, carry):
            p_re, p_im = carry
            rows = pl.ds(pl.multiple_of(jj * n_seg, n_seg), n_seg)
            xr_sc[rows, sl] = p_re * c_re - p_im * c_im
            xi_sc[rows, sl] = p_re * c_im + p_im * c_re
            return ab_re * p_re - ab_im * p_im, ab_re * p_im + ab_im * p_re

        p_re, p_im = lax.fori_loop(0, n_j, step, (pr_sc[:, sl], pi_sc[:, sl]))
        pr_sc[:, sl] = p_re
        pi_sc[:, sl] = p_im

    o_ref[...] = y_ref[...] + _s5_out_proj(xr_sc[...], xi_sc[...], cwr_ref, cwi_ref)


def _s5_fix(y_rows, end_re, end_im, sp, n_seg, jc):
    rows = y_rows.shape[0]
    n_j_total = rows // n_seg
    steps = n_j_total // jc
    full = lambda a: pl.BlockSpec(a.shape, lambda i: (0,) * a.ndim)
    ins = [y_rows, end_re, end_im, sp["a_re"], sp["a_im"], sp["cw_re"], sp["cw_im"]]
    seg = lambda: pltpu.VMEM((n_seg, S5_LANES), F32)
    return pl.pallas_call(
        functools.partial(_s5fix_kernel, n_seg=n_seg, n_j=jc, n_j_total=n_j_total),
        grid=(steps,),
        in_specs=[pl.BlockSpec((jc * n_seg, S5_WIDTH), lambda i: (i, 0))] + [full(a) for a in ins[1:]],
        out_specs=[pl.BlockSpec((jc * n_seg, S5_WIDTH), lambda i: (i, 0)),
                   pl.BlockSpec((1, S5_LANES), lambda i: (0, 0)),
                   pl.BlockSpec((1, S5_LANES), lambda i: (0, 0))],
        out_shape=[jax.ShapeDtypeStruct((rows, S5_WIDTH), F32),
                   jax.ShapeDtypeStruct((1, S5_LANES), F32),
                   jax.ShapeDtypeStruct((1, S5_LANES), F32)],
        scratch_shapes=[seg(), seg(), seg(), seg(),
                        pltpu.VMEM((jc * n_seg, S5_LANES), F32), pltpu.VMEM((jc * n_seg, S5_LANES), F32)],
        compiler_params=_cparams("arbitrary"),
        name="s5_fix",
    )(*ins)


S5_SEGMENTS = SUBLANES


def _s5_prompt(u, sp):
    t = u.shape[0]
    n_seg = S5_SEGMENTS
    seg_len = t // n_seg
    jc = min(64, seg_len)
    u_rows = u.reshape(n_seg, seg_len, S5_WIDTH).transpose(1, 0, 2).reshape(t, S5_WIDTH)
    zero = jnp.zeros((n_seg, S5_LANES), F32)
    y_loc, end_re, end_im = _s5_scan(u_rows, zero, zero, sp, n_seg, jc)
    y_rows, f_re, f_im = _s5_fix(y_loc, end_re, end_im, sp, n_seg, jc)
    y = y_rows.reshape(seg_len, n_seg, S5_WIDTH).transpose(1, 0, 2).reshape(t, S5_WIDTH)
    return y, f_re, f_im


def _s5_sample(u3, h0_re, h0_im, sp):
    b, t, _ = u3.shape
    u_rows = u3.transpose(1, 0, 2).reshape(t * b, S5_WIDTH)
    y_rows, h_re, h_im = _s5_scan(u_rows, h0_re, h0_im, sp, b, t)
    return y_rows.reshape(t, b, S5_WIDTH).transpose(1, 0, 2), h_re, h_im


def _merge_kernel(x_ref, oc_ref, os_ref, ow_ref, om_ref, y5_ref, ga_ref, gm_ref, wglu_ref, bglu_ref,
                  wn_ref, wm_ref, w5_ref, wo_ref, g_ref, b_ref, o_ref):
    qw = NSA_Q_W
    ga = ga_ref[...]
    o_nsa = (_sigmoid(ga[:, :qw]) * oc_ref[...] + _sigmoid(ga[:, qw:2 * qw]) * os_ref[...]
             + _sigmoid(ga[:, 2 * qw:]) * ow_ref[...])
    z = _gelu(y5_ref[...])
    o_s5 = z * _sigmoid(jnp.dot(_bf(z), wglu_ref[...], preferred_element_type=F32) + bglu_ref[...])
    gm = gm_ref[...]
    d = D_MODEL
    merged = (_sigmoid(gm[:, :d]) * jnp.dot(_bf(o_nsa), wn_ref[...], preferred_element_type=F32)
              + _sigmoid(gm[:, d:2 * d]) * jnp.dot(_bf(om_ref[...]), wm_ref[...], preferred_element_type=F32)
              + _sigmoid(gm[:, 2 * d:]) * jnp.dot(_bf(o_s5), w5_ref[...], preferred_element_type=F32))
    mix = jnp.dot(_bf(merged), wo_ref[...], preferred_element_type=F32)
    o_ref[...] = _layer_norm(DN_ALPHA * x_ref[...] + mix, g_ref[...], b_ref[...])


def _merge(x2, o_c, o_s, o_w, o_m, y5, proj2, lp, tm):
    m = x2.shape[0]
    row = lambda n: pl.BlockSpec((tm, n), lambda i: (i, 0))
    full = lambda a: pl.BlockSpec(a.shape, lambda i: (0,) * a.ndim)
    ws = [lp["s5_w_glu"].astype(BF16), lp["s5_b_glu"].reshape(1, S5_WIDTH),
          lp["w_br_nsa"].astype(BF16), lp["w_br_moba"].astype(BF16), lp["w_br_s5"].astype(BF16),
          lp["w_out"].astype(BF16), lp["ln1_g"].reshape(1, D_MODEL), lp["ln1_b"].reshape(1, D_MODEL)]
    gaw, gmw = 3 * NSA_Q_W, N_BRANCH * D_MODEL
    return pl.pallas_call(
        _merge_kernel,
        grid=(m // tm,),
        in_specs=[row(D_MODEL), row(NSA_Q_W), row(NSA_Q_W), row(NSA_Q_W), row(MOBA_Q_W), row(S5_WIDTH),
                  pl.BlockSpec((tm, gaw), lambda i: (i, PC_GA // gaw)),
                  pl.BlockSpec((tm, gmw), lambda i: (i, PC_GM // gmw))] + [full(a) for a in ws],
        out_specs=row(D_MODEL),
        out_shape=jax.ShapeDtypeStruct((m, D_MODEL), F32),
        compiler_params=_cparams("parallel"),
        name="merge",
    )(x2, o_c, o_s, o_w, o_m, y5, proj2, proj2, *ws)


def _router_kernel(x_ref, w_ref, b_ref, o_ref):
    logits = _dot3(x_ref[...], w_ref[...]) + b_ref[...]
    lane = lax.broadcasted_iota(jnp.int32, logits.shape, 1)
    lanef = lane.astype(F32)
    logits = jnp.where(lane < N_EXPERTS, logits, -jnp.inf)
    sel = _top_select(logits, lanef, MOE_TOP_K)
    m = jnp.max(logits, axis=-1, keepdims=True)
    e = jnp.where(sel > 0.5, jnp.exp(logits - m), 0.0)
    o_ref[...] = e / jnp.sum(e, axis=-1, keepdims=True)


def _router(x2, w_router, b_router, tm):
    m = x2.shape[0]
    w = jnp.pad(w_router, ((0, 0), (0, LANES - N_EXPERTS)))
    bias = jnp.pad(b_router, (0, LANES - N_EXPERTS)).reshape(1, LANES)
    return pl.pallas_call(
        _router_kernel,
        grid=(m // tm,),
        in_specs=[pl.BlockSpec((tm, D_MODEL), lambda i: (i, 0)),
                  pl.BlockSpec(w.shape, lambda i: (0, 0)), pl.BlockSpec(bias.shape, lambda i: (0, 0))],
        out_specs=pl.BlockSpec((tm, LANES), lambda i: (i, 0)),
        out_shape=jax.ShapeDtypeStruct((m, LANES), F32),
        compiler_params=_cparams("parallel"),
        name="router",
    )(x2, w, bias)


def _ffn_kernel(*refs, use_comb):
    if use_comb:
        x_ref, comb_ref, wg_ref, wu_ref, wd_ref, g_ref, b_ref, o_ref, acc_sc = refs
    else:
        x_ref, wg_ref, wu_ref, wd_ref, g_ref, b_ref, o_ref, acc_sc = refs
    f = pl.program_id(1)

    @pl.when(f == 0)
    def _():
        acc_sc[...] = jnp.zeros(acc_sc.shape, F32)

    xb = _bf(x_ref[...])
    wg = wg_ref[0] if use_comb else wg_ref[...]
    wu = wu_ref[0] if use_comb else wu_ref[...]
    wd = wd_ref[0] if use_comb else wd_ref[...]
    gate = jnp.dot(xb, _bf(wg), preferred_element_type=F32)
    up = jnp.dot(xb, _bf(wu), preferred_element_type=F32)
    h = gate * _sigmoid(gate) * up
    if use_comb:
        comb = comb_ref[...]
        lane = lax.broadcasted_iota(jnp.int32, comb.shape, 1)
        h = h * jnp.sum(jnp.where(lane == f, comb, 0.0), axis=-1, keepdims=True)
    acc_sc[...] += jnp.dot(_bf(h), _bf(wd), preferred_element_type=F32)

    @pl.when(f == pl.num_programs(1) - 1)
    def _():
        o_ref[...] = _layer_norm(DN_ALPHA * x_ref[...] + acc_sc[...], g_ref[...], b_ref[...])


def _ffn(x2, comb, wg, wu, wd, ln_g, ln_b, tm, tf):
    m = x2.shape[0]
    use_comb = comb is not None
    if use_comb:
        n_f = wg.shape[0]
        w_specs = [pl.BlockSpec((1,) + wg.shape[1:], lambda i, f: (f, 0, 0)),
                   pl.BlockSpec((1,) + wu.shape[1:], lambda i, f: (f, 0, 0)),
                   pl.BlockSpec((1,) + wd.shape[1:], lambda i, f: (f, 0, 0))]
    else:
        n_f = wg.shape[1] // tf
        w_specs = [pl.BlockSpec((D_MODEL, tf), lambda i, f: (0, f)),
                   pl.BlockSpec((D_MODEL, tf), lambda i, f: (0, f)),
                   pl.BlockSpec((tf, D_MODEL), lambda i, f: (f, 0))]
    vec = pl.BlockSpec((1, D_MODEL), lambda i, f: (0, 0))
    in_specs = [pl.BlockSpec((tm, D_MODEL), lambda i, f: (i, 0))]
    args = [x2]
    if use_comb:
        in_specs.append(pl.BlockSpec((tm, LANES), lambda i, f: (i, 0)))
        args.append(comb)
    return pl.pallas_call(
        functools.partial(_ffn_kernel, use_comb=use_comb),
        grid=(m // tm, n_f),
        in_specs=in_specs + w_specs + [vec, vec],
        out_specs=pl.BlockSpec((tm, D_MODEL), lambda i, f: (i, 0)),
        out_shape=jax.ShapeDtypeStruct((m, D_MODEL), F32),
        scratch_shapes=[pltpu.VMEM((tm, D_MODEL), F32)],
        compiler_params=_cparams("parallel", "arbitrary"),
        name="ffn",
    )(*args, wg.astype(BF16), wu.astype(BF16), wd.astype(BF16), ln_g.reshape(1, D_MODEL), ln_b.reshape(1, D_MODEL))


def _token_mixer(x3, t_real, qpos0, past, lp, sp, h0_re, h0_im):
    b, tp, _ = x3.shape
    rows = b * tp
    x2 = x3.reshape(rows, D_MODEL)
    proj2 = _matmul(x2, lp["w_proj"], min(256, rows), PC_END // 2)
    proj3 = proj2.reshape(b, tp, PC_END)
    kva = proj3[:, :t_real, PC_KVA:PC_END]
    new_cmp, new_sel, new_win = (kva[:, :, i * 2 * NSA_KV_W:(i + 1) * 2 * NSA_KV_W] for i in range(3))
    new_moba = proj3[:, :t_real, PC_KVB:PC_KVA]
    cw = _cmp_weights(lp["phi1_k"], lp["phi1_v"])
    cmp_args = (lp["pe_k"], lp["pe_v"], lp["phi1_k"], lp["phi1_v"], lp["phi2_k"], lp["phi2_v"])
    rw_a, rw_b = 2 * NSA_KV_W, 2 * MOBA_KV_W

    qa_col = PC_QA // (NSA_HEADS * NSA_KV_W)
    qb_col = PC_QB // (MOBA_HEADS * MOBA_KV_W)
    if past is None:
        assert qpos0 == 0 and b == 1 and t_real == tp and t_real % MOBA_BLOCK == 0
        tq_sel = 256 if tp % 256 == 0 else 128
        tq_moba = 512 if tp % 512 == 0 else tq_sel
        tq = 256 if tp % 256 == 0 else 128
        tk = 512 if tp % 512 == 0 else tp
        ab = _cmp_ab_linear(new_cmp.reshape(t_real // NSA_CMP_STRIDE, NSA_CMP_STRIDE * rw_a), cw)
        kc = _cmp_finish(ab, 1, *cmp_args)
        n_kt = tp // tk
        assert n_kt <= LANES
        assert tq_sel == tq
        o_c, sel_bias, act = _cmp_attention(proj3, kc, 0, t_real, tq, act_blocks=tk // NSA_SEL_BLOCK, bias_t=True)
        act = act[0, :, 0, :n_kt] > 0.5
        sel_v_t = proj3[:, :, PC_KVA + rw_a + NSA_KV_W:PC_KVA + 2 * rw_a].transpose(0, 2, 1)
        o_s = _flash_worklist(proj3, qa_col, proj3, (PC_KVA + rw_a) // NSA_KV_W, sel_v_t, sel_bias, act,
                              _attn_cfg("sel"), tq=tq, tk=tk)
        tk_win = 256 if tp % 256 == 0 else tk
        n_win = min(tp // tk_win, (NSA_WINDOW + tq - 2) // tk_win + 2)
        o_w = _flash_linear(proj3, qa_col, proj3, (PC_KVA + 2 * rw_a) // rw_a, None,
                            _attn_cfg("win"), tq=tq, tk=tk_win, qpos0=0, kpos0=0, n_steps=n_win)
        kmean = _moba_kmean(proj3, None)
        mb_bias = _moba_gate(proj3, kmean, 0, t_real, tq_moba, True)
        v_t = proj3[:, :, PC_KVB + MOBA_KV_W:PC_KVA].transpose(0, 2, 1)
        o_m = _flash_transposed(proj3, qb_col, proj3, PC_KVB // MOBA_KV_W, v_t, mb_bias, _attn_cfg("moba"),
                                tq=tq_moba, tk=MOBA_BLOCK, n_steps=tp // MOBA_BLOCK)
        y5, h_re, h_im = _s5_prompt(proj2[:, PC_U:PC_U + S5_WIDTH], sp)
        keep = min(NSA_WINDOW, t_real)
        win_state = new_win[:, t_real - keep:]
    else:
        assert t_real < NSA_CMP_STRIDE and qpos0 % MOBA_BLOCK == 0 and tp % SUBLANES == 0
        pt, base = past["page_table"], past["page_base"]
        ab = _cmp_ab_paged(past["cmp_t"], base, pt, cw)
        kc = _cmp_finish(ab, b, *cmp_args)
        o_c, sel_mask, _ = _cmp_attention(proj3, kc, qpos0, t_real, tp)
        o_s = _decode_attn(proj3, qa_col, (PC_KVA + rw_a) // rw_a, past["sel_t"], pt, base, sel_mask,
                           _attn_cfg("sel"), qpos0=qpos0, kpos0=0)
        wb = past["win_t"].shape[2]
        o_w = _decode_attn(proj3, qa_col, (PC_KVA + 2 * rw_a) // rw_a, past["win_t"],
                           jnp.arange(b, dtype=jnp.int32).reshape(b, 1), past["win_base"], None,
                           _attn_cfg("win"), qpos0=qpos0, kpos0=qpos0 - wb)
        kmean = _moba_kmean(past["moba_t"], pt, base)
        mb_mask = _moba_gate(proj3, kmean, qpos0, t_real, tp, False)
        o_m = _decode_attn(proj3, qb_col, PC_KVB // rw_b, past["moba_t"], pt, base, mb_mask,
                           _attn_cfg("moba"), qpos0=qpos0, kpos0=0)
        y5r, h_re, h_im = _s5_sample(proj3[:, :t_real, PC_U:PC_U + S5_WIDTH], h0_re, h0_im, sp)
        y5 = jnp.pad(y5r, ((0, 0), (0, tp - t_real), (0, 0))).reshape(rows, S5_WIDTH)
        win_state = jnp.concatenate([past["win_rows"], new_win], axis=1)[:, -wb:]

    flat = lambda a: a.reshape(rows, a.shape[-1])
    x1 = _merge(x2, flat(o_c), flat(o_s), flat(o_w), flat(o_m), y5, proj2, lp, min(256, rows))
    return x1, (new_cmp, new_sel, win_state, new_moba, h_re, h_im)


def _layer(l, x3, t_real, qpos0, past, h0_re, h0_im, lp, sp, ffn_p):
    b, tp, _ = x3.shape
    x1, st = _token_mixer(x3, t_real, qpos0, past, lp, sp, h0_re, h0_im)
    tm = min(512, x1.shape[0])
    if l % 2 == 0:
        x2 = _ffn(x1, None, ffn_p["wg"], ffn_p["wu"], ffn_p["wd"], lp["ln2_g"], lp["ln2_b"], tm, ffn_p["tf"])
    else:
        comb = _router(x1, ffn_p["w_router"], ffn_p["b_router"], tm)
        x2 = _ffn(x1, comb, ffn_p["wg"], ffn_p["wu"], ffn_p["wd"], lp["ln2_g"], lp["ln2_b"], tm, None)
    return x2.reshape(b, tp, D_MODEL), st


def kernel(x_prompt, x_sample, cache_nsa_cmp, cache_nsa_sel, cache_nsa_win, cache_moba, state_s5_re, state_s5_im,
           page_table, w_in, pe_k, phi1_k, phi2_k, pe_v, phi1_v, phi2_v, s5_lambda_re, s5_lambda_im, s5_log_dt,
           s5_b_re, s5_b_im, s5_c_re, s5_c_im, s5_d, s5_w_glu, s5_b_glu, w_br_nsa, w_br_moba, w_br_s5, w_out,
           ln1_g, ln1_b, ln2_g, ln2_b, ffn_w_gate, ffn_w_up, ffn_w_down, moe_w_router, moe_b_router,
           moe_w_gate, moe_w_up, moe_w_down):
    depth = w_in.shape[0]
    bp, seq, _ = x_prompt.shape
    bs, dec_seq, _ = x_sample.shape
    n_pool = cache_moba.shape[1]
    past_len = page_table.shape[1] * PAGE_SIZE
    tps = _round_up(dec_seq, SUBLANES)
    y_p = x_prompt
    y_s = jnp.pad(x_sample, ((0, 0), (0, tps - dec_seq), (0, 0)))
    rw_a, rw_b = 2 * NSA_KV_W, 2 * MOBA_KV_W

    def tiles_t(cache):
        d, n, r = cache.shape[:3]
        return cache.transpose(0, 1, 3, 4, 5, 2).reshape(d * n, -1, r)

    cmp_t, sel_t, moba_t, win_t = (tiles_t(c) for c in (cache_nsa_cmp, cache_nsa_sel, cache_moba, cache_nsa_win))
    p_st, s_st = [], []
    for l in range(depth):
        lp = dict(w_proj=_proj_weight(w_in[l]), pe_k=pe_k[l], phi1_k=phi1_k[l], phi2_k=phi2_k[l], pe_v=pe_v[l],
                  phi1_v=phi1_v[l], phi2_v=phi2_v[l], s5_lambda_re=s5_lambda_re[l], s5_lambda_im=s5_lambda_im[l],
                  s5_log_dt=s5_log_dt[l], s5_b_re=s5_b_re[l], s5_b_im=s5_b_im[l], s5_c_re=s5_c_re[l],
                  s5_c_im=s5_c_im[l], s5_d=s5_d[l], s5_w_glu=s5_w_glu[l], s5_b_glu=s5_b_glu[l],
                  w_br_nsa=w_br_nsa[l], w_br_moba=w_br_moba[l], w_br_s5=w_br_s5[l], w_out=w_out[l],
                  ln1_g=ln1_g[l], ln1_b=ln1_b[l], ln2_g=ln2_g[l], ln2_b=ln2_b[l])
        sp = _s5_params(lp)
        if l % 2 == 0:
            d_ff = ffn_w_gate.shape[2]
            tf = d_ff // 2 if (d_ff // 2) % LANES == 0 else d_ff
            ffn_p = dict(wg=ffn_w_gate[l // 2], wu=ffn_w_up[l // 2], wd=ffn_w_down[l // 2], tf=tf)
        else:
            ffn_p = dict(wg=moe_w_gate[l // 2], wu=moe_w_up[l // 2], wd=moe_w_down[l // 2],
                         w_router=moe_w_router[l // 2], b_router=moe_b_router[l // 2])
        y_p, st = _layer(l, y_p, seq, 0, None, None, None, lp, sp, ffn_p)
        p_st.append(st)
        past = dict(page_table=page_table, page_base=l * n_pool, cmp_t=cmp_t, sel_t=sel_t, moba_t=moba_t,
                    win_t=win_t, win_base=l * bs, win_rows=cache_nsa_win[l].reshape(bs, -1, rw_a))
        y_s, st = _layer(l, y_s, dec_seq, past_len, past, state_s5_re[l].reshape(bs, S5_LANES),
                         state_s5_im[l].reshape(bs, S5_LANES), lp, sp, ffn_p)
        s_st.append(st)

    def stk(states, j, tail):
        return jnp.stack([st[j].reshape(st[j].shape[:2] + tail) if tail else st[j] for st in states], axis=0)

    row_a = (2, NSA_KV_HEADS, HEAD_DIM)
    row_b = (2, MOBA_KV_HEADS, HEAD_DIM)

    def s5_state(states, j, b):
        return jnp.stack([st[j].reshape(b, S5_GROUPS, S5_STATE) for st in states], axis=0)

    return (y_p, y_s[:, :dec_seq],
            stk(p_st, 0, row_a), stk(p_st, 1, row_a), stk(p_st, 2, row_a), stk(p_st, 3, row_b),
            s5_state(p_st, 4, bp), s5_state(p_st, 5, bp),
            stk(s_st, 0, row_a), stk(s_st, 1, row_a), stk(s_st, 2, row_a), stk(s_st, 3, row_b),
            s5_state(s_st, 4, bs), s5_state(s_st, 5, bs))
```

```python
import functools
import math

import numpy as np
import jax
import jax.numpy as jnp
from jax import lax
from jax.experimental import pallas as pl
from jax.experimental.pallas import tpu as pltpu

F32 = jnp.float32
BF16 = jnp.bfloat16

D_MODEL = 1024
HEAD_DIM = 64
PAGE_SIZE = 128
NSA_HEADS = 8
NSA_KV_HEADS = 2
NSA_CMP_LEN = 32
NSA_CMP_STRIDE = 16
NSA_PHI_HIDDEN = 128
NSA_SEL_BLOCK = 64
NSA_SEL_TOPK = 16
NSA_N_LOCAL = 2
NSA_WINDOW = 512
MOBA_HEADS = 8
MOBA_KV_HEADS = 4
MOBA_BLOCK = 256
MOBA_TOPK = 3
S5_GROUPS = 32
S5_GROUP_CH = 16
S5_WIDTH = S5_GROUPS * S5_GROUP_CH
S5_STATE = 64
S5_LANES = S5_GROUPS * S5_STATE
N_BRANCH = 3
N_EXPERTS = 8
MOE_TOP_K = 2
LN_EPS = 1e-5
DEPTH = 2
DN_ALPHA = (2.0 * DEPTH) ** 0.25

NSA_Q_W = NSA_HEADS * HEAD_DIM
NSA_KV_W = NSA_KV_HEADS * HEAD_DIM
MOBA_Q_W = MOBA_HEADS * HEAD_DIM
MOBA_KV_W = MOBA_KV_HEADS * HEAD_DIM
IN_SPLITS = (NSA_Q_W, 6 * NSA_KV_W, 3 * NSA_HEADS, MOBA_Q_W, 2 * MOBA_KV_W, S5_WIDTH, N_BRANCH * D_MODEL)
IN_OFFS = tuple(int(v) for v in np.cumsum((0,) + IN_SPLITS))

LANES = 128
SUBLANES = 8
VMEM_LIMIT = 56 * 1024 * 1024

PC_QB = 0
PC_QA = PC_QB + MOBA_HEADS * MOBA_KV_W
PC_GM = PC_QA + NSA_HEADS * NSA_KV_W
PC_GA = PC_GM + N_BRANCH * D_MODEL
PC_U = PC_GA + 3 * NSA_Q_W
PC_KVB = PC_U + S5_WIDTH
PC_KVA = PC_KVB + 2 * MOBA_KV_W
PC_END = PC_KVA + 6 * NSA_KV_W

NEG = -1e30
LOG2E = 1.0 / math.log(2.0)
SLOPES_A = tuple(float(2.0 ** (-8.0 * (i + 1) / NSA_HEADS)) * LOG2E for i in range(NSA_HEADS))
SLOPES_B = tuple(float(2.0 ** (-8.0 * (i + 1) / MOBA_HEADS)) * LOG2E for i in range(MOBA_HEADS))


def _cparams(*sem):
    return pltpu.CompilerParams(dimension_semantics=sem, vmem_limit_bytes=VMEM_LIMIT)


def _round_up(n, m):
    return -(-n // m) * m


def _bf(x):
    return x.astype(BF16)


def _dot(a, b):
    return jnp.dot(_bf(a), _bf(b), preferred_element_type=F32)


def _dot_nt(a, b):
    return lax.dot_general(_bf(a), _bf(b), (((1,), (1,)), ((), ())), preferred_element_type=F32)


def _split(a):
    hi = a.astype(BF16)
    lo = (a - hi.astype(F32)).astype(BF16)
    return hi, lo


def _dot3(a, b):
    ah, al = _split(a)
    bh, bl = _split(b)
    d = functools.partial(jnp.dot, preferred_element_type=F32)
    return d(ah, bh) + (d(ah, bl) + d(al, bh))


def _dot3_nt(a, b):
    ah, al = _split(a)
    bh, bl = _split(b)
    d = functools.partial(lax.dot_general, dimension_numbers=(((1,), (1,)), ((), ())),
                          preferred_element_type=F32)
    return d(ah, bh) + (d(ah, bl) + d(al, bh))


def _dot2_exact_rhs(a, b_bf16):
    ah, al = _split(a)
    d = functools.partial(jnp.dot, preferred_element_type=F32)
    return d(ah, b_bf16) + d(al, b_bf16)


def _sigmoid(x):
    return 1.0 / (1.0 + jnp.exp(-x))


def _gelu(x):
    c = math.sqrt(2.0 / math.pi)
    return x * (0.5 * (1.0 + jnp.tanh(c * (x + 0.044715 * (x * x * x)))))


def _layer_norm(v, g, b):
    mu = jnp.mean(v, axis=-1, keepdims=True)
    vc = v - mu
    var = jnp.mean(vc * vc, axis=-1, keepdims=True)
    return vc * lax.rsqrt(var + LN_EPS) * g + b


def _mm_kernel(x_ref, w_ref, o_ref):
    o_ref[...] = jnp.dot(_bf(x_ref[...]), w_ref[...], preferred_element_type=F32)


def _matmul(x, w, tm, tn):
    m, k = x.shape
    n = w.shape[1]
    return pl.pallas_call(
        _mm_kernel,
        grid=(n // tn, m // tm),
        in_specs=[pl.BlockSpec((tm, k), lambda j, i: (i, 0)),
                  pl.BlockSpec((k, tn), lambda j, i: (0, j))],
        out_specs=pl.BlockSpec((tm, tn), lambda j, i: (i, j)),
        out_shape=jax.ShapeDtypeStruct((m, n), F32),
        compiler_params=_cparams("parallel", "arbitrary"),
        name="proj",
    )(x, w)


def _proj_weight(w_in):
    scale = HEAD_DIM ** -0.5 * LOG2E
    qa = w_in[:, IN_OFFS[0]:IN_OFFS[1]].reshape(D_MODEL, NSA_HEADS, 1, HEAD_DIM) * scale
    oh_a = np.zeros((NSA_HEADS, NSA_KV_HEADS, 1), np.float32)
    for h in range(NSA_HEADS):
        oh_a[h, h // (NSA_HEADS // NSA_KV_HEADS)] = 1.0
    qa = (qa * oh_a[None]).reshape(D_MODEL, NSA_HEADS * NSA_KV_W)
    qb = w_in[:, IN_OFFS[3]:IN_OFFS[4]].reshape(D_MODEL, MOBA_HEADS, 1, HEAD_DIM) * scale
    oh_b = np.zeros((MOBA_HEADS, MOBA_KV_HEADS, 1), np.float32)
    for h in range(MOBA_HEADS):
        oh_b[h, h // (MOBA_HEADS // MOBA_KV_HEADS)] = 1.0
    qb = (qb * oh_b[None]).reshape(D_MODEL, MOBA_HEADS * MOBA_KV_W)
    ga = w_in[:, IN_OFFS[2]:IN_OFFS[3]].reshape(D_MODEL, NSA_HEADS, 3).transpose(0, 2, 1)
    ga = jnp.broadcast_to(ga[..., None], (D_MODEL, 3, NSA_HEADS, HEAD_DIM)).reshape(D_MODEL, 3 * NSA_Q_W)
    gm = w_in[:, IN_OFFS[6]:IN_OFFS[7]]
    u = w_in[:, IN_OFFS[5]:IN_OFFS[6]]
    kvb = w_in[:, IN_OFFS[4]:IN_OFFS[5]]
    kva = w_in[:, IN_OFFS[1]:IN_OFFS[2]]
    return jnp.concatenate([qb, qa, gm, ga, u, kvb, kva], axis=1).astype(BF16)


def _cmpab_kernel(x_ref, w_ref, o_ref):
    row_w = 2 * NSA_KV_W
    x = x_ref[...]
    for kv in range(2):
        xk = jnp.concatenate(
            [x[:, r * row_w + kv * NSA_KV_W: r * row_w + (kv + 1) * NSA_KV_W] for r in range(NSA_CMP_STRIDE)],
            axis=1)
        o_ref[:, kv * 4 * NSA_PHI_HIDDEN:(kv + 1) * 4 * NSA_PHI_HIDDEN] = _dot(xk, w_ref[kv])


def _cmp_weights(phi1_k, phi1_v):
    eye = np.eye(NSA_KV_HEADS, dtype=np.float32)
    out = []
    for phi in (phi1_k, phi1_v):
        halves = []
        for half in range(NSA_CMP_LEN // NSA_CMP_STRIDE):
            w = phi[half * NSA_CMP_STRIDE * HEAD_DIM:(half + 1) * NSA_CMP_STRIDE * HEAD_DIM]
            w = w.reshape(NSA_CMP_STRIDE, 1, HEAD_DIM, 1, NSA_PHI_HIDDEN)
            w = w * eye[None, :, None, :, None]
            halves.append(w.reshape(NSA_CMP_STRIDE * NSA_KV_W, NSA_KV_HEADS * NSA_PHI_HIDDEN))
        out.append(jnp.concatenate(halves, axis=1))
    return jnp.stack(out).astype(BF16)


def _cmp_ab_linear(x, w):
    mc = x.shape[0]
    tmc = min(mc, 256)
    n_out = 8 * NSA_PHI_HIDDEN
    return pl.pallas_call(
        _cmpab_kernel,
        grid=(mc // tmc,),
        in_specs=[pl.BlockSpec((tmc, x.shape[1]), lambda i: (i, 0)),
                  pl.BlockSpec(w.shape, lambda i: (0, 0, 0))],
        out_specs=pl.BlockSpec((tmc, n_out), lambda i: (i, 0)),
        out_shape=jax.ShapeDtypeStruct((mc, n_out), F32),
        compiler_params=_cparams("parallel"),
        name="cmp_ab",
    )(x, w)


CMP_PAGES_PER_STEP = 16


def _cmpab_t_kernel(*refs, npp):
    page_refs, w_ref, o_ref, pk_sc, pv_sc = refs[1:1 + npp], refs[1 + npp], refs[2 + npp], refs[3 + npp], refs[4 + npp]
    w = NSA_KV_W
    p = PAGE_SIZE
    eye = jnp.where(lax.broadcasted_iota(jnp.int32, (p, p), 0) == lax.broadcasted_iota(jnp.int32, (p, p), 1),
                    1.0, 0.0).astype(BF16)
    for k in range(npp):
        page_t = page_refs[k][0]
        pk_sc[k * p:(k + 1) * p, :] = _dot_nt(eye, page_t[:w])
        pv_sc[k * p:(k + 1) * p, :] = _dot_nt(eye, page_t[w:])
    m = npp * (p // NSA_CMP_STRIDE)
    n_half = 4 * NSA_PHI_HIDDEN
    for kv, sc in ((0, pk_sc), (1, pv_sc)):
        acc = None
        for r in range(0, NSA_CMP_STRIDE, 2):
            x = jnp.concatenate([sc[pl.ds(r, m, stride=NSA_CMP_STRIDE), :],
                                 sc[pl.ds(r + 1, m, stride=NSA_CMP_STRIDE), :]], axis=1)
            part = _dot(x, w_ref[kv, r // 2])
            acc = part if acc is None else acc + part
        o_ref[:, kv * n_half:(kv + 1) * n_half] = acc


def _cmp_ab_paged(pool_t, base, page_table, w):
    b, n_pages = page_table.shape
    cpp = PAGE_SIZE // NSA_CMP_STRIDE
    npp = math.gcd(CMP_PAGES_PER_STEP, n_pages)
    steps = n_pages // npp
    n_out = 8 * NSA_PHI_HIDDEN
    w4 = w.reshape(2, NSA_CMP_STRIDE // 2, 2 * NSA_KV_W, w.shape[-1])

    def x_map(k):
        return lambda bi, s, pt: (base + pt[bi * n_pages + s * npp + k], 0, 0)

    grid_spec = pltpu.PrefetchScalarGridSpec(
        num_scalar_prefetch=1,
        grid=(b, steps),
        in_specs=[pl.BlockSpec((1,) + pool_t.shape[1:], x_map(k)) for k in range(npp)]
        + [pl.BlockSpec(w4.shape, lambda bi, s, pt: (0, 0, 0, 0))],
        out_specs=pl.BlockSpec((npp * cpp, n_out), lambda bi, s, pt: (bi * steps + s, 0)),
        scratch_shapes=[pltpu.VMEM((npp * PAGE_SIZE, NSA_KV_W), F32), pltpu.VMEM((npp * PAGE_SIZE, NSA_KV_W), F32)],
    )
    return pl.pallas_call(
        functools.partial(_cmpab_t_kernel, npp=npp),
        grid_spec=grid_spec,
        out_shape=jax.ShapeDtypeStruct((b * n_pages * cpp, n_out), F32),
        compiler_params=_cparams("parallel", "arbitrary"),
        name="cmp_ab_paged",
    )(page_table.reshape(-1), *([pool_t] * npp), w4)


def _cmpfin_kernel(ab_ref, pe_ref, w1_ref, w2_ref, o_ref):
    mc = ab_ref.shape[0]
    hid2 = NSA_KV_HEADS * NSA_PHI_HIDDEN
    for kv in range(2):
        c = _dot(pe_ref[kv], w1_ref[kv])[0:1]
        c = jnp.concatenate([c] * NSA_KV_HEADS, axis=1)
        a = ab_ref[:, kv * 2 * hid2: kv * 2 * hid2 + hid2]
        bnext = pltpu.roll(ab_ref[:, kv * 2 * hid2 + hid2:(kv + 1) * 2 * hid2], mc - 1, 0)
        hpre = a + bnext + c
        o_ref[:, kv * NSA_KV_W:(kv + 1) * NSA_KV_W] = _dot(_gelu(hpre), w2_ref[kv])


def _cmp_finish(ab, nb, pe_k, pe_v, phi1_k, phi1_v, phi2_k, phi2_v):
    mc = ab.shape[0] // nb
    flat = NSA_CMP_LEN * HEAD_DIM
    pe = jnp.stack([jnp.broadcast_to(p.reshape(1, flat), (SUBLANES, flat)) for p in (pe_k, pe_v)])
    w1 = jnp.stack([phi1_k, phi1_v]).astype(BF16)
    eye = np.eye(NSA_KV_HEADS, dtype=np.float32)
    w2 = jnp.stack([(p[None, :, None, :] * eye[:, None, :, None]).reshape(NSA_KV_HEADS * NSA_PHI_HIDDEN, NSA_KV_W)
                    for p in (phi2_k, phi2_v)]).astype(BF16)
    out = pl.pallas_call(
        _cmpfin_kernel,
        grid=(nb,),
        in_specs=[pl.BlockSpec((mc, ab.shape[1]), lambda i: (i, 0)),
                  pl.BlockSpec(pe.shape, lambda i: (0, 0, 0)),
                  pl.BlockSpec(w1.shape, lambda i: (0, 0, 0)),
                  pl.BlockSpec(w2.shape, lambda i: (0, 0, 0))],
        out_specs=pl.BlockSpec((mc, 2 * NSA_KV_W), lambda i: (i, 0)),
        out_shape=jax.ShapeDtypeStruct((nb * mc, 2 * NSA_KV_W), F32),
        compiler_params=_cparams("parallel"),
        name="cmp_finish",
    )(ab, pe, w1, w2)
    return out.reshape(nb, mc, 2 * NSA_KV_W)


def _stack_heads(q, n_heads, w):
    return jnp.concatenate([_bf(q[:, h * w:(h + 1) * w]) for h in range(n_heads)], axis=0)


def _place_heads(o_heads, n_heads, n_groups, o_ref):
    hpg = n_heads // n_groups
    tq = o_heads[0].shape[0]
    lane = lax.broadcasted_iota(jnp.int32, (tq, LANES), 1)
    for pair in range(n_heads // 2):
        pieces = []
        for h in (2 * pair, 2 * pair + 1):
            src = (h // hpg) * HEAD_DIM
            piece = o_heads[h][:, (src // LANES) * LANES:(src // LANES + 1) * LANES]
            if src % LANES != (h % 2) * HEAD_DIM:
                piece = pltpu.roll(piece, HEAD_DIM, 1)
            pieces.append(piece)
        o_ref[0, :, pair * LANES:(pair + 1) * LANES] = jnp.where(lane < HEAD_DIM, pieces[0], pieces[1])


def _top_select(score, lanef, n_pick):
    sel = jnp.zeros(score.shape, F32)
    for _ in range(n_pick):
        m = jnp.max(score, axis=-1, keepdims=True)
        idx = jnp.min(jnp.where(score == m, lanef, 1e9), axis=-1, keepdims=True)
        pick = lanef == idx
        sel = jnp.where(pick & (m > -jnp.inf), 1.0, sel)
        score = jnp.where(pick, -jnp.inf, score)
    return sel


def _cmpattn_kernel(q_ref, kc_ref, ov_ref, o_ref, sel_ref, act_ref, *, tq, qpos0, n_cmp, n_sel_blk, act_blocks,
                    bias_t):
    w = NSA_KV_W
    hpg = NSA_HEADS // NSA_KV_HEADS
    ncp = kc_ref.shape[1]
    nbp = ov_ref.shape[1]
    qbase = qpos0 + pl.program_id(1) * tq
    qs = _stack_heads(q_ref[0], NSA_HEADS, w)
    kc = kc_ref[0]
    s_all = _dot_nt(qs, kc[:, :w])
    c_idx = lax.broadcasted_iota(jnp.int32, (1, ncp), 1)
    row = lax.broadcasted_iota(jnp.int32, (tq, 1), 0)
    cend_rel = c_idx * NSA_CMP_STRIDE + (NSA_CMP_LEN - 1) - qbase
    valid = (row - cend_rel >= 0) & (c_idx < n_cmp)
    colf = cend_rel.astype(F32)
    ps, psums = [], []
    for h in range(NSA_HEADS):
        s = jnp.where(valid, s_all[h * tq:(h + 1) * tq] + SLOPES_A[h] * colf, NEG)
        m = jnp.max(s, axis=-1, keepdims=True)
        e = jnp.where(valid, jnp.exp2(s - m), 0.0)
        p = e / jnp.maximum(jnp.sum(e, axis=-1, keepdims=True), 1e-30)
        ps.append(_bf(p))
        if h % hpg == 0:
            psums.append(p)
        else:
            psums[-1] = psums[-1] + p
    o_all = jnp.dot(jnp.concatenate(ps, axis=0), _bf(kc[:, w:]), preferred_element_type=F32)
    _place_heads([o_all[h * tq:(h + 1) * tq] for h in range(NSA_HEADS)], NSA_HEADS, NSA_KV_HEADS, o_ref)

    lane = lax.broadcasted_iota(jnp.int32, (tq, nbp), 1)
    lanef = lane.astype(F32)
    cur = (qbase + row) // NSA_SEL_BLOCK
    forced = (lane == 0) | ((lane <= cur) & (lane > cur - NSA_N_LOCAL))
    validb = (lane <= cur) & (lane < n_sel_blk)
    n_top = min(NSA_SEL_TOPK, n_sel_blk)
    scores = [jnp.where(forced, jnp.inf, jnp.where(validb, _dot2_exact_rhs(psums[g], ov_ref[...]), -jnp.inf))
              for g in range(NSA_KV_HEADS)]
    sel_all = _top_select(jnp.concatenate(scores, axis=0), jnp.concatenate([lanef] * NSA_KV_HEADS, axis=0), n_top)
    if bias_t:
        sel_ref[0] = jnp.where(sel_all > 0.5, 0.0, NEG).T
    else:
        for g in range(NSA_KV_HEADS):
            sel_ref[0, g] = sel_all[g * tq:(g + 1) * tq]
    any_blk = jnp.max(sel_all, axis=0, keepdims=True)
    pool = jnp.where((lax.broadcasted_iota(jnp.int32, (nbp, LANES), 0) // act_blocks)
                     == lax.broadcasted_iota(jnp.int32, (nbp, LANES), 1), 1.0, 0.0).astype(BF16)
    act_ref[0, 0] = jnp.dot(_bf(jnp.broadcast_to(any_blk, (SUBLANES, nbp))), pool, preferred_element_type=F32)


def _cmp_attention(proj3, kc, qpos0, t_real, tq, act_blocks=SUBLANES, bias_t=False):
    b, tp, _ = proj3.shape
    ncp = kc.shape[1]
    seq = qpos0 + t_real
    n_cmp = seq // NSA_CMP_STRIDE - NSA_CMP_LEN // NSA_CMP_STRIDE + 1
    n_sel_blk = -(-seq // NSA_SEL_BLOCK)
    nbp = _round_up(n_sel_blk, LANES)
    cs = np.arange(ncp)[:, None] * NSA_CMP_STRIDE
    ss = np.arange(nbp)[None, :] * NSA_SEL_BLOCK
    ov = ((cs + NSA_CMP_LEN > ss) & (cs < ss + NSA_SEL_BLOCK)
          & (np.arange(ncp)[:, None] < n_cmp) & (np.arange(nbp)[None, :] < n_sel_blk))
    ov = jnp.asarray(ov.astype(np.float32), BF16)
    qw = NSA_HEADS * NSA_KV_W
    if bias_t:
        assert b == 1
        sel_spec = pl.BlockSpec((1, nbp, NSA_KV_HEADS * tq), lambda bi, i: (i, 0, 0))
        sel_shape = jax.ShapeDtypeStruct((tp // tq, nbp, NSA_KV_HEADS * tq), F32)
    else:
        sel_spec = pl.BlockSpec((1, NSA_KV_HEADS, tq, nbp), lambda bi, i: (bi, 0, i, 0))
        sel_shape = jax.ShapeDtypeStruct((b, NSA_KV_HEADS, tp, nbp), F32)
    return pl.pallas_call(
        functools.partial(_cmpattn_kernel, tq=tq, qpos0=qpos0, n_cmp=n_cmp, n_sel_blk=n_sel_blk,
                          act_blocks=act_blocks, bias_t=bias_t),
        grid=(b, tp // tq),
        in_specs=[pl.BlockSpec((1, tq, qw), lambda bi, i: (bi, i, PC_QA // qw)),
                  pl.BlockSpec((1, ncp, 2 * NSA_KV_W), lambda bi, i: (bi, 0, 0)),
                  pl.BlockSpec(ov.shape, lambda bi, i: (0, 0))],
        out_specs=[pl.BlockSpec((1, tq, NSA_Q_W), lambda bi, i: (bi, i, 0)),
                   sel_spec,
                   pl.BlockSpec((1, 1, SUBLANES, LANES), lambda bi, i: (bi, i, 0, 0))],
        out_shape=[jax.ShapeDtypeStruct((b, tp, NSA_Q_W), F32),
                   sel_shape,
                   jax.ShapeDtypeStruct((b, tp // tq, SUBLANES, LANES), F32)],
        compiler_params=_cparams("parallel", "arbitrary"),
        name="cmp_attn",
    )(proj3, kc, ov)


def _kmean_kernel(*refs, n_x):
    x_refs, o_ref = refs[:n_x], refs[n_x]
    outs = [jnp.sum(r[0], axis=0, keepdims=True) * (1.0 / MOBA_BLOCK) for r in x_refs]
    o_ref[0, 0] = outs[0] if len(outs) == 1 else jnp.concatenate(outs, axis=0)


KMEAN_BLOCKS_PER_STEP = 16


def _moba_kmean(rows3, page_table, base=0):
    w = MOBA_KV_W
    if page_table is None:
        b, t, _ = rows3.shape
        n_blk = t // MOBA_BLOCK
        bps = math.gcd(KMEAN_BLOCKS_PER_STEP, n_blk)
        steps = n_blk // bps
        out = pl.pallas_call(
            functools.partial(_kmean_kernel, n_x=bps),
            grid=(b, steps),
            in_specs=[pl.BlockSpec((1, MOBA_BLOCK, w), (lambda bi, s, k=k: (bi, s * bps + k, PC_KVB // w)))
                      for k in range(bps)],
            out_specs=pl.BlockSpec((1, 1, bps, w), lambda bi, s: (bi, s, 0, 0)),
            out_shape=jax.ShapeDtypeStruct((b, steps, bps, w), F32),
            compiler_params=_cparams("parallel", "arbitrary"),
            name="kmean",
        )(*([rows3] * bps))
        return out.reshape(b, n_blk, w)
    b, n_pages = page_table.shape
    ppb = MOBA_BLOCK // PAGE_SIZE
    n_blk = n_pages // ppb
    bps = math.gcd(KMEAN_BLOCKS_PER_STEP, n_blk)
    steps = n_blk // bps
    npp = bps * ppb

    def x_map(k):
        return lambda bi, s, pt: (base + pt[bi * n_pages + s * npp + k], 0, 0)

    grid_spec = pltpu.PrefetchScalarGridSpec(
        num_scalar_prefetch=1,
        grid=(b, steps),
        in_specs=[pl.BlockSpec((1, w, PAGE_SIZE), x_map(k)) for k in range(npp)],
        out_specs=pl.BlockSpec((1, 1, bps, w), lambda bi, s, pt: (bi, s, 0, 0)),
    )
    out = pl.pallas_call(
        functools.partial(_kmean_t_kernel, npp=npp),
        grid_spec=grid_spec,
        out_shape=jax.ShapeDtypeStruct((b, steps, bps, w), F32),
        compiler_params=_cparams("parallel", "arbitrary"),
        name="kmean_paged",
    )(page_table.reshape(-1), *([rows3] * npp))
    return out.reshape(b, n_blk, w)


def _kmean_t_kernel(*refs, npp):
    page_refs, o_ref = refs[1:1 + npp], refs[1 + npp]
    ppb = MOBA_BLOCK // PAGE_SIZE
    ones = jnp.ones((SUBLANES, PAGE_SIZE), BF16)
    d = functools.partial(lax.dot_general, dimension_numbers=(((1,), (1,)), ((), ())), preferred_element_type=F32)
    outs = []
    for i in range(npp // ppb):
        acc = None
        for k in range(ppb):
            hi, lo = _split(page_refs[i * ppb + k][0])
            part = d(ones, hi) + d(ones, lo)
            acc = part if acc is None else acc + part
        outs.append(acc[0:1] * (1.0 / MOBA_BLOCK))
    o_ref[0, 0] = outs[0] if len(outs) == 1 else jnp.concatenate(outs, axis=0)


def _mobagate_kernel(q_ref, km_ref, m_ref, *, tq, qpos0, k_top, as_bias):
    w = MOBA_KV_W
    nbp = km_ref.shape[1]
    qbase = qpos0 + pl.program_id(1) * tq
    q = q_ref[0]
    qs = jnp.concatenate([q[:, h * w:(h + 1) * w] for h in range(MOBA_HEADS)], axis=0)
    gs = _dot3_nt(qs, km_ref[0])
    rows = MOBA_HEADS * tq
    lane = lax.broadcasted_iota(jnp.int32, (rows, nbp), 1)
    own = (qbase + lax.broadcasted_iota(jnp.int32, (rows, 1), 0) % tq) // MOBA_BLOCK
    score = jnp.where(lane < own, gs, -jnp.inf)
    sel = jnp.where(lane == own, 1.0, _top_select(score, lane.astype(F32), k_top))
    if as_bias:
        m_ref[0] = jnp.where(sel > 0.5, 0.0, NEG).T
    else:
        for h in range(MOBA_HEADS):
            m_ref[0, h] = sel[h * tq:(h + 1) * tq]


def _moba_gate(proj3, kmean, qpos0, t_real, tq, as_bias):
    b, tp, _ = proj3.shape
    n_mb = -(-(qpos0 + t_real) // MOBA_BLOCK)
    nbp = _round_up(n_mb, LANES)
    km = jnp.pad(kmean, ((0, 0), (0, nbp - kmean.shape[1]), (0, 0)))
    qw = MOBA_HEADS * MOBA_KV_W
    if as_bias:
        assert b == 1
        out_spec = pl.BlockSpec((1, nbp, MOBA_HEADS * tq), lambda bi, i: (i, 0, 0))
        out_shape = jax.ShapeDtypeStruct((tp // tq, nbp, MOBA_HEADS * tq), F32)
    else:
        out_spec = pl.BlockSpec((1, MOBA_HEADS, tq, nbp), lambda bi, i: (bi, 0, i, 0))
        out_shape = jax.ShapeDtypeStruct((b, MOBA_HEADS, tp, nbp), F32)
    return pl.pallas_call(
        functools.partial(_mobagate_kernel, tq=tq, qpos0=qpos0, k_top=min(MOBA_TOPK, n_mb), as_bias=as_bias),
        grid=(b, tp // tq),
        in_specs=[pl.BlockSpec((1, tq, qw), lambda bi, i: (bi, i, PC_QB // qw)),
                  pl.BlockSpec((1, nbp, MOBA_KV_W), lambda bi, i: (bi, 0, 0))],
        out_specs=out_spec,
        out_shape=out_shape,
        compiler_params=_cparams("parallel", "arbitrary"),
        name="moba_gate",
    )(proj3, km)


def _attend_tile(qs, k, v, mask, kbase, qbase, m_sc, l_sc, acc_sc, alpha_sc, *, n_heads, n_groups, n_mask, tq,
                 blk_shift, window, slopes, causal=True):
    tk = k.shape[0]
    s_all = _dot_nt(qs, k)
    delta = qbase - kbase
    col1 = lax.broadcasted_iota(jnp.int32, (1, tk), 1)
    colf = (col1 - delta).astype(F32)
    ok = None
    if causal or window is not None:
        dd = lax.broadcasted_iota(jnp.int32, (tq, tk), 0) - lax.broadcasted_iota(jnp.int32, (tq, tk), 1)
        ok = dd >= -delta
        if window is not None:
            ok = ok & (dd < window - delta)
    if n_mask:
        nbp = mask.shape[-1]
        blk_col = (kbase + col1) >> blk_shift
        expand = jnp.where(lax.broadcasted_iota(jnp.int32, (nbp, tk), 0) == blk_col, 1.0, 0.0).astype(BF16)
        mexp = jnp.dot(_bf(mask.reshape(n_mask * tq, nbp)), expand, preferred_element_type=F32)
    hpm = n_heads // n_mask if n_mask else n_heads
    ps = []
    valid = ok
    for h in range(n_heads):
        rows = slice(h * tq, (h + 1) * tq)
        s = s_all[rows] + slopes[h] * colf
        if n_mask and h % hpm == 0:
            g = h // hpm
            valid = mexp[g * tq:(g + 1) * tq] > 0.5
            if ok is not None:
                valid = ok & valid
        if valid is not None:
            s = jnp.where(valid, s, NEG)
        m_old = m_sc[rows]
        m_new = jnp.maximum(m_old, jnp.max(s, axis=-1, keepdims=True))
        alpha_sc[rows] = jnp.exp2(m_old - m_new)
        m_sc[rows] = m_new
        ps.append(_bf(jnp.exp2(s - _lane_rep(m_new, tk))))
    p_all = jnp.concatenate(ps, axis=0)
    alpha = alpha_sc[...]
    l_sc[...] = alpha * l_sc[...] + jnp.dot(p_all, jnp.ones((tk, LANES), BF16), preferred_element_type=F32)
    acc_sc[...] = (_lane_rep(alpha, acc_sc.shape[1]) * acc_sc[...]
                   + jnp.dot(p_all, _bf(v), preferred_element_type=F32))


def _lane_rep(x, n):
    return x if n == LANES else jnp.concatenate([x] * (n // LANES), axis=1)


def _attn_finish(m_sc, l_sc, acc_sc, o_ref, n_heads, n_groups, tq):
    w = acc_sc.shape[1]
    o = jnp.where(_lane_rep(m_sc[...], w) > 0.5 * NEG, acc_sc[...] / _lane_rep(l_sc[...], w), 0.0)
    _place_heads([o[h * tq:(h + 1) * tq] for h in range(n_heads)], n_heads, n_groups, o_ref)


def _attn_init(q_ref, q_sc, m_sc, l_sc, acc_sc, n_heads, w):
    q_sc[...] = _stack_heads(q_ref[0], n_heads, w)
    m_sc[...] = jnp.full(m_sc.shape, NEG, F32)
    l_sc[...] = jnp.zeros(l_sc.shape, F32)
    acc_sc[...] = jnp.zeros(acc_sc.shape, F32)


def _kv_tile_range(qi, *, tq, tk, qpos0, kpos0, window, n_kt):
    qlo = qpos0 + qi * tq
    last = jnp.minimum((qlo + tq - 1 - kpos0) // tk, n_kt - 1)
    if window is None:
        first = 0
    else:
        first = jnp.maximum(qlo - window + 1 - kpos0, 0) // tk
    return first, last


def _flash_kernel(*refs, cfg, rng):
    if cfg["n_mask"]:
        q_ref, kv_ref, mask_ref, o_ref, q_sc, m_sc, l_sc, acc_sc, alpha_sc = refs
    else:
        q_ref, kv_ref, o_ref, q_sc, m_sc, l_sc, acc_sc, alpha_sc = refs
        mask_ref = None
    w, tq, tk = cfg["n_groups"] * HEAD_DIM, rng["tq"], rng["tk"]
    qi, j = pl.program_id(1), pl.program_id(2)
    first, last = _kv_tile_range(qi, **rng)

    @pl.when(j == 0)
    def _():
        _attn_init(q_ref, q_sc, m_sc, l_sc, acc_sc, cfg["n_heads"], w)

    jt = first + j
    kbase = rng["kpos0"] + jt * tk
    qbase = rng["qpos0"] + qi * tq

    def tile(causal):
        kv = kv_ref[0]
        _attend_tile(q_sc[...], kv[:, :w], kv[:, w:], None if mask_ref is None else mask_ref[0], kbase, qbase,
                     m_sc, l_sc, acc_sc, alpha_sc, tq=tq, causal=causal, **cfg)

    if cfg["window"] is None:
        fully_past = kbase + tk <= qbase
        pl.when((jt <= last) & fully_past)(lambda: tile(False))
        pl.when((jt <= last) & jnp.logical_not(fully_past))(lambda: tile(True))
    else:
        pl.when(jt <= last)(lambda: tile(True))

    @pl.when(j == pl.num_programs(2) - 1)
    def _():
        _attn_finish(m_sc, l_sc, acc_sc, o_ref, cfg["n_heads"], cfg["n_groups"], tq)


def _flash_linear(q_arr, q_col, kv_arr, kv_col, mask, cfg, *, tq, tk, qpos0, kpos0, n_steps):
    b, tp, _ = q_arr.shape
    w = cfg["n_groups"] * HEAD_DIM
    n_heads = cfg["n_heads"]
    n_kt = kv_arr.shape[1] // tk
    rng = dict(tq=tq, tk=tk, qpos0=qpos0, kpos0=kpos0, window=cfg["window"], n_kt=n_kt)

    def kv_map(bi, i, j):
        first, last = _kv_tile_range(i, **rng)
        return (bi, jnp.minimum(first + j, last), kv_col)

    in_specs = [pl.BlockSpec((1, tq, n_heads * w), lambda bi, i, j: (bi, i, q_col)),
                pl.BlockSpec((1, tk, 2 * w), kv_map)]
    args = [q_arr, kv_arr]
    if cfg["n_mask"]:
        in_specs.append(pl.BlockSpec((1, cfg["n_mask"], tq, mask.shape[-1]), lambda bi, i, j: (bi, 0, i, 0)))
        args.append(mask)
    return pl.pallas_call(
        functools.partial(_flash_kernel, cfg=cfg, rng=rng),
        grid=(b, tp // tq, n_steps),
        in_specs=in_specs,
        out_specs=pl.BlockSpec((1, tq, n_heads * HEAD_DIM), lambda bi, i, j: (bi, i, 0)),
        out_shape=jax.ShapeDtypeStruct((b, tp, n_heads * HEAD_DIM), F32),
        scratch_shapes=[pltpu.VMEM((n_heads * tq, w), BF16), pltpu.VMEM((n_heads * tq, LANES), F32),
                        pltpu.VMEM((n_heads * tq, LANES), F32), pltpu.VMEM((n_heads * tq, w), F32),
                        pltpu.VMEM((n_heads * tq, LANES), F32)],
        compiler_params=_cparams("parallel", "parallel", "arbitrary"),
        name="flash",
    )(*args)


def _flash_t_kernel(q_ref, k_ref, vt_ref, rb_ref, o_ref, q_sc, m_sc, l_sc, acc_sc, *, cfg, rng):
    n_heads, n_groups, slopes = cfg["n_heads"], cfg["n_groups"], cfg["slopes"]
    w, tq, tk = n_groups * HEAD_DIM, rng["tq"], rng["tk"]
    qi, j = pl.program_id(1), pl.program_id(2)
    first, last = _kv_tile_range(qi, **rng)

    @pl.when(j == 0)
    def _():
        q_sc[...] = _stack_heads(q_ref[0], n_heads, w)
        m_sc[...] = jnp.full(m_sc.shape, NEG, F32)
        l_sc[...] = jnp.zeros(l_sc.shape, F32)
        acc_sc[...] = jnp.zeros(acc_sc.shape, F32)

    jt = first + j
    kbase = rng["kpos0"] + jt * tk
    qbase = rng["qpos0"] + qi * tq

    def tile(causal):
        s_t = _dot_nt(k_ref[0], q_sc[...])
        kb = (lax.broadcasted_iota(jnp.int32, (tk, LANES), 0) + (kbase - qbase)).astype(F32)
        rb = rb_ref[0, pl.ds(jt % SUBLANES, 1), :]
        ok = _causal_t(tk, tq, qbase - kbase) if causal else None
        _t_update(s_t, kb, ok, None, rb, vt_ref, m_sc, l_sc, acc_sc,
                  n_heads=n_heads, n_groups=n_groups, slopes=slopes, tq=tq)

    fully_past = kbase + tk <= qbase
    pl.when((jt <= last) & fully_past)(lambda: tile(False))
    pl.when((jt <= last) & jnp.logical_not(fully_past))(lambda: tile(True))

    @pl.when(j == pl.num_programs(2) - 1)
    def _():
        _t_finish(m_sc, l_sc, acc_sc, o_ref, n_heads, tq)


def _causal_t(tk, tq, delta):
    return lax.broadcasted_iota(jnp.int32, (tk, tq), 0) - lax.broadcasted_iota(jnp.int32, (tk, tq), 1) <= delta


def _t_update(s_t, kb, ok, elem_bias, rb, vt_ref, m_sc, l_sc, acc_sc, *, n_heads, n_groups, slopes, tq):
    hpg = n_heads // n_groups
    vrows = HEAD_DIM + SUBLANES
    for g in range(n_groups):
        ps, alphas = [], []
        for lt in range(g * hpg * tq // LANES, (g + 1) * hpg * tq // LANES):
            h = lt * LANES // tq
            qq0 = (lt * LANES) % tq
            lanes = slice(lt * LANES, (lt + 1) * LANES)
            s = s_t[:, lanes] + slopes[h] * kb
            if elem_bias is not None:
                s = s + elem_bias[:, g * tq + qq0:g * tq + qq0 + LANES]
            if ok is not None:
                s = jnp.where(ok[:, qq0:qq0 + LANES], s, NEG)
            m_old = m_sc[:, lanes]
            m_cur = jnp.max(s, axis=0, keepdims=True)
            if rb is not None:
                m_cur = m_cur + rb[:, lanes]
            m_new = jnp.maximum(m_old, m_cur)
            alphas.append(jnp.exp2(m_old - m_new))
            m_sc[:, lanes] = m_new
            shift = jnp.where(m_new > 0.5 * NEG, m_new if rb is None else m_new - rb[:, lanes], -NEG)
            ps.append(_bf(jnp.exp2(s - shift)))
        glanes = slice(g * hpg * tq, (g + 1) * hpg * tq)
        alpha = jnp.concatenate(alphas, axis=1)
        pv = jnp.dot(_bf(vt_ref[0, g * vrows:(g + 1) * vrows, :]), jnp.concatenate(ps, axis=1),
                     preferred_element_type=F32)
        acc_sc[:, glanes] = alpha * acc_sc[:, glanes] + pv[:HEAD_DIM]
        l_sc[:, glanes] = alpha * l_sc[:, glanes] + pv[HEAD_DIM:HEAD_DIM + 1]


def _t_finish(m_sc, l_sc, acc_sc, o_ref, n_heads, tq):
    o = jnp.where(m_sc[...] > 0.5 * NEG, acc_sc[...] / l_sc[...], 0.0)
    for pair in range(n_heads // 2):
        slab = jnp.concatenate([o[:, h * tq:(h + 1) * tq] for h in (2 * pair, 2 * pair + 1)], axis=0)
        o_ref[0, :, pair * LANES:(pair + 1) * LANES] = slab.T


def _vt_with_ones(v_t, n_groups):
    tp = v_t.shape[2]
    v = jnp.concatenate([v_t.reshape(1, n_groups, HEAD_DIM, tp), jnp.ones((1, n_groups, SUBLANES, tp), F32)], axis=2)
    return v.reshape(1, n_groups * (HEAD_DIM + SUBLANES), tp)


def _flash_transposed(q_arr, q_col, k_arr, k_col, v_t, bias_t, cfg, *, tq, tk, n_steps):
    b, tp, _ = q_arr.shape
    assert b == 1 and (1 << cfg["blk_shift"]) == tk
    w = cfg["n_groups"] * HEAD_DIM
    n_heads = cfg["n_heads"]
    nq = tp // tq
    r = n_heads * tq
    assert bias_t.shape[0] == nq and bias_t.shape[2] == r
    v_t = _vt_with_ones(v_t, cfg["n_groups"])
    vrows = v_t.shape[1]
    rng = dict(tq=tq, tk=tk, qpos0=0, kpos0=0, window=None, n_kt=tp // tk)

    def tile_of(i, j):
        first, last = _kv_tile_range(i, **rng)
        return jnp.minimum(first + j, last)

    return pl.pallas_call(
        functools.partial(_flash_t_kernel, cfg=cfg, rng=rng),
        grid=(b, nq, n_steps),
        in_specs=[pl.BlockSpec((1, tq, n_heads * w), lambda bi, i, j: (bi, i, q_col)),
                  pl.BlockSpec((1, tk, w), lambda bi, i, j: (bi, tile_of(i, j), k_col)),
                  pl.BlockSpec((1, vrows, tk), lambda bi, i, j: (bi, 0, tile_of(i, j))),
                  pl.BlockSpec((1, SUBLANES, r), lambda bi, i, j: (i, tile_of(i, j) // SUBLANES, 0))],
        out_specs=pl.BlockSpec((1, tq, n_heads * HEAD_DIM), lambda bi, i, j: (bi, i, 0)),
        out_shape=jax.ShapeDtypeStruct((b, tp, n_heads * HEAD_DIM), F32),
        scratch_shapes=[pltpu.VMEM((r, w), BF16), pltpu.VMEM((1, r), F32), pltpu.VMEM((1, r), F32),
                        pltpu.VMEM((HEAD_DIM, r), F32)],
        compiler_params=_cparams("parallel", "parallel", "arbitrary"),
        name="flash_t",
    )(q_arr, k_arr, v_t, bias_t)


def _worklist(act, tq, tk):
    nq, n_kt = act.shape
    i = np.arange(nq)[:, None]
    jt = np.arange(n_kt)[None, :]
    diag = np.minimum((i * tq + tq - 1) // tk, n_kt - 1)
    causal = jt <= diag
    s_max = int(causal.sum())
    need = (act | jnp.asarray(jt == diag)) & jnp.asarray(causal)
    key = jnp.where(need, jnp.asarray(i * n_kt + jt, jnp.int32), nq * n_kt).reshape(-1)
    key = jnp.sort(key)[:s_max]
    n_need = jnp.sum(need.astype(jnp.int32))
    pos = jnp.arange(s_max, dtype=jnp.int32)
    valid = pos < n_need
    key = jnp.where(valid, key, jnp.take(key, n_need - 1))
    qi, kt = key // n_kt, key % n_kt
    first = valid & ((pos == 0) | (qi != jnp.roll(qi, 1)))
    last = valid & ((pos == n_need - 1) | (qi != jnp.roll(qi, -1)))
    return jnp.concatenate([qi, kt, first.astype(jnp.int32), last.astype(jnp.int32),
                            valid.astype(jnp.int32)]).astype(jnp.int32), s_max


def _flash_list_kernel(wl_ref, q_ref, k_ref, vt_ref, mt_ref, o_ref, q_sc, m_sc, l_sc, acc_sc, *,
                       cfg, tq, tk, s_max):
    s = pl.program_id(0)
    n_heads, n_groups = cfg["n_heads"], cfg["n_groups"]
    w = n_groups * HEAD_DIM
    qbase = wl_ref[s] * tq
    kbase = wl_ref[s_max + s] * tk
    valid = wl_ref[4 * s_max + s] == 1

    @pl.when(wl_ref[2 * s_max + s] == 1)
    def _():
        q_sc[...] = _stack_heads(q_ref[0], n_heads, w)
        m_sc[...] = jnp.full(m_sc.shape, NEG, F32)
        l_sc[...] = jnp.zeros(l_sc.shape, F32)
        acc_sc[...] = jnp.zeros(acc_sc.shape, F32)

    def tile(causal):
        s_t = _dot_nt(k_ref[0], q_sc[...])
        kb = (lax.broadcasted_iota(jnp.int32, (tk, LANES), 0) + (kbase - qbase)).astype(F32)
        nbp = mt_ref.shape[1]
        key_blk = (kbase + lax.broadcasted_iota(jnp.int32, (tk, nbp), 0)) >> cfg["blk_shift"]
        expand = jnp.where(key_blk == lax.broadcasted_iota(jnp.int32, (tk, nbp), 1), 1.0, 0.0).astype(BF16)
        elem_bias = jnp.dot(expand, _bf(mt_ref[0]), preferred_element_type=F32)
        ok = _causal_t(tk, tq, qbase - kbase) if causal else None
        _t_update(s_t, kb, ok, elem_bias, None, vt_ref, m_sc, l_sc, acc_sc,
                  n_heads=n_heads, n_groups=n_groups, slopes=cfg["slopes"], tq=tq)

    fully_past = kbase + tk <= qbase
    pl.when(valid & fully_past)(lambda: tile(False))
    pl.when(valid & jnp.logical_not(fully_past))(lambda: tile(True))

    @pl.when(wl_ref[3 * s_max + s] == 1)
    def _():
        _t_finish(m_sc, l_sc, acc_sc, o_ref, n_heads, tq)


def _flash_worklist(q_arr, q_col, k_arr, k_col, v_t, bias_t, act, cfg, *, tq, tk):
    b, tp, _ = q_arr.shape
    assert b == 1 and cfg["n_mask"] == cfg["n_groups"] and cfg["window"] is None
    w = cfg["n_groups"] * HEAD_DIM
    n_heads = cfg["n_heads"]
    r = n_heads * tq
    v_t = _vt_with_ones(v_t, cfg["n_groups"])
    wl, s_max = _worklist(act, tq, tk)
    grid_spec = pltpu.PrefetchScalarGridSpec(
        num_scalar_prefetch=1,
        grid=(s_max,),
        in_specs=[pl.BlockSpec((1, tq, n_heads * w), lambda s, wl: (0, wl[s], q_col)),
                  pl.BlockSpec((1, tk, w), lambda s, wl: (0, wl[s_max + s], k_col)),
                  pl.BlockSpec((1, v_t.shape[1], tk), lambda s, wl: (0, 0, wl[s_max + s])),
                  pl.BlockSpec((1,) + bias_t.shape[1:], lambda s, wl: (wl[s], 0, 0))],
        out_specs=pl.BlockSpec((1, tq, n_heads * HEAD_DIM), lambda s, wl: (0, wl[s], 0)),
        scratch_shapes=[pltpu.VMEM((r, w), BF16), pltpu.VMEM((1, r), F32), pltpu.VMEM((1, r), F32),
                        pltpu.VMEM((HEAD_DIM, r), F32)],
    )
    return pl.pallas_call(
        functools.partial(_flash_list_kernel, cfg=cfg, tq=tq, tk=tk, s_max=s_max),
        grid_spec=grid_spec,
        out_shape=jax.ShapeDtypeStruct((b, tp, n_heads * HEAD_DIM), F32),
        compiler_params=_cparams("arbitrary"),
        name="flash_list",
    )(wl, q_arr, k_arr, v_t, bias_t)


DECODE_PAGES_PER_STEP = 32


def _decode_update(qs, k_op, v_op, transposed, kbase, qpos_col, slope_col, mrow, m_sc, l_sc, acc_sc, *,
                   blk_shift, window, qpos0):
    s = jnp.dot(qs, _bf(k_op), preferred_element_type=F32) if transposed else _dot_nt(qs, k_op)
    tk = s.shape[1]
    kpos = kbase + lax.broadcasted_iota(jnp.int32, (1, tk), 1)
    d = qpos_col - kpos
    ok = d >= 0
    if window is not None:
        ok = ok & (d < window)
    if mrow is not None:
        nbp = mrow.shape[1]
        expand = jnp.where(lax.broadcasted_iota(jnp.int32, (nbp, tk), 0) == (kpos >> blk_shift), 1.0, 0.0)
        ok = ok & (jnp.dot(mrow, expand.astype(BF16), preferred_element_type=F32) > 0.5)
    s = jnp.where(ok, s + slope_col * (kpos - qpos0).astype(F32), NEG)
    m_old = m_sc[...]
    m_new = jnp.maximum(m_old, jnp.max(s, axis=-1, keepdims=True))
    alpha = jnp.exp2(m_old - m_new)
    p = _bf(jnp.exp2(s - m_new))
    l_sc[...] = alpha * l_sc[...] + jnp.sum(p.astype(F32), axis=-1, keepdims=True)
    m_sc[...] = m_new
    pv = _dot_nt(p, v_op) if transposed else jnp.dot(p, _bf(v_op), preferred_element_type=F32)
    acc_sc[...] = alpha * acc_sc[...] + pv


def _decode_kernel(*refs, cfg, npp, tq, qpos0, kpos0, tile_rows):
    has_mask = bool(cfg["n_mask"])
    page_refs = refs[2:2 + npp]
    q_ref, tail_ref = refs[1], refs[2 + npp]
    rest = refs[3 + npp:]
    if has_mask:
        mask_ref, o_ref, q_sc, m_sc, l_sc, acc_sc, slope_sc, mrow_sc = rest
    else:
        o_ref, q_sc, m_sc, l_sc, acc_sc, slope_sc = rest
    n_heads = cfg["n_heads"]
    w = cfg["n_groups"] * HEAD_DIM
    j = pl.program_id(1)
    n_steps = pl.num_programs(1)
    row = lax.broadcasted_iota(jnp.int32, (n_heads * tq, 1), 0)

    @pl.when(j == 0)
    def _():
        _attn_init(q_ref, q_sc, m_sc, l_sc, acc_sc, n_heads, w)
        slope = jnp.zeros((n_heads * tq, 1), F32)
        for h in range(n_heads):
            slope = jnp.where(row // tq == h, cfg["slopes"][h], slope)
        slope_sc[...] = slope
        if has_mask:
            hpm = n_heads // cfg["n_mask"]
            mrow_sc[...] = _bf(jnp.concatenate([mask_ref[0, h // hpm] for h in range(n_heads)], axis=0))

    qpos_col = qpos0 + row % tq
    kw = dict(blk_shift=cfg["blk_shift"], window=cfg["window"], qpos0=qpos0)
    mrow = mrow_sc[...] if has_mask else None

    @pl.when(j < n_steps - 1)
    def _():
        kt = [r[0, :w, :] for r in page_refs]
        vt = [r[0, w:, :] for r in page_refs]
        kt = kt[0] if npp == 1 else jnp.concatenate(kt, axis=1)
        vt = vt[0] if npp == 1 else jnp.concatenate(vt, axis=1)
        _decode_update(q_sc[...], kt, vt, True, kpos0 + j * (npp * tile_rows), qpos_col, slope_sc[...], mrow,
                       m_sc, l_sc, acc_sc, **kw)

    @pl.when(j == n_steps - 1)
    def _():
        tail = tail_ref[0]
        _decode_update(q_sc[...], tail[:, :w], tail[:, w:], False, qpos0, qpos_col, slope_sc[...], mrow,
                       m_sc, l_sc, acc_sc, **kw)
        o = jnp.where(m_sc[...] > 0.5 * NEG, acc_sc[...] / l_sc[...], 0.0)
        _place_heads([o[h * tq:(h + 1) * tq] for h in range(n_heads)], n_heads, cfg["n_groups"], o_ref)


def _decode_attn(proj3, q_col, tail_col, tiles_t, table, base, mask, cfg, *, qpos0, kpos0):
    b, tq, _ = proj3.shape
    w = cfg["n_groups"] * HEAD_DIM
    n_heads = cfg["n_heads"]
    n_tiles = table.shape[1]
    tile_rows = tiles_t.shape[2]
    npp = math.gcd(DECODE_PAGES_PER_STEP, n_tiles)
    steps = n_tiles // npp
    r = n_heads * tq

    def tile_map(k):
        return lambda bi, j, tb: (base + tb[bi * n_tiles + jnp.minimum(j, steps - 1) * npp + k], 0, 0)

    in_specs = ([pl.BlockSpec((1, tq, n_heads * w), lambda bi, j, tb: (bi, 0, q_col))]
                + [pl.BlockSpec((1, 2 * w, tile_rows), tile_map(k)) for k in range(npp)]
                + [pl.BlockSpec((1, tq, 2 * w), lambda bi, j, tb: (bi, 0, tail_col))])
    args = [table.reshape(-1), proj3] + [tiles_t] * npp + [proj3]
    scratch = [pltpu.VMEM((r, w), BF16), pltpu.VMEM((r, 1), F32), pltpu.VMEM((r, 1), F32), pltpu.VMEM((r, w), F32),
               pltpu.VMEM((r, 1), F32)]
    if cfg["n_mask"]:
        in_specs.append(pl.BlockSpec((1, cfg["n_mask"], tq, mask.shape[-1]), lambda bi, j, tb: (bi, 0, 0, 0)))
        args.append(mask)
        scratch.append(pltpu.VMEM((r, mask.shape[-1]), BF16))
    grid_spec = pltpu.PrefetchScalarGridSpec(
        num_scalar_prefetch=1,
        grid=(b, steps + 1),
        in_specs=in_specs,
        out_specs=pl.BlockSpec((1, tq, n_heads * HEAD_DIM), lambda bi, j, tb: (bi, 0, 0)),
        scratch_shapes=scratch,
    )
    return pl.pallas_call(
        functools.partial(_decode_kernel, cfg=cfg, npp=npp, tq=tq, qpos0=qpos0, kpos0=kpos0, tile_rows=tile_rows),
        grid_spec=grid_spec,
        out_shape=jax.ShapeDtypeStruct((b, tq, n_heads * HEAD_DIM), F32),
        compiler_params=_cparams("parallel", "arbitrary"),
        name="decode_attn",
    )(*args)


def _attn_cfg(kind):
    if kind == "moba":
        return dict(n_heads=MOBA_HEADS, n_groups=MOBA_KV_HEADS, n_mask=MOBA_HEADS,
                    blk_shift=int(math.log2(MOBA_BLOCK)), window=None, slopes=SLOPES_B)
    if kind == "sel":
        return dict(n_heads=NSA_HEADS, n_groups=NSA_KV_HEADS, n_mask=NSA_KV_HEADS,
                    blk_shift=int(math.log2(NSA_SEL_BLOCK)), window=None, slopes=SLOPES_A)
    return dict(n_heads=NSA_HEADS, n_groups=NSA_KV_HEADS, n_mask=0, blk_shift=0, window=NSA_WINDOW, slopes=SLOPES_A)


S5_CHUNK = 4 * LANES
S5_GCHUNK = S5_CHUNK // S5_STATE


def _s5_params(lp):
    lr, li = lp["s5_lambda_re"], lp["s5_lambda_im"]
    dt = jnp.exp(lp["s5_log_dt"])[:, None]
    mag = jnp.exp(lr * dt)
    a_re = mag * jnp.cos(li * dt)
    a_im = mag * jnp.sin(li * dt)
    den = lr * lr + li * li
    f_re = ((a_re - 1.0) * lr + a_im * li) / den
    f_im = (a_im * lr - (a_re - 1.0) * li) / den
    b_re, b_im = lp["s5_b_re"], lp["s5_b_im"]
    bb_re = f_re[..., None] * b_re - f_im[..., None] * b_im
    bb_im = f_re[..., None] * b_im + f_im[..., None] * b_re
    n_ch = S5_GROUPS // S5_GCHUNK
    eye = np.eye(S5_GCHUNK, dtype=np.float32)

    def in_w(bb):
        x = bb.reshape(n_ch, S5_GCHUNK, S5_STATE, S5_GROUP_CH).transpose(0, 1, 3, 2)
        x = x[:, :, :, None, :] * eye[None, :, None, :, None]
        return x.reshape(n_ch, S5_GCHUNK * S5_GROUP_CH, S5_CHUNK)

    def out_w(c):
        x = c.reshape(n_ch, S5_GCHUNK, S5_GROUP_CH, S5_STATE).transpose(0, 1, 3, 2)
        x = x[:, :, :, None, :] * eye[None, :, None, :, None]
        return x.reshape(n_ch, S5_CHUNK, S5_GCHUNK * S5_GROUP_CH).astype(BF16)

    return dict(a_re=a_re.reshape(1, S5_LANES), a_im=a_im.reshape(1, S5_LANES),
                bw_re=in_w(bb_re), bw_im=in_w(bb_im),
                cw_re=out_w(lp["s5_c_re"]), cw_im=out_w(lp["s5_c_im"]),
                d=lp["s5_d"].reshape(1, S5_WIDTH))


def _s5_out_proj(h_re, h_im, cwr_ref, cwi_ref):
    cols = []
    for c in range(S5_LANES // S5_CHUNK):
        sl = slice(c * S5_CHUNK, (c + 1) * S5_CHUNK)
        cols.append(jnp.dot(_bf(h_re[:, sl]), cwr_ref[c], preferred_element_type=F32)
                    - jnp.dot(_bf(h_im[:, sl]), cwi_ref[c], preferred_element_type=F32))
    return jnp.concatenate(cols, axis=1)


def _s5scan_kernel(u_ref, h0r_ref, h0i_ref, ar_ref, ai_ref, bwr_ref, bwi_ref, cwr_ref, cwi_ref, d_ref,
                   y_ref, hr_ref, hi_ref, bur_sc, bui_sc, hr_sc, hi_sc, *, n_par, n_j):
    @pl.when(pl.program_id(0) == 0)
    def _():
        hr_sc[...] = h0r_ref[...]
        hi_sc[...] = h0i_ref[...]

    u = u_ref[...]
    gw = S5_GCHUNK * S5_GROUP_CH
    for c in range(S5_LANES // S5_CHUNK):
        uc = u[:, c * gw:(c + 1) * gw]
        bur_sc[:, c * S5_CHUNK:(c + 1) * S5_CHUNK] = _dot3(uc, bwr_ref[c])
        bui_sc[:, c * S5_CHUNK:(c + 1) * S5_CHUNK] = _dot3(uc, bwi_ref[c])

    for c in range(S5_LANES // S5_CHUNK):
        sl = slice(c * S5_CHUNK, (c + 1) * S5_CHUNK)
        a_re = jnp.broadcast_to(ar_ref[:, sl], (n_par, S5_CHUNK))
        a_im = jnp.broadcast_to(ai_ref[:, sl], (n_par, S5_CHUNK))

        def step(jj, carry):
            h_re, h_im = carry
            rows = pl.ds(pl.multiple_of(jj * n_par, n_par), n_par)
            n_re = a_re * h_re - a_im * h_im + bur_sc[rows, sl]
            n_im = a_re * h_im + a_im * h_re + bui_sc[rows, sl]
            bur_sc[rows, sl] = n_re
            bui_sc[rows, sl] = n_im
            return n_re, n_im

        h_re, h_im = lax.fori_loop(0, n_j, step, (hr_sc[:, sl], hi_sc[:, sl]))
        hr_sc[:, sl] = h_re
        hi_sc[:, sl] = h_im

    y_ref[...] = _s5_out_proj(bur_sc[...], bui_sc[...], cwr_ref, cwi_ref) + d_ref[...] * u
    hr_ref[...] = hr_sc[...]
    hi_ref[...] = hi_sc[...]


def _s5_scan(u_rows, h0_re, h0_im, sp, n_par, jc):
    rows = u_rows.shape[0]
    n_j_total = rows // n_par
    steps = n_j_total // jc
    full = lambda a: pl.BlockSpec(a.shape, lambda i: (0,) * a.ndim)
    ins = [u_rows, h0_re, h0_im, sp["a_re"], sp["a_im"], sp["bw_re"], sp["bw_im"], sp["cw_re"], sp["cw_im"], sp["d"]]
    return pl.pallas_call(
        functools.partial(_s5scan_kernel, n_par=n_par, n_j=jc),
        grid=(steps,),
        in_specs=[pl.BlockSpec((jc * n_par, S5_WIDTH), lambda i: (i, 0))] + [full(a) for a in ins[1:]],
        out_specs=[pl.BlockSpec((jc * n_par, S5_WIDTH), lambda i: (i, 0)),
                   pl.BlockSpec((n_par, S5_LANES), lambda i: (0, 0)),
                   pl.BlockSpec((n_par, S5_LANES), lambda i: (0, 0))],
        out_shape=[jax.ShapeDtypeStruct((rows, S5_WIDTH), F32),
                   jax.ShapeDtypeStruct((n_par, S5_LANES), F32),
                   jax.ShapeDtypeStruct((n_par, S5_LANES), F32)],
        scratch_shapes=[pltpu.VMEM((jc * n_par, S5_LANES), F32), pltpu.VMEM((jc * n_par, S5_LANES), F32),
                        pltpu.VMEM((n_par, S5_LANES), F32), pltpu.VMEM((n_par, S5_LANES), F32)],
        compiler_params=_cparams("arbitrary"),
        name="s5_scan",
    )(*ins)


def _s5fix_kernel(y_ref, er_ref, ei_ref, ar_ref, ai_ref, cwr_ref, cwi_ref, o_ref, fr_ref, fi_ref,
                  cr_sc, ci_sc, pr_sc, pi_sc, xr_sc, xi_sc, *, n_seg, n_j, n_j_total):
    a_re = ar_ref[...]
    a_im = ai_ref[...]

    @pl.when(pl.program_id(0) == 0)
    def _():
        def pw_step(_, carry):
            p_re, p_im = carry
            return a_re * p_re - a_im * p_im, a_re * p_im + a_im * p_re

        al_re, al_im = lax.fori_loop(0, n_j_total - 1, pw_step, (a_re, a_im))
        c_re = jnp.zeros((1, S5_LANES), F32)
        c_im = jnp.zeros((1, S5_LANES), F32)
        for s in range(n_seg):
            cr_sc[s:s + 1, :] = c_re
            ci_sc[s:s + 1, :] = c_im
            e_re, e_im = er_ref[s:s + 1, :], ei_ref[s:s + 1, :]
            c_re, c_im = (e_re + al_re * c_re - al_im * c_im, e_im + al_re * c_im + al_im * c_re)
        fr_ref[...] = c_re
        fi_ref[...] = c_im
        pr_sc[...] = jnp.broadcast_to(a_re, pr_sc.shape)
        pi_sc[...] = jnp.broadcast_to(a_im, pi_sc.shape)

    for c in range(S5_LANES // S5_CHUNK):
        sl = slice(c * S5_CHUNK, (c + 1) * S5_CHUNK)
        ab_re = jnp.broadcast_to(a_re[:, sl], (n_seg, S5_CHUNK))
        ab_im = jnp.broadcast_to(a_im[:, sl], (n_seg, S5_CHUNK))
        c_re, c_im = cr_sc[:, sl], ci_sc[:, sl]

        def step(jj, carry):
            p_re, p_im = carry
            rows = pl.ds(pl.multiple_of(jj * n_seg, n_seg), n_seg)
            xr_sc[rows, sl] = p_re * c_re - p_im * c_im
            xi_sc[rows, sl] = p_re * c_im + p_im * c_re
            return ab_re * p_re - ab_im * p_im, ab_re * p_im + ab_im * p_re

        p_re, p_im = lax.fori_loop(0, n_j, step, (pr_sc[:, sl], pi_sc[:, sl]))
        pr_sc[:, sl] = p_re
        pi_sc[:, sl] = p_im

    o_ref[...] = y_ref[...] + _s5_out_proj(xr_sc[...], xi_sc[...], cwr_ref, cwi_ref)


def _s5_fix(y_rows, end_re, end_im, sp, n_seg, jc):
    rows = y_rows.shape[0]
    n_j_total = rows // n_seg
    steps = n_j_total // jc
    full = lambda a: pl.BlockSpec(a.shape, lambda i: (0,) * a.ndim)
    ins = [y_rows, end_re, end_im, sp["a_re"], sp["a_im"], sp["cw_re"], sp["cw_im"]]
    seg = lambda: pltpu.VMEM((n_seg, S5_LANES), F32)
    return pl.pallas_call(
        functools.partial(_s5fix_kernel, n_seg=n_seg, n_j=jc, n_j_total=n_j_total),
        grid=(steps,),
        in_specs=[pl.BlockSpec((jc * n_seg, S5_WIDTH), lambda i: (i, 0))] + [full(a) for a in ins[1:]],
        out_specs=[pl.BlockSpec((jc * n_seg, S5_WIDTH), lambda i: (i, 0)),
                   pl.BlockSpec((1, S5_LANES), lambda i: (0, 0)),
                   pl.BlockSpec((1, S5_LANES), lambda i: (0, 0))],
        out_shape=[jax.ShapeDtypeStruct((rows, S5_WIDTH), F32),
                   jax.ShapeDtypeStruct((1, S5_LANES), F32),
                   jax.ShapeDtypeStruct((1, S5_LANES), F32)],
        scratch_shapes=[seg(), seg(), seg(), seg(),
                        pltpu.VMEM((jc * n_seg, S5_LANES), F32), pltpu.VMEM((jc * n_seg, S5_LANES), F32)],
        compiler_params=_cparams("arbitrary"),
        name="s5_fix",
    )(*ins)


S5_SEGMENTS = SUBLANES


def _s5_prompt(u, sp):
    t = u.shape[0]
    n_seg = S5_SEGMENTS
    seg_len = t // n_seg
    jc = min(64, seg_len)
    u_rows = u.reshape(n_seg, seg_len, S5_WIDTH).transpose(1, 0, 2).reshape(t, S5_WIDTH)
    zero = jnp.zeros((n_seg, S5_LANES), F32)
    y_loc, end_re, end_im = _s5_scan(u_rows, zero, zero, sp, n_seg, jc)
    y_rows, f_re, f_im = _s5_fix(y_loc, end_re, end_im, sp, n_seg, jc)
    y = y_rows.reshape(seg_len, n_seg, S5_WIDTH).transpose(1, 0, 2).reshape(t, S5_WIDTH)
    return y, f_re, f_im


def _s5_sample(u3, h0_re, h0_im, sp):
    b, t, _ = u3.shape
    u_rows = u3.transpose(1, 0, 2).reshape(t * b, S5_WIDTH)
    y_rows, h_re, h_im = _s5_scan(u_rows, h0_re, h0_im, sp, b, t)
    return y_rows.reshape(t, b, S5_WIDTH).transpose(1, 0, 2), h_re, h_im


def _merge_kernel(x_ref, oc_ref, os_ref, ow_ref, om_ref, y5_ref, ga_ref, gm_ref, wglu_ref, bglu_ref,
                  wn_ref, wm_ref, w5_ref, wo_ref, g_ref, b_ref, o_ref):
    qw = NSA_Q_W
    ga = ga_ref[...]
    o_nsa = (_sigmoid(ga[:, :qw]) * oc_ref[...] + _sigmoid(ga[:, qw:2 * qw]) * os_ref[...]
             + _sigmoid(ga[:, 2 * qw:]) * ow_ref[...])
    z = _gelu(y5_ref[...])
    o_s5 = z * _sigmoid(jnp.dot(_bf(z), wglu_ref[...], preferred_element_type=F32) + bglu_ref[...])
    gm = gm_ref[...]
    d = D_MODEL
    merged = (_sigmoid(gm[:, :d]) * jnp.dot(_bf(o_nsa), wn_ref[...], preferred_element_type=F32)
              + _sigmoid(gm[:, d:2 * d]) * jnp.dot(_bf(om_ref[...]), wm_ref[...], preferred_element_type=F32)
              + _sigmoid(gm[:, 2 * d:]) * jnp.dot(_bf(o_s5), w5_ref[...], preferred_element_type=F32))
    mix = jnp.dot(_bf(merged), wo_ref[...], preferred_element_type=F32)
    o_ref[...] = _layer_norm(DN_ALPHA * x_ref[...] + mix, g_ref[...], b_ref[...])


def _merge(x2, o_c, o_s, o_w, o_m, y5, proj2, lp, tm):
    m = x2.shape[0]
    row = lambda n: pl.BlockSpec((tm, n), lambda i: (i, 0))
    full = lambda a: pl.BlockSpec(a.shape, lambda i: (0,) * a.ndim)
    ws = [lp["s5_w_glu"].astype(BF16), lp["s5_b_glu"].reshape(1, S5_WIDTH),
          lp["w_br_nsa"].astype(BF16), lp["w_br_moba"].astype(BF16), lp["w_br_s5"].astype(BF16),
          lp["w_out"].astype(BF16), lp["ln1_g"].reshape(1, D_MODEL), lp["ln1_b"].reshape(1, D_MODEL)]
    gaw, gmw = 3 * NSA_Q_W, N_BRANCH * D_MODEL
    return pl.pallas_call(
        _merge_kernel,
        grid=(m // tm,),
        in_specs=[row(D_MODEL), row(NSA_Q_W), row(NSA_Q_W), row(NSA_Q_W), row(MOBA_Q_W), row(S5_WIDTH),
                  pl.BlockSpec((tm, gaw), lambda i: (i, PC_GA // gaw)),
                  pl.BlockSpec((tm, gmw), lambda i: (i, PC_GM // gmw))] + [full(a) for a in ws],
        out_specs=row(D_MODEL),
        out_shape=jax.ShapeDtypeStruct((m, D_MODEL), F32),
        compiler_params=_cparams("parallel"),
        name="merge",
    )(x2, o_c, o_s, o_w, o_m, y5, proj2, proj2, *ws)


def _router_kernel(x_ref, w_ref, b_ref, o_ref):
    logits = _dot3(x_ref[...], w_ref[...]) + b_ref[...]
    lane = lax.broadcasted_iota(jnp.int32, logits.shape, 1)
    lanef = lane.astype(F32)
    logits = jnp.where(lane < N_EXPERTS, logits, -jnp.inf)
    sel = _top_select(logits, lanef, MOE_TOP_K)
    m = jnp.max(logits, axis=-1, keepdims=True)
    e = jnp.where(sel > 0.5, jnp.exp(logits - m), 0.0)
    o_ref[...] = e / jnp.sum(e, axis=-1, keepdims=True)


def _router(x2, w_router, b_router, tm):
    m = x2.shape[0]
    w = jnp.pad(w_router, ((0, 0), (0, LANES - N_EXPERTS)))
    bias = jnp.pad(b_router, (0, LANES - N_EXPERTS)).reshape(1, LANES)
    return pl.pallas_call(
        _router_kernel,
        grid=(m // tm,),
        in_specs=[pl.BlockSpec((tm, D_MODEL), lambda i: (i, 0)),
                  pl.BlockSpec(w.shape, lambda i: (0, 0)), pl.BlockSpec(bias.shape, lambda i: (0, 0))],
        out_specs=pl.BlockSpec((tm, LANES), lambda i: (i, 0)),
        out_shape=jax.ShapeDtypeStruct((m, LANES), F32),
        compiler_params=_cparams("parallel"),
        name="router",
    )(x2, w, bias)


def _ffn_kernel(*refs, use_comb):
    if use_comb:
        x_ref, comb_ref, wg_ref, wu_ref, wd_ref, g_ref, b_ref, o_ref, acc_sc = refs
    else:
        x_ref, wg_ref, wu_ref, wd_ref, g_ref, b_ref, o_ref, acc_sc = refs
    f = pl.program_id(1)

    @pl.when(f == 0)
    def _():
        acc_sc[...] = jnp.zeros(acc_sc.shape, F32)

    xb = _bf(x_ref[...])
    wg = wg_ref[0] if use_comb else wg_ref[...]
    wu = wu_ref[0] if use_comb else wu_ref[...]
    wd = wd_ref[0] if use_comb else wd_ref[...]
    gate = jnp.dot(xb, _bf(wg), preferred_element_type=F32)
    up = jnp.dot(xb, _bf(wu), preferred_element_type=F32)
    h = gate * _sigmoid(gate) * up
    if use_comb:
        comb = comb_ref[...]
        lane = lax.broadcasted_iota(jnp.int32, comb.shape, 1)
        h = h * jnp.sum(jnp.where(lane == f, comb, 0.0), axis=-1, keepdims=True)
    acc_sc[...] += jnp.dot(_bf(h), _bf(wd), preferred_element_type=F32)

    @pl.when(f == pl.num_programs(1) - 1)
    def _():
        o_ref[...] = _layer_norm(DN_ALPHA * x_ref[...] + acc_sc[...], g_ref[...], b_ref[...])


def _ffn(x2, comb, wg, wu, wd, ln_g, ln_b, tm, tf):
    m = x2.shape[0]
    use_comb = comb is not None
    if use_comb:
        n_f = wg.shape[0]
        w_specs = [pl.BlockSpec((1,) + wg.shape[1:], lambda i, f: (f, 0, 0)),
                   pl.BlockSpec((1,) + wu.shape[1:], lambda i, f: (f, 0, 0)),
                   pl.BlockSpec((1,) + wd.shape[1:], lambda i, f: (f, 0, 0))]
    else:
        n_f = wg.shape[1] // tf
        w_specs = [pl.BlockSpec((D_MODEL, tf), lambda i, f: (0, f)),
                   pl.BlockSpec((D_MODEL, tf), lambda i, f: (0, f)),
                   pl.BlockSpec((tf, D_MODEL), lambda i, f: (f, 0))]
    vec = pl.BlockSpec((1, D_MODEL), lambda i, f: (0, 0))
    in_specs = [pl.BlockSpec((tm, D_MODEL), lambda i, f: (i, 0))]
    args = [x2]
    if use_comb:
        in_specs.append(pl.BlockSpec((tm, LANES), lambda i, f: (i, 0)))
        args.append(comb)
    return pl.pallas_call(
        functools.partial(_ffn_kernel, use_comb=use_comb),
        grid=(m // tm, n_f),
        in_specs=in_specs + w_specs + [vec, vec],
        out_specs=pl.BlockSpec((tm, D_MODEL), lambda i, f: (i, 0)),
        out_shape=jax.ShapeDtypeStruct((m, D_MODEL), F32),
        scratch_shapes=[pltpu.VMEM((tm, D_MODEL), F32)],
        compiler_params=_cparams("parallel", "arbitrary"),
        name="ffn",
    )(*args, wg.astype(BF16), wu.astype(BF16), wd.astype(BF16), ln_g.reshape(1, D_MODEL), ln_b.reshape(1, D_MODEL))


def _token_mixer(x3, t_real, qpos0, past, lp, sp, h0_re, h0_im):
    b, tp, _ = x3.shape
    rows = b * tp
    x2 = x3.reshape(rows, D_MODEL)
    proj2 = _matmul(x2, lp["w_proj"], min(256, rows), PC_END // 2)
    proj3 = proj2.reshape(b, tp, PC_END)
    kva = proj3[:, :t_real, PC_KVA:PC_END]
    new_cmp, new_sel, new_win = (kva[:, :, i * 2 * NSA_KV_W:(i + 1) * 2 * NSA_KV_W] for i in range(3))
    new_moba = proj3[:, :t_real, PC_KVB:PC_KVA]
    cw = _cmp_weights(lp["phi1_k"], lp["phi1_v"])
    cmp_args = (lp["pe_k"], lp["pe_v"], lp["phi1_k"], lp["phi1_v"], lp["phi2_k"], lp["phi2_v"])
    rw_a, rw_b = 2 * NSA_KV_W, 2 * MOBA_KV_W

    qa_col = PC_QA // (NSA_HEADS * NSA_KV_W)
    qb_col = PC_QB // (MOBA_HEADS * MOBA_KV_W)
    if past is None:
        assert qpos0 == 0 and b == 1 and t_real == tp and t_real % MOBA_BLOCK == 0
        tq_sel = 256 if tp % 256 == 0 else 128
        tq_moba = 1024 if tp % 1024 == 0 else tq_sel
        tq = 256 if tp % 256 == 0 else 128
        tk = 512 if tp % 512 == 0 else tp
        ab = _cmp_ab_linear(new_cmp.reshape(t_real // NSA_CMP_STRIDE, NSA_CMP_STRIDE * rw_a), cw)
        kc = _cmp_finish(ab, 1, *cmp_args)
        n_kt = tp // tk
        assert n_kt <= LANES
        assert tq_sel == tq
        o_c, sel_bias, act = _cmp_attention(proj3, kc, 0, t_real, tq, act_blocks=tk // NSA_SEL_BLOCK, bias_t=True)
        act = act[0, :, 0, :n_kt] > 0.5
        sel_v_t = proj3[:, :, PC_KVA + rw_a + NSA_KV_W:PC_KVA + 2 * rw_a].transpose(0, 2, 1)
        o_s = _flash_worklist(proj3, qa_col, proj3, (PC_KVA + rw_a) // NSA_KV_W, sel_v_t, sel_bias, act,
                              _attn_cfg("sel"), tq=tq, tk=tk)
        tk_win = 256 if tp % 256 == 0 else tk
        n_win = min(tp // tk_win, (NSA_WINDOW + tq - 2) // tk_win + 2)
        o_w = _flash_linear(proj3, qa_col, proj3, (PC_KVA + 2 * rw_a) // rw_a, None,
                            _attn_cfg("win"), tq=tq, tk=tk_win, qpos0=0, kpos0=0, n_steps=n_win)
        kmean = _moba_kmean(proj3, None)
        mb_bias = _moba_gate(proj3, kmean, 0, t_real, tq_moba, True)
        v_t = proj3[:, :, PC_KVB + MOBA_KV_W:PC_KVA].transpose(0, 2, 1)
        o_m = _flash_transposed(proj3, qb_col, proj3, PC_KVB // MOBA_KV_W, v_t, mb_bias, _attn_cfg("moba"),
                                tq=tq_moba, tk=MOBA_BLOCK, n_steps=tp // MOBA_BLOCK)
        y5, h_re, h_im = _s5_prompt(proj2[:, PC_U:PC_U + S5_WIDTH], sp)
        keep = min(NSA_WINDOW, t_real)
        win_state = new_win[:, t_real - keep:]
    else:
        assert t_real < NSA_CMP_STRIDE and qpos0 % MOBA_BLOCK == 0 and tp % SUBLANES == 0
        pt, base = past["page_table"], past["page_base"]
        ab = _cmp_ab_paged(past["cmp_t"], base, pt, cw)
        kc = _cmp_finish(ab, b, *cmp_args)
        o_c, sel_mask, _ = _cmp_attention(proj3, kc, qpos0, t_real, tp)
        o_s = _decode_attn(proj3, qa_col, (PC_KVA + rw_a) // rw_a, past["sel_t"], pt, base, sel_mask,
                           _attn_cfg("sel"), qpos0=qpos0, kpos0=0)
        wb = past["win_t"].shape[2]
        o_w = _decode_attn(proj3, qa_col, (PC_KVA + 2 * rw_a) // rw_a, past["win_t"],
                           jnp.arange(b, dtype=jnp.int32).reshape(b, 1), past["win_base"], None,
                           _attn_cfg("win"), qpos0=qpos0, kpos0=qpos0 - wb)
        kmean = _moba_kmean(past["moba_t"], pt, base)
        mb_mask = _moba_gate(proj3, kmean, qpos0, t_real, tp, False)
        o_m = _decode_attn(proj3, qb_col, PC_KVB // rw_b, past["moba_t"], pt, base, mb_mask,
                           _attn_cfg("moba"), qpos0=qpos0, kpos0=0)
        y5r, h_re, h_im = _s5_sample(proj3[:, :t_real, PC_U:PC_U + S5_WIDTH], h0_re, h0_im, sp)
        y5 = jnp.pad(y5r, ((0, 0), (0, tp - t_real), (0, 0))).reshape(rows, S5_WIDTH)
        win_state = jnp.concatenate([past["win_rows"], new_win], axis=1)[:, -wb:]

    flat = lambda a: a.reshape(rows, a.shape[-1])
    x1 = _merge(x2, flat(o_c), flat(o_s), flat(o_w), flat(o_m), y5, proj2, lp, min(256, rows))
    return x1, (new_cmp, new_sel, win_state, new_moba, h_re, h_im)


def _layer(l, x3, t_real, qpos0, past, h0_re, h0_im, lp, sp, ffn_p):
    b, tp, _ = x3.shape
    x1, st = _token_mixer(x3, t_real, qpos0, past, lp, sp, h0_re, h0_im)
    tm = min(512, x1.shape[0])
    if l % 2 == 0:
        x2 = _ffn(x1, None, ffn_p["wg"], ffn_p["wu"], ffn_p["wd"], lp["ln2_g"], lp["ln2_b"], tm, ffn_p["tf"])
    else:
        comb = _router(x1, ffn_p["w_router"], ffn_p["b_router"], tm)
        x2 = _ffn(x1, comb, ffn_p["wg"], ffn_p["wu"], ffn_p["wd"], lp["ln2_g"], lp["ln2_b"], tm, None)
    return x2.reshape(b, tp, D_MODEL), st


def kernel(x_prompt, x_sample, cache_nsa_cmp, cache_nsa_sel, cache_nsa_win, cache_moba, state_s5_re, state_s5_im,
           page_table, w_in, pe_k, phi1_k, phi2_k, pe_v, phi1_v, phi2_v, s5_lambda_re, s5_lambda_im, s5_log_dt,
           s5_b_re, s5_b_im, s5_c_re, s5_c_im, s5_d, s5_w_glu, s5_b_glu, w_br_nsa, w_br_moba, w_br_s5, w_out,
           ln1_g, ln1_b, ln2_g, ln2_b, ffn_w_gate, ffn_w_up, ffn_w_down, moe_w_router, moe_b_router,
           moe_w_gate, moe_w_up, moe_w_down):
    depth = w_in.shape[0]
    bp, seq, _ = x_prompt.shape
    bs, dec_seq, _ = x_sample.shape
    n_pool = cache_moba.shape[1]
    past_len = page_table.shape[1] * PAGE_SIZE
    tps = _round_up(dec_seq, SUBLANES)
    y_p = x_prompt
    y_s = jnp.pad(x_sample, ((0, 0), (0, tps - dec_seq), (0, 0)))
    rw_a, rw_b = 2 * NSA_KV_W, 2 * MOBA_KV_W

    def tiles_t(cache):
        d, n, r = cache.shape[:3]
        return cache.transpose(0, 1, 3, 4, 5, 2).reshape(d * n, -1, r)

    cmp_t, sel_t, moba_t, win_t = (tiles_t(c) for c in (cache_nsa_cmp, cache_nsa_sel, cache_moba, cache_nsa_win))
    p_st, s_st = [], []
    for l in range(depth):
        lp = dict(w_proj=_proj_weight(w_in[l]), pe_k=pe_k[l], phi1_k=phi1_k[l], phi2_k=phi2_k[l], pe_v=pe_v[l],
                  phi1_v=phi1_v[l], phi2_v=phi2_v[l], s5_lambda_re=s5_lambda_re[l], s5_lambda_im=s5_lambda_im[l],
                  s5_log_dt=s5_log_dt[l], s5_b_re=s5_b_re[l], s5_b_im=s5_b_im[l], s5_c_re=s5_c_re[l],
                  s5_c_im=s5_c_im[l], s5_d=s5_d[l], s5_w_glu=s5_w_glu[l], s5_b_glu=s5_b_glu[l],
                  w_br_nsa=w_br_nsa[l], w_br_moba=w_br_moba[l], w_br_s5=w_br_s5[l], w_out=w_out[l],
                  ln1_g=ln1_g[l], ln1_b=ln1_b[l], ln2_g=ln2_g[l], ln2_b=ln2_b[l])
        sp = _s5_params(lp)
        if l % 2 == 0:
            d_ff = ffn_w_gate.shape[2]
            tf = d_ff // 2 if (d_ff // 2) % LANES == 0 else d_ff
            ffn_p = dict(wg=ffn_w_gate[l // 2], wu=ffn_w_up[l // 2], wd=ffn_w_down[l // 2], tf=tf)
        else:
            ffn_p = dict(wg=moe_w_gate[l // 2], wu=moe_w_up[l // 2], wd=moe_w_down[l // 2],
                         w_router=moe_w_router[l // 2], b_router=moe_b_router[l // 2])
        y_p, st = _layer(l, y_p, seq, 0, None, None, None, lp, sp, ffn_p)
        p_st.append(st)
        past = dict(page_table=page_table, page_base=l * n_pool, cmp_t=cmp_t, sel_t=sel_t, moba_t=moba_t,
                    win_t=win_t, win_base=l * bs, win_rows=cache_nsa_win[l].reshape(bs, -1, rw_a))
        y_s, st = _layer(l, y_s, dec_seq, past_len, past, state_s5_re[l].reshape(bs, S5_LANES),
                         state_s5_im[l].reshape(bs, S5_LANES), lp, sp, ffn_p)
        s_st.append(st)

    def stk(states, j, tail):
        return jnp.stack([st[j].reshape(st[j].shape[:2] + tail) if tail else st[j] for st in states], axis=0)

    row_a = (2, NSA_KV_HEADS, HEAD_DIM)
    row_b = (2, MOBA_KV_HEADS, HEAD_DIM)

    def s5_state(states, j, b):
        return jnp.stack([st[j].reshape(b, S5_GROUPS, S5_STATE) for st in states], axis=0)

    return (y_p, y_s[:, :dec_seq],
            stk(p_st, 0, row_a), stk(p_st, 1, row_a), stk(p_st, 2, row_a), stk(p_st, 3, row_b),
            s5_state(p_st, 4, bp), s5_state(p_st, 5, bp),
            stk(s_st, 0, row_a), stk(s_st, 1, row_a), stk(s_st, 2, row_a), stk(s_st, 3, row_b),
            s5_state(s_st, 4, bs), s5_state(s_st, 5, bs))
```
